```python
import jax, jax.numpy as jnp
from jax import lax
import numpy as np

D_MODEL = 4096
BATCH = 32
SEQ = 256
DEPTH = 1
DEC_BATCH = 4
DEC_SEQ = 4096
PAST_LEN = 512

GRID_W = 64
M_HEADS = 4
M_DK = D_MODEL // 16
M_DV = D_MODEL // 8
R_HEADS = 8
R_DK = D_MODEL // 16
R_DV = D_MODEL // 16
M_WIDTH = M_HEADS * M_DV
R_WIDTH = R_HEADS * R_DV
MIX_WIDTH = M_WIDTH + R_WIDTH
CHUNK = 128
N_GATE_ROWS = 4
IN_SIZES = (M_HEADS * M_DK, M_HEADS * M_DK, M_WIDTH, M_WIDTH, N_GATE_ROWS * M_HEADS,
            R_HEADS * R_DK, R_HEADS * R_DK, R_WIDTH, R_WIDTH)
IN_WIDTH = sum(IN_SIZES)
N_GROUPS = 4
EXPERTS_PER_GROUP = 8
N_EXPERTS = N_GROUPS * EXPERTS_PER_GROUP
TOP_K = 2
D_EXPERT = D_MODEL // 4
MOE_BLOCK = 128
ROPE_BASE = 10000.0
EPS = 1e-6

kernel_name = 'hybrid_mlstm_retention_hmoe_diffusion_step'


def _rmsnorm(x, w):
    xf = x.astype(jnp.float32)
    y = xf * lax.rsqrt(jnp.mean(xf * xf, axis=-1, keepdims=True) + EPS)
    return (y * w.astype(jnp.float32)).astype(x.dtype)


def _head_rmsnorm(h, w):
    y = h * lax.rsqrt(jnp.mean(h * h, axis=-1, keepdims=True) + EPS)
    return y * w.astype(jnp.float32).reshape(h.shape[2], h.shape[3])


def _split_cols(z, sizes):
    return jnp.split(z, np.cumsum(sizes)[:-1].tolist(), axis=-1)


def _to_chunks(a):
    b, t = a.shape[:2]
    a = a.reshape((b, t // CHUNK, CHUNK) + a.shape[2:])
    return jnp.moveaxis(jnp.moveaxis(a, 1, 0), 3, 2)


def _from_chunks(a):
    a = jnp.moveaxis(jnp.moveaxis(a, 0, 1), 2, 3)
    return a.reshape((a.shape[0], a.shape[1] * a.shape[2]) + a.shape[3:])


def _flip(a):
    return jnp.flip(a, axis=1)


def _rope_2d(x):
    t, d = x.shape[1], x.shape[-1]
    rows = t // GRID_W
    row = jnp.repeat(jnp.arange(rows, dtype=jnp.float32), GRID_W)
    col = jnp.tile(jnp.arange(GRID_W, dtype=jnp.float32), rows)
    half, quarter = d // 2, d // 4
    inv = ROPE_BASE ** (-jnp.arange(quarter, dtype=jnp.float32) / quarter)

    def rot(xa, pos):
        ang = pos[:, None] * inv[None, :]
        cos = jnp.cos(ang)[None, :, None, :]
        sin = jnp.sin(ang)[None, :, None, :]
        x1, x2 = xa[..., :quarter], xa[..., quarter:]
        return jnp.concatenate([x1 * cos - x2 * sin, x1 * sin + x2 * cos], axis=-1)

    return jnp.concatenate([rot(x[..., :half], row), rot(x[..., half:], col)], axis=-1)


def _mlstm_scan(q, k, v, ig, lf, c0, n0, m0):
    causal = jnp.tril(jnp.ones((CHUNK, CHUNK), dtype=bool))

    def step(carry, xs):
        c, n, m = carry
        qc, kc, vc, ic, fc = xs
        b = jnp.cumsum(fc, axis=-1)
        dmat = jnp.where(causal, b[..., :, None] - b[..., None, :] + ic[..., None, :], -jnp.inf)
        inter = b + m[..., None]
        mt = jnp.maximum(jnp.max(dmat, axis=-1), inter)
        s = jnp.einsum('bhtd,bhsd->bhts', qc, kc) * jnp.exp(dmat - mt[..., None])
        wi = jnp.exp(inter - mt)
        num = jnp.einsum('bhts,bhsv->bhtv', s, vc) + wi[..., None] * jnp.einsum('bhtd,bhdv->bhtv', qc, c)
        den = jnp.sum(s, axis=-1) + wi * jnp.einsum('bhtd,bhd->bht', qc, n)
        h = num / jnp.maximum(jnp.abs(den), jnp.exp(-mt))[..., None]
        bl = b[..., -1]
        g = bl[..., None] - b + ic
        m_new = jnp.maximum(bl + m, jnp.max(g, axis=-1))
        a_prev = jnp.exp(bl + m - m_new)
        a_tok = jnp.exp(g - m_new[..., None])[..., None] * kc
        c_new = a_prev[..., None, None] * c + jnp.einsum('bhsd,bhsv->bhdv', a_tok, vc)
        n_new = a_prev[..., None] * n + jnp.sum(a_tok, axis=2)
        return (c_new, n_new, m_new), h

    carry0 = (c0.astype(jnp.float32), n0.astype(jnp.float32), m0.astype(jnp.float32))
    xs = (_to_chunks(q), _to_chunks(k), _to_chunks(v), _to_chunks(ig), _to_chunks(lf))
    (c, n, m), hs = lax.scan(step, carry0, xs)
    return _from_chunks(hs), (c, n, m)


def _retention_scan(q, k, v, log_decay, s0):
    idx = jnp.arange(CHUNK, dtype=jnp.float32)
    diff = idx[:, None] - idx[None, :]
    intra = jnp.where(diff >= 0, jnp.exp(jnp.maximum(diff, 0.0)[None] * log_decay[:, None, None]), 0.0)
    q_dec = jnp.exp((idx + 1.0)[None, :] * log_decay[:, None])[..., None]
    k_dec = jnp.exp((CHUNK - 1.0 - idx)[None, :] * log_decay[:, None])[..., None]
    s_dec = jnp.exp(CHUNK * log_decay)[:, None, None]

    def step(s, xs):
        qc, kc, vc = xs
        a = jnp.einsum('bhtd,bhsd->bhts', qc, kc) * intra
        o = jnp.einsum('bhts,bhsv->bhtv', a, vc) + jnp.einsum('bhtd,bhdv->bhtv', qc, s) * q_dec
        s_new = s_dec * s + jnp.einsum('bhsd,bhsv->bhdv', kc * k_dec, vc)
        return s_new, o

    s, outs = lax.scan(step, s0.astype(jnp.float32), (_to_chunks(q), _to_chunks(k), _to_chunks(v)))
    return _from_chunks(outs), s


def _token_mixer(h, st_c, st_n, st_m, st_s, use_grid_pos, w_in, b_mgates, mlstm_norm_w, ret_norm_w,
                 ret_log_decay, w_out):
    bsz, t, _ = h.shape
    z = (h @ w_in).astype(jnp.float32)
    mq, mk, mv, mo, mg, rq, rk, rv, rg = _split_cols(z, IN_SIZES)

    mq = mq.reshape(bsz, t, M_HEADS, M_DK)
    mk = mk.reshape(bsz, t, M_HEADS, M_DK) * (M_DK ** -0.5)
    mv = mv.reshape(bsz, t, M_HEADS, M_DV)
    gates = mg.reshape(bsz, t, N_GATE_ROWS, M_HEADS) + b_mgates.astype(jnp.float32)
    i_f, lf_f = gates[:, :, 0], jax.nn.log_sigmoid(gates[:, :, 1])
    i_b, lf_b = gates[:, :, 2], jax.nn.log_sigmoid(gates[:, :, 3])
    hf, (cf, nf, mf) = _mlstm_scan(mq, mk, mv, i_f, lf_f, st_c[:, 0], st_n[:, 0], st_m[:, 0])
    hb, (cb, nb, mb) = _mlstm_scan(_flip(mq), _flip(mk), _flip(mv), _flip(i_b), _flip(lf_b),
                                   st_c[:, 1], st_n[:, 1], st_m[:, 1])
    h_m = _head_rmsnorm(hf + _flip(hb), mlstm_norm_w) * jax.nn.sigmoid(mo).reshape(bsz, t, M_HEADS, M_DV)

    rq = rq.reshape(bsz, t, R_HEADS, R_DK)
    rk = rk.reshape(bsz, t, R_HEADS, R_DK)
    if use_grid_pos:
        rq, rk = _rope_2d(rq), _rope_2d(rk)
    rk = rk * (R_DK ** -0.5)
    rv = rv.reshape(bsz, t, R_HEADS, R_DV)
    log_decay = -jnp.exp(ret_log_decay.astype(jnp.float32))
    of, sf = _retention_scan(rq, rk, rv, log_decay[0], st_s[:, 0])
    ob, sb = _retention_scan(_flip(rq), _flip(rk), _flip(rv), log_decay[1], st_s[:, 1])
    h_r = _head_rmsnorm(of + _flip(ob), ret_norm_w) * jax.nn.silu(rg).reshape(bsz, t, R_HEADS, R_DV)

    mixed = jnp.concatenate([h_m.reshape(bsz, t, M_WIDTH), h_r.reshape(bsz, t, R_WIDTH)], axis=-1)
    out = mixed.astype(h.dtype) @ w_out
    states = (jnp.stack([cf, cb], axis=1), jnp.stack([nf, nb], axis=1),
              jnp.stack([mf, mb], axis=1), jnp.stack([sf, sb], axis=1))
    return out, states


def _hier_moe(h, w_rg, b_rg, w_re, b_re, w_eg, w_eu, w_ed):
    t, d = h.shape
    hf = h.astype(jnp.float32)
    g_prob = jax.nn.softmax(hf @ w_rg.astype(jnp.float32) + b_rg.astype(jnp.float32), axis=-1)
    g_idx = jnp.argmax(g_prob, axis=-1)
    g_w = jnp.take_along_axis(g_prob, g_idx[:, None], axis=-1)[:, 0]
    e_logit = (hf @ w_re.astype(jnp.float32) + b_re.astype(jnp.float32)).reshape(t, N_GROUPS, EXPERTS_PER_GROUP)
    e_logit = jnp.take_along_axis(e_logit, g_idx[:, None, None], axis=1)[:, 0]
    top_p, top_i = lax.top_k(jax.nn.softmax(e_logit, axis=-1), TOP_K)
    top_p = top_p / jnp.sum(top_p, axis=-1, keepdims=True)
    eid = (g_idx[:, None] * EXPERTS_PER_GROUP + top_i).reshape(-1).astype(jnp.int32)
    wt = (g_w[:, None] * top_p).reshape(-1)
    tok = jnp.repeat(jnp.arange(t, dtype=jnp.int32), TOP_K)
    n_assign = t * TOP_K

    order = jnp.argsort(eid)
    eid_s, tok_s, wt_s = eid[order], tok[order], wt[order]
    counts = jnp.bincount(eid, length=N_EXPERTS)
    padded = ((counts + MOE_BLOCK - 1) // MOE_BLOCK) * MOE_BLOCK
    pad_end = jnp.cumsum(padded)
    pad_start = pad_end - padded
    start = jnp.cumsum(counts) - counts
    dest = pad_start[eid_s] + jnp.arange(n_assign, dtype=jnp.int32) - start[eid_s]
    n_blocks = -(-n_assign // MOE_BLOCK) + N_EXPERTS
    n_slots = n_blocks * MOE_BLOCK
    slot_tok = jnp.full((n_slots,), t, dtype=jnp.int32).at[dest].set(tok_s)
    slot_w = jnp.zeros((n_slots,), jnp.float32).at[dest].set(wt_s)
    block_e = jnp.minimum(jnp.searchsorted(pad_end, jnp.arange(n_blocks) * MOE_BLOCK, side='right'),
                          N_EXPERTS - 1)
    h_pad = jnp.concatenate([h, jnp.zeros((1, d), h.dtype)], axis=0)

    def expert_block(args):
        tok_b, e = args
        xb = h_pad[tok_b]
        return (jax.nn.silu(xb @ w_eg[e]) * (xb @ w_eu[e])) @ w_ed[e]

    yb = lax.map(expert_block, (slot_tok.reshape(n_blocks, MOE_BLOCK), block_e))
    out = jnp.zeros((t + 1, d), jnp.float32).at[slot_tok].add(
        yb.reshape(n_slots, d).astype(jnp.float32) * slot_w[:, None])
    return out[:t].astype(h.dtype)


def _trunk_layer(x, cmod, st_c, st_n, st_m, st_s, use_grid_pos, norm1_w, norm2_w, w_in, b_mgates,
                 mlstm_norm_w, ret_norm_w, ret_log_decay, w_out, w_rg, b_rg, w_re, b_re, w_eg, w_eu, w_ed):
    shift1, scale1, gate1, shift2, scale2, gate2 = jnp.split(cmod[:, None, :], 6, axis=-1)
    h = _rmsnorm(x, norm1_w) * (1.0 + scale1) + shift1
    mix, states = _token_mixer(h, st_c, st_n, st_m, st_s, use_grid_pos, w_in, b_mgates, mlstm_norm_w,
                               ret_norm_w, ret_log_decay, w_out)
    x = x + gate1 * mix
    h = _rmsnorm(x, norm2_w) * (1.0 + scale2) + shift2
    bsz, t, d = h.shape
    x = x + gate2 * _hier_moe(h.reshape(bsz * t, d), w_rg, b_rg, w_re, b_re, w_eg, w_eu, w_ed).reshape(bsz, t, d)
    return x, states


def setup_inputs(seed: int = 0) -> dict:
    key = jax.random.key(seed)
    ks = jax.random.split(key, 26)
    f32 = jnp.float32

    def nrm(k, shape, scale):
        return jax.random.normal(k, shape, f32) * scale

    gate_base = jnp.array([-1.0, 3.0, -1.0, 3.0], dtype=f32)[None, :, None]
    decay_base = jnp.log(-jnp.log1p(-(2.0 ** (-5.0 - jnp.arange(R_HEADS, dtype=f32)))))
    return {
        'x_prompt': nrm(ks[0], (BATCH, SEQ, D_MODEL), 1.0),
        'x_sample': nrm(ks[1], (DEC_BATCH, DEC_SEQ, D_MODEL), 1.0),
        'state_mlstm_C': nrm(ks[2], (DEC_BATCH, DEPTH, 2, M_HEADS, M_DK, M_DV), 0.05),
        'state_mlstm_n': nrm(ks[3], (DEC_BATCH, DEPTH, 2, M_HEADS, M_DK), 0.1),
        'state_mlstm_m': 1.0 + nrm(ks[4], (DEC_BATCH, DEPTH, 2, M_HEADS), 0.5),
        'state_ret_S': nrm(ks[5], (DEC_BATCH, DEPTH, 2, R_HEADS, R_DK, R_DV), 0.1),
        'c': nrm(ks[6], (DEC_BATCH, D_MODEL), 1.0),
        'c_ctx': nrm(ks[7], (D_MODEL,), 1.0),
        'norm1_w': 1.0 + nrm(ks[8], (DEPTH, D_MODEL), 0.02),
        'norm2_w': 1.0 + nrm(ks[9], (DEPTH, D_MODEL), 0.02),
        'w_ada': nrm(ks[10], (DEPTH, D_MODEL, 6 * D_MODEL), 0.5 * D_MODEL ** -0.5),
        'b_ada': nrm(ks[11], (DEPTH, 6 * D_MODEL), 0.01),
        'w_in': nrm(ks[12], (DEPTH, D_MODEL, IN_WIDTH), D_MODEL ** -0.5),
        'b_mgates': gate_base + nrm(ks[13], (DEPTH, N_GATE_ROWS, M_HEADS), 0.1),
        'mlstm_norm_w': 1.0 + nrm(ks[14], (DEPTH, M_WIDTH), 0.02),
        'ret_norm_w': 1.0 + nrm(ks[15], (DEPTH, R_WIDTH), 0.02),
        'ret_log_decay': decay_base + nrm(ks[16], (DEPTH, 2, R_HEADS), 0.05),
        'w_out': nrm(ks[17], (DEPTH, MIX_WIDTH, D_MODEL), MIX_WIDTH ** -0.5),
        'w_router_group': nrm(ks[18], (DEPTH, D_MODEL, N_GROUPS), D_MODEL ** -0.5),
        'b_router_group': nrm(ks[19], (DEPTH, N_GROUPS), 0.01),
        'w_router_expert': nrm(ks[20], (DEPTH, D_MODEL, N_EXPERTS), D_MODEL ** -0.5),
        'b_router_expert': nrm(ks[21], (DEPTH, N_EXPERTS), 0.01),
        'w_exp_gate': nrm(ks[22], (DEPTH, N_EXPERTS, D_MODEL, D_EXPERT), D_MODEL ** -0.5),
        'w_exp_up': nrm(ks[23], (DEPTH, N_EXPERTS, D_MODEL, D_EXPERT), D_MODEL ** -0.5),
        'w_exp_down': nrm(ks[24], (DEPTH, N_EXPERTS, D_EXPERT, D_MODEL), D_EXPERT ** -0.5),
        'final_norm_w': 1.0 + nrm(ks[25], (D_MODEL,), 0.02),
    }


def reference(x_prompt, x_sample, state_mlstm_C, state_mlstm_n, state_mlstm_m, state_ret_S, c, c_ctx,
              norm1_w, norm2_w, w_ada, b_ada, w_in, b_mgates, mlstm_norm_w, ret_norm_w, ret_log_decay, w_out,
              w_router_group, b_router_group, w_router_expert, b_router_expert, w_exp_gate, w_exp_up,
              w_exp_down, final_norm_w):
    bsz = x_prompt.shape[0]
    f32 = jnp.float32
    zero_c = jnp.zeros((bsz, 2, M_HEADS, M_DK, M_DV), f32)
    zero_n = jnp.zeros((bsz, 2, M_HEADS, M_DK), f32)
    zero_m = jnp.zeros((bsz, 2, M_HEADS), f32)
    zero_s = jnp.zeros((bsz, 2, R_HEADS, R_DK, R_DV), f32)
    x_ctx, x_lat = x_prompt, x_sample
    list_c, list_n, list_m, list_s = [], [], [], []
    for l in range(DEPTH):
        layer_w = (norm1_w[l], norm2_w[l], w_in[l], b_mgates[l], mlstm_norm_w[l], ret_norm_w[l],
                   ret_log_decay[l], w_out[l], w_router_group[l], b_router_group[l], w_router_expert[l],
                   b_router_expert[l], w_exp_gate[l], w_exp_up[l], w_exp_down[l])
        mod_ctx = jax.nn.silu(c_ctx)[None, :] @ w_ada[l] + b_ada[l]
        mod_lat = jax.nn.silu(c) @ w_ada[l] + b_ada[l]
        x_ctx, (s_c, s_n, s_m, s_s) = _trunk_layer(x_ctx, mod_ctx, zero_c, zero_n, zero_m, zero_s, False, *layer_w)
        list_c.append(s_c)
        list_n.append(s_n)
        list_m.append(s_m)
        list_s.append(s_s)
        x_lat, _ = _trunk_layer(x_lat, mod_lat, state_mlstm_C[:, l], state_mlstm_n[:, l], state_mlstm_m[:, l],
                                state_ret_S[:, l], True, *layer_w)
    y_prompt = _rmsnorm(x_ctx, final_norm_w)
    y_sample = _rmsnorm(x_lat, final_norm_w)
    new_mlstm_C = jnp.stack(list_c, axis=1)
    new_mlstm_n = jnp.stack(list_n, axis=1)
    new_mlstm_m = jnp.stack(list_m, axis=1)
    new_ret_S = jnp.stack(list_s, axis=1)
    return (y_prompt, y_sample, new_mlstm_C, new_mlstm_n, new_mlstm_m, new_ret_S)
```

```python
import functools
import math

import numpy as np
import jax
import jax.numpy as jnp
from jax import lax
from jax.experimental import pallas as pl
from jax.experimental.pallas import tpu as pltpu

F32 = jnp.float32
BF16 = jnp.bfloat16
I32 = jnp.int32

CHUNK = 128
GRID_W = 64
ROPE_BASE = 10000.0
EPS = 1e-6
TOP_K = 2
N_GATE_ROWS = 4

LANES = 128
SUBLANES = 8
VMEM_LIMIT_BYTES = 56 * 1024 * 1024
NEG_BIG = -1e30
MOE_TILE = 256


def _divisor(n, pref, mult=SUBLANES):
    if n <= pref:
        return n
    d = (pref // mult) * mult
    while d > mult and n % d:
        d -= mult
    assert n % d == 0, (n, pref, mult)
    return d


def _cparams(*sem):
    return pltpu.CompilerParams(dimension_semantics=sem, vmem_limit_bytes=VMEM_LIMIT_BYTES)


def _dot(a, b):
    return jnp.dot(a, b, preferred_element_type=F32)


def _dot_nt(a, b):
    return lax.dot_general(a, b, (((1,), (1,)), ((), ())), preferred_element_type=F32)


def _dot_tn(a, b):
    return lax.dot_general(a, b, (((0,), (0,)), ((), ())), preferred_element_type=F32)


def _split_bf16(x):
    hi = x.astype(BF16)
    lo = (x - hi.astype(F32)).astype(BF16)
    return hi, lo


def _sigmoid(x):
    return 1.0 / (1.0 + jnp.exp(-x))


def _log_sigmoid(x):
    return jnp.minimum(x, 0.0) - jnp.log(1.0 + jnp.exp(-jnp.abs(x)))


def _mod_row(i, tm, n_ctx_rows, t_lat):
    r0 = i * tm
    return jnp.where(r0 < n_ctx_rows, 0, 1 + (r0 - n_ctx_rows) // t_lat)


def _ada_kernel(c_ref, w_ref, b_ref, o_ref):
    c = c_ref[...]
    s = c * _sigmoid(c)
    s_hi, s_lo = _split_bf16(s)
    w_hi, w_lo = _split_bf16(w_ref[...])
    o_ref[...] = _dot(s_hi, w_hi) + _dot(s_lo, w_hi) + _dot(s_hi, w_lo) + b_ref[...]


def _ada_mod(cvec, w_ada, b_ada):
    r, d = cvec.shape
    n = w_ada.shape[1]
    tn = _divisor(n, 256, LANES)
    return pl.pallas_call(
        _ada_kernel,
        grid=(n // tn,),
        in_specs=[pl.BlockSpec((r, d), lambda j: (0, 0)),
                  pl.BlockSpec((d, tn), lambda j: (0, j)),
                  pl.BlockSpec((1, tn), lambda j: (0, j))],
        out_specs=pl.BlockSpec((r, tn), lambda j: (0, j)),
        out_shape=jax.ShapeDtypeStruct((r, n), F32),
        compiler_params=_cparams("parallel"),
        name="ada_mod",
    )(cvec, w_ada, b_ada.reshape(1, n))


def _norm_mod_kernel(x_ref, w_ref, mod_ref, o_ref, *, shift_idx, scale_idx):
    x = x_ref[...]
    y = x * lax.rsqrt(jnp.mean(x * x, axis=-1, keepdims=True) + EPS) * w_ref[...]
    y = y * (1.0 + mod_ref[0, scale_idx:scale_idx + 1, :]) + mod_ref[0, shift_idx:shift_idx + 1, :]
    o_ref[...] = y.astype(o_ref.dtype)


def _norm_mod(x, w, mod, n_ctx_rows, t_lat, shift_idx, scale_idx, out_dtype):
    m, d = x.shape
    tm = _divisor(math.gcd(n_ctx_rows, t_lat), 256)
    mrow = functools.partial(_mod_row, tm=tm, n_ctx_rows=n_ctx_rows, t_lat=t_lat)
    return pl.pallas_call(
        functools.partial(_norm_mod_kernel, shift_idx=shift_idx, scale_idx=scale_idx),
        grid=(m // tm,),
        in_specs=[pl.BlockSpec((tm, d), lambda i: (i, 0)),
                  pl.BlockSpec((1, d), lambda i: (0, 0)),
                  pl.BlockSpec((1, mod.shape[1], d), lambda i: (mrow(i), 0, 0))],
        out_specs=pl.BlockSpec((tm, d), lambda i: (i, 0)),
        out_shape=jax.ShapeDtypeStruct((m, d), out_dtype),
        compiler_params=_cparams("parallel"),
        name="norm_mod",
    )(x, w.reshape(1, d), mod)


def _inproj_kernel(a_ref, b_ref, wg_ref, z_ref, zg_ref):
    a = a_ref[...]
    z_ref[...] = _dot(a, b_ref[...])

    @pl.when(pl.program_id(1) == 0)
    def _gates():
        zg_ref[...] = _dot(a, wg_ref[...])


def _inproj(h, w_main, w_gate):
    m, k = h.shape
    n = w_main.shape[1]
    tm = _divisor(m, 1024)
    tn = _divisor(n, 1024, LANES)
    return pl.pallas_call(
        _inproj_kernel,
        grid=(m // tm, n // tn),
        in_specs=[pl.BlockSpec((tm, k), lambda i, j: (i, 0)),
                  pl.BlockSpec((k, tn), lambda i, j: (0, j)),
                  pl.BlockSpec((k, LANES), lambda i, j: (0, 0))],
        out_specs=[pl.BlockSpec((tm, tn), lambda i, j: (i, j)),
                   pl.BlockSpec((tm, LANES), lambda i, j: (i, 0))],
        out_shape=[jax.ShapeDtypeStruct((m, n), F32),
                   jax.ShapeDtypeStruct((m, LANES), F32)],
        compiler_params=_cparams("parallel", "arbitrary"),
        name="in_proj",
    )(h, w_main, w_gate)


def _scan_schedule(n_ctx, t_ctx, n_lat, t_lat, reverse):
    rowblk, first, last, ctxb, latb, islat, posblk = [], [], [], [], [], [], []
    nc_ctx, nc_lat = t_ctx // CHUNK, t_lat // CHUNK
    for b in range(n_ctx):
        order = range(nc_ctx - 1, -1, -1) if reverse else range(nc_ctx)
        for pos, c in enumerate(order):
            rowblk.append(b * nc_ctx + c)
            first.append(int(pos == 0))
            last.append(int(pos == nc_ctx - 1))
            ctxb.append(b)
            latb.append(0)
            islat.append(0)
            posblk.append(nc_lat)
    base = n_ctx * nc_ctx
    for b in range(n_lat):
        order = range(nc_lat - 1, -1, -1) if reverse else range(nc_lat)
        for pos, c in enumerate(order):
            rowblk.append(base + b * nc_lat + c)
            first.append(int(pos == 0))
            last.append(int(pos == nc_lat - 1))
            ctxb.append(n_ctx - 1)
            latb.append(b)
            islat.append(1)
            posblk.append(c)
    tabs = (rowblk, first, last, ctxb, latb, islat, posblk)
    return tuple(jnp.asarray(np.asarray(t, np.int32)) for t in tabs)


def _mlstm_kernel(rowblk, first, last, ctxb, latb, islat, posblk,
                  q_ref, k_ref, v_ref, g_ref, gb_ref, c0_ref, n0_ref, m0_ref, *rest,
                  heads, dk, dv, reverse, combine):
    if combine:
        hf_ref, o_ref, nw_ref, out_ref, cout_ref, nout_ref, mout_ref, c_s, n_s, m_s = rest
    else:
        out_ref, cout_ref, nout_ref, mout_ref, c_s, n_s, m_s = rest
    w = pl.program_id(0)
    lat = islat[w] == 1

    @pl.when(first[w] == 1)
    def _init():
        c_s[...] = jnp.where(lat, c0_ref[0, 0], 0.0)
        n_s[...] = jnp.where(lat, n0_ref[0, 0], 0.0)
        m_s[...] = jnp.where(lat, m0_ref[0, 0], 0.0)

    L = CHUNK
    ti = lax.broadcasted_iota(I32, (L, L), 0)
    si = lax.broadcasted_iota(I32, (L, L), 1)
    mask = (si >= ti) if reverse else (si <= ti)
    mask_t = (ti >= si) if reverse else (ti <= si)
    scale = dk ** -0.5
    gate_row = 2 if reverse else 0

    g = g_ref[...] + gb_ref[...]
    gt = g.T
    lg = _log_sigmoid(g)
    lgt = _log_sigmoid(gt)

    for h in range(heads):
        ci = gate_row * heads + h
        cf = (gate_row + 1) * heads + h
        i_col, i_row = g[:, ci:ci + 1], gt[ci:ci + 1, :]
        lf_col, lf_row = lg[:, cf:cf + 1], lgt[cf:cf + 1, :]
        b_col = jnp.sum(jnp.where(mask, lf_row, 0.0), axis=1, keepdims=True)
        b_row = jnp.sum(jnp.where(mask_t, lf_col, 0.0), axis=0, keepdims=True)
        bl = jnp.sum(lf_row, axis=1, keepdims=True)
        m_old = m_s[h:h + 1, 0:1]

        dm = jnp.where(mask, b_col - b_row + i_row, NEG_BIG)
        inter = b_col + m_old
        mt = jnp.maximum(jnp.max(dm, axis=1, keepdims=True), inter)
        p = jnp.exp(dm - mt)
        wi = jnp.exp(inter - mt)

        q = q_ref[:, h * dk:(h + 1) * dk]
        k = k_ref[:, h * dk:(h + 1) * dk]
        qb, kb = q.astype(BF16), k.astype(BF16)
        vb = v_ref[:, h * dv:(h + 1) * dv].astype(BF16)
        c_old = c_s[h]
        n_old = n_s[h:h + 1, :]

        s = _dot_nt(qb, kb) * (scale * p)
        num = _dot(s.astype(BF16), vb) + wi * _dot(qb, c_old.astype(BF16))
        den = jnp.sum(s, axis=1, keepdims=True) + wi * jnp.sum(q * n_old, axis=1, keepdims=True)
        hval = num / jnp.maximum(jnp.abs(den), jnp.exp(-mt))

        g_col = bl - b_col + i_col
        g_row = bl - b_row + i_row
        m_new = jnp.maximum(bl + m_old, jnp.max(g_row, axis=1, keepdims=True))
        a_prev = jnp.exp(bl + m_old - m_new)
        ak = (jnp.exp(g_col - m_new) * scale) * k
        c_s[h] = a_prev * c_old + _dot_tn(ak.astype(BF16), vb)
        n_s[h:h + 1, :] = a_prev * n_old + jnp.sum(ak, axis=0, keepdims=True)
        m_s[h:h + 1, :] = jnp.broadcast_to(m_new, (1, LANES))

        if combine:
            hs = hval + hf_ref[:, h * dv:(h + 1) * dv]
            y = hs * lax.rsqrt(jnp.mean(hs * hs, axis=-1, keepdims=True) + EPS)
            y = y * nw_ref[:, h * dv:(h + 1) * dv] * _sigmoid(o_ref[:, h * dv:(h + 1) * dv])
            out_ref[:, h * dv:(h + 1) * dv] = y.astype(out_ref.dtype)
        else:
            out_ref[:, h * dv:(h + 1) * dv] = hval

    @pl.when(jnp.logical_and(last[w] == 1, jnp.logical_not(lat)))
    def _emit():
        cout_ref[0] = c_s[...]
        nout_ref[0] = n_s[...]
        mout_ref[0] = m_s[...]


def _mlstm_scan(z, zg, gbias, st_c, st_n, st_m, dims, reverse, hf=None, norm_w=None):
    (n_ctx, t_ctx, n_lat, t_lat, heads, dk, dv) = dims
    combine = hf is not None
    m = z.shape[0]
    d = 1 if reverse else 0
    tabs = _scan_schedule(n_ctx, t_ctx, n_lat, t_lat, reverse)
    qw, vw = heads * dk, heads * dv
    assert (2 * qw) % vw == 0
    v_blk = (2 * qw) // vw
    in_specs = [
        pl.BlockSpec((CHUNK, qw), lambda w, rb, *_: (rb[w], 0)),
        pl.BlockSpec((CHUNK, qw), lambda w, rb, *_: (rb[w], 1)),
        pl.BlockSpec((CHUNK, vw), lambda w, rb, *_: (rb[w], v_blk)),
        pl.BlockSpec((CHUNK, LANES), lambda w, rb, *_: (rb[w], 0)),
        pl.BlockSpec((1, LANES), lambda w, *_: (0, 0)),
        pl.BlockSpec((1, 1, heads, dk, dv), lambda w, rb, f, l, cb, lb, *_: (lb[w], d, 0, 0, 0)),
        pl.BlockSpec((1, 1, heads, dk), lambda w, rb, f, l, cb, lb, *_: (lb[w], d, 0, 0)),
        pl.BlockSpec((1, 1, heads, LANES), lambda w, rb, f, l, cb, lb, *_: (lb[w], d, 0, 0)),
    ]
    args = [z, z, z, zg, gbias, st_c, st_n, st_m]
    if combine:
        in_specs += [
            pl.BlockSpec((CHUNK, vw), lambda w, rb, *_: (rb[w], 0)),
            pl.BlockSpec((CHUNK, vw), lambda w, rb, *_: (rb[w], v_blk + 1)),
            pl.BlockSpec((1, vw), lambda w, *_: (0, 0)),
        ]
        args += [hf, z, norm_w.reshape(1, vw)]
    out_dtype = BF16 if combine else F32
    out_specs = [
        pl.BlockSpec((CHUNK, vw), lambda w, rb, *_: (rb[w], 0)),
        pl.BlockSpec((1, heads, dk, dv), lambda w, rb, f, l, cb, *_: (cb[w], 0, 0, 0)),
        pl.BlockSpec((1, heads, dk), lambda w, rb, f, l, cb, *_: (cb[w], 0, 0)),
        pl.BlockSpec((1, heads, LANES), lambda w, rb, f, l, cb, *_: (cb[w], 0, 0)),
    ]
    out_shape = [
        jax.ShapeDtypeStruct((m, vw), out_dtype),
        jax.ShapeDtypeStruct((n_ctx, heads, dk, dv), F32),
        jax.ShapeDtypeStruct((n_ctx, heads, dk), F32),
        jax.ShapeDtypeStruct((n_ctx, heads, LANES), F32),
    ]
    return pl.pallas_call(
        functools.partial(_mlstm_kernel, heads=heads, dk=dk, dv=dv, reverse=reverse, combine=combine),
        grid_spec=pltpu.PrefetchScalarGridSpec(
            num_scalar_prefetch=len(tabs), grid=(m // CHUNK,),
            in_specs=in_specs, out_specs=out_specs,
            scratch_shapes=[pltpu.VMEM((heads, dk, dv), F32),
                            pltpu.VMEM((heads, dk), F32),
                            pltpu.VMEM((heads, LANES), F32)]),
        out_shape=out_shape,
        compiler_params=_cparams("arbitrary"),
        name="mlstm_bwd" if reverse else "mlstm_fwd",
    )(*tabs, *args)


def _rope_partner(x, d):
    q, hlf = d // 4, d // 2
    return jnp.concatenate([x[:, q:hlf], x[:, :q], x[:, hlf + q:], x[:, hlf:hlf + q]], axis=1)


def _ret_kernel(rowblk, first, last, ctxb, latb, islat, posblk,
                q_ref, k_ref, v_ref, cos_ref, sin_ref, ld_ref, s0_ref, *rest,
                heads, dk, dv, reverse, combine):
    if combine:
        of_ref, g_ref, nw_ref, out_ref, sout_ref, s_s = rest
    else:
        out_ref, sout_ref, s_s = rest
    w = pl.program_id(0)
    lat = islat[w] == 1

    @pl.when(first[w] == 1)
    def _init():
        s_s[...] = jnp.where(lat, s0_ref[0, 0], 0.0)

    L = CHUNK
    ti = lax.broadcasted_iota(I32, (L, L), 0).astype(F32)
    si = lax.broadcasted_iota(I32, (L, L), 1).astype(F32)
    diff = (si - ti) if reverse else (ti - si)
    idx = lax.broadcasted_iota(I32, (L, 1), 0).astype(F32)
    q_pow = (L - idx) if reverse else (idx + 1.0)
    k_pow = idx if reverse else (L - 1.0 - idx)
    scale = dk ** -0.5
    drow = 1 if reverse else 0
    log_decay = -jnp.exp(ld_ref[drow:drow + 1, :])
    cos_t, sin_t = cos_ref[...], sin_ref[...]

    for h in range(heads):
        ld = log_decay[:, h:h + 1]
        intra = jnp.where(diff >= 0.0, jnp.exp(jnp.maximum(diff, 0.0) * ld), 0.0)
        q_dec = jnp.exp(q_pow * ld)
        k_dec = jnp.exp(k_pow * ld)
        s_dec = jnp.exp(float(L) * ld)

        q = q_ref[:, h * dk:(h + 1) * dk]
        k = k_ref[:, h * dk:(h + 1) * dk]
        q = q * cos_t + _rope_partner(q, dk) * sin_t
        k = k * cos_t + _rope_partner(k, dk) * sin_t
        qb, kb = q.astype(BF16), k.astype(BF16)
        vb = v_ref[:, h * dv:(h + 1) * dv].astype(BF16)
        s_old = s_s[h]

        a = _dot_nt(qb, kb) * (scale * intra)
        o = _dot(a.astype(BF16), vb) + _dot(qb, s_old.astype(BF16)) * q_dec
        s_s[h] = s_dec * s_old + _dot_tn((k * (scale * k_dec)).astype(BF16), vb)

        if combine:
            hs = o + of_ref[:, h * dv:(h + 1) * dv]
            y = hs * lax.rsqrt(jnp.mean(hs * hs, axis=-1, keepdims=True) + EPS)
            gate = g_ref[:, h * dv:(h + 1) * dv]
            y = y * nw_ref[:, h * dv:(h + 1) * dv] * (gate * _sigmoid(gate))
            out_ref[:, h * dv:(h + 1) * dv] = y.astype(out_ref.dtype)
        else:
            out_ref[:, h * dv:(h + 1) * dv] = o

    @pl.when(jnp.logical_and(last[w] == 1, jnp.logical_not(lat)))
    def _emit():
        sout_ref[0] = s_s[...]


def _ret_scan(z, cos_tab, sin_tab, ld, st_s, dims, col0, reverse, of=None, norm_w=None):
    (n_ctx, t_ctx, n_lat, t_lat, heads, dk, dv) = dims
    combine = of is not None
    m = z.shape[0]
    d = 1 if reverse else 0
    tabs = _scan_schedule(n_ctx, t_ctx, n_lat, t_lat, reverse)
    qw, vw = heads * dk, heads * dv
    assert qw == vw and col0 % qw == 0
    b0 = col0 // qw
    in_specs = [
        pl.BlockSpec((CHUNK, qw), lambda w, rb, *_: (rb[w], b0)),
        pl.BlockSpec((CHUNK, qw), lambda w, rb, *_: (rb[w], b0 + 1)),
        pl.BlockSpec((CHUNK, vw), lambda w, rb, *_: (rb[w], b0 + 2)),
        pl.BlockSpec((CHUNK, dk), lambda w, rb, f, l, cb, lb, il, pb: (pb[w], 0)),
        pl.BlockSpec((CHUNK, dk), lambda w, rb, f, l, cb, lb, il, pb: (pb[w], 0)),
        pl.BlockSpec((SUBLANES, LANES), lambda w, *_: (0, 0)),
        pl.BlockSpec((1, 1, heads, dk, dv), lambda w, rb, f, l, cb, lb, *_: (lb[w], d, 0, 0, 0)),
    ]
    args = [z, z, z, cos_tab, sin_tab, ld, st_s]
    if combine:
        in_specs += [
            pl.BlockSpec((CHUNK, vw), lambda w, rb, *_: (rb[w], 0)),
            pl.BlockSpec((CHUNK, vw), lambda w, rb, *_: (rb[w], b0 + 3)),
            pl.BlockSpec((1, vw), lambda w, *_: (0, 0)),
        ]
        args += [of, z, norm_w.reshape(1, vw)]
    out_dtype = BF16 if combine else F32
    return pl.pallas_call(
        functools.partial(_ret_kernel, heads=heads, dk=dk, dv=dv, reverse=reverse, combine=combine),
        grid_spec=pltpu.PrefetchScalarGridSpec(
            num_scalar_prefetch=len(tabs), grid=(m // CHUNK,),
            in_specs=in_specs,
            out_specs=[pl.BlockSpec((CHUNK, vw), lambda w, rb, *_: (rb[w], 0)),
                       pl.BlockSpec((1, heads, dk, dv), lambda w, rb, f, l, cb, *_: (cb[w], 0, 0, 0))],
            scratch_shapes=[pltpu.VMEM((heads, dk, dv), F32)]),
        out_shape=[jax.ShapeDtypeStruct((m, vw), out_dtype),
                   jax.ShapeDtypeStruct((n_ctx, heads, dk, dv), F32)],
        compiler_params=_cparams("arbitrary"),
        name="ret_bwd" if reverse else "ret_fwd",
    )(*tabs, *args)


def _rope_tables(t_lat, d):
    quarter = d // 4
    rows = t_lat // GRID_W
    row = jnp.repeat(jnp.arange(rows, dtype=F32), GRID_W)
    col = jnp.tile(jnp.arange(GRID_W, dtype=F32), rows)
    inv = ROPE_BASE ** (-jnp.arange(quarter, dtype=F32) / quarter)
    ar = row[:, None] * inv[None, :]
    ac = col[:, None] * inv[None, :]
    cos_t = jnp.concatenate([jnp.cos(ar), jnp.cos(ar), jnp.cos(ac), jnp.cos(ac)], axis=1)
    sin_t = jnp.concatenate([-jnp.sin(ar), jnp.sin(ar), -jnp.sin(ac), jnp.sin(ac)], axis=1)
    cos_t = jnp.concatenate([cos_t, jnp.ones((CHUNK, d), F32)], axis=0)
    sin_t = jnp.concatenate([sin_t, jnp.zeros((CHUNK, d), F32)], axis=0)
    return cos_t, sin_t


def _outproj_kernel(a1_ref, a2_ref, b1_ref, b2_ref, x_ref, mod_ref, o_ref, *, gate_idx):
    acc = _dot(a1_ref[...], b1_ref[...]) + _dot(a2_ref[...], b2_ref[...])
    o_ref[...] = x_ref[...] + mod_ref[0, gate_idx:gate_idx + 1, :] * acc


def _outproj(a1, a2, w_out, x, mod, n_ctx_rows, t_lat, gate_idx):
    m, k1 = a1.shape
    k2 = a2.shape[1]
    n = w_out.shape[1]
    assert k1 == k2
    tm = _divisor(math.gcd(n_ctx_rows, t_lat), 1024)
    tn = _divisor(n, 512, LANES)
    mrow = functools.partial(_mod_row, tm=tm, n_ctx_rows=n_ctx_rows, t_lat=t_lat)
    return pl.pallas_call(
        functools.partial(_outproj_kernel, gate_idx=gate_idx),
        grid=(m // tm, n // tn),
        in_specs=[pl.BlockSpec((tm, k1), lambda i, j: (i, 0)),
                  pl.BlockSpec((tm, k2), lambda i, j: (i, 0)),
                  pl.BlockSpec((k1, tn), lambda i, j: (0, j)),
                  pl.BlockSpec((k2, tn), lambda i, j: (1, j)),
                  pl.BlockSpec((tm, tn), lambda i, j: (i, j)),
                  pl.BlockSpec((1, mod.shape[1], tn), lambda i, j: (mrow(i), 0, j))],
        out_specs=pl.BlockSpec((tm, tn), lambda i, j: (i, j)),
        out_shape=jax.ShapeDtypeStruct((m, n), F32),
        compiler_params=_cparams("parallel", "arbitrary"),
        name="out_proj",
    )(a1, a2, w_out, w_out, x, mod)


def _router_kernel(x_ref, w_ref, mod_ref, wr_ref, br_ref, h_ref, eid_ref, wt_ref,
                   *, shift_idx, scale_idx, n_groups, per_group):
    x = x_ref[...]
    y = x * lax.rsqrt(jnp.mean(x * x, axis=-1, keepdims=True) + EPS) * w_ref[...]
    y = y * (1.0 + mod_ref[0, scale_idx:scale_idx + 1, :]) + mod_ref[0, shift_idx:shift_idx + 1, :]
    h_ref[...] = y
    y_hi, y_lo = _split_bf16(y)
    w_hi, w_lo = _split_bf16(wr_ref[...])
    logits = _dot(y_hi, w_hi) + _dot(y_lo, w_hi) + _dot(y_hi, w_lo) + br_ref[...]

    n_exp = n_groups * per_group
    lane = lax.broadcasted_iota(I32, logits.shape, 1)
    gmask = lane < n_groups
    gl = jnp.where(gmask, logits, NEG_BIG)
    gmax = jnp.max(gl, axis=1, keepdims=True)
    gsum = jnp.sum(jnp.where(gmask, jnp.exp(gl - gmax), 0.0), axis=1, keepdims=True)
    g_w = 1.0 / gsum
    g_idx = jnp.min(jnp.where(gl == gmax, lane, LANES), axis=1, keepdims=True)

    in_group = jnp.logical_and(lane >= n_groups + g_idx * per_group,
                               lane < n_groups + (g_idx + 1) * per_group)
    in_group = jnp.logical_and(in_group, lane < n_groups + n_exp)
    el = jnp.where(in_group, logits, NEG_BIG)
    m1 = jnp.max(el, axis=1, keepdims=True)
    i1 = jnp.min(jnp.where(el == m1, lane, LANES), axis=1, keepdims=True)
    el2 = jnp.where(lane == i1, NEG_BIG, el)
    m2 = jnp.max(el2, axis=1, keepdims=True)
    i2 = jnp.min(jnp.where(el2 == m2, lane, LANES), axis=1, keepdims=True)
    e2 = jnp.exp(m2 - m1)
    p1 = 1.0 / (1.0 + e2)
    p2 = e2 * p1
    eid_ref[...] = jnp.where(lane == 0, i1 - n_groups, jnp.where(lane == 1, i2 - n_groups, 0))
    wt_ref[...] = jnp.where(lane == 0, g_w * p1, jnp.where(lane == 1, g_w * p2, 0.0))


def _router(x, w, mod, w_r, b_r, n_ctx_rows, t_lat, shift_idx, scale_idx, n_groups, per_group):
    m, d = x.shape
    tm = _divisor(math.gcd(n_ctx_rows, t_lat), 256)
    mrow = functools.partial(_mod_row, tm=tm, n_ctx_rows=n_ctx_rows, t_lat=t_lat)
    return pl.pallas_call(
        functools.partial(_router_kernel, shift_idx=shift_idx, scale_idx=scale_idx,
                          n_groups=n_groups, per_group=per_group),
        grid=(m // tm,),
        in_specs=[pl.BlockSpec((tm, d), lambda i: (i, 0)),
                  pl.BlockSpec((1, d), lambda i: (0, 0)),
                  pl.BlockSpec((1, mod.shape[1], d), lambda i: (mrow(i), 0, 0)),
                  pl.BlockSpec((d, LANES), lambda i: (0, 0)),
                  pl.BlockSpec((1, LANES), lambda i: (0, 0))],
        out_specs=[pl.BlockSpec((tm, d), lambda i: (i, 0)),
                   pl.BlockSpec((tm, LANES), lambda i: (i, 0)),
                   pl.BlockSpec((tm, LANES), lambda i: (i, 0))],
        out_shape=[jax.ShapeDtypeStruct((m, d), F32),
                   jax.ShapeDtypeStruct((m, LANES), I32),
                   jax.ShapeDtypeStruct((m, LANES), F32)],
        compiler_params=_cparams("parallel"),
        name="router",
    )(x, w.reshape(1, d), mod, w_r, b_r)


def _gather_rows_kernel(idx_ref, src_hbm, dst_hbm, sem, *, rows):
    base = pl.program_id(0) * rows

    def issue(r, carry):
        t = idx_ref[0, 0, r]
        pltpu.make_async_copy(src_hbm.at[pl.ds(t, 1), :], dst_hbm.at[pl.ds(base + r, 1), :], sem).start()
        return carry

    lax.fori_loop(0, rows, issue, 0)

    def drain(r, carry):
        pltpu.make_async_copy(src_hbm.at[pl.ds(0, 1), :], dst_hbm.at[pl.ds(base + r, 1), :], sem).wait()
        return carry

    lax.fori_loop(0, rows, drain, 0)


def _gather_rows(src, slot_src, rows):
    n_slots = slot_src.shape[0]
    nblk = n_slots // rows
    d = src.shape[1]
    return pl.pallas_call(
        functools.partial(_gather_rows_kernel, rows=rows),
        grid=(nblk,),
        in_specs=[pl.BlockSpec((1, 1, rows), lambda i: (i, 0, 0), memory_space=pltpu.SMEM),
                  pl.BlockSpec(memory_space=pl.ANY)],
        out_specs=pl.BlockSpec(memory_space=pl.ANY),
        scratch_shapes=[pltpu.SemaphoreType.DMA(())],
        out_shape=jax.ShapeDtypeStruct((n_slots, d), src.dtype),
        compiler_params=_cparams("arbitrary"),
        name="moe_dispatch",
    )(slot_src.reshape(nblk, 1, rows), src)


def _moe_up_kernel(xblk, eid, wjt, oblk, ojt, fst, valid, x_ref, wg_ref, wu_ref, h_ref, wg_s, wu_s):
    w = pl.program_id(0)

    @pl.when(valid[w] == 1)
    def _go():
        @pl.when(fst[w] == 1)
        def _cast():
            wg_s[...] = wg_ref[0].astype(BF16)
            wu_s[...] = wu_ref[0].astype(BF16)

        x = x_ref[...].astype(BF16)
        a = _dot(x, wg_s[...])
        b = _dot(x, wu_s[...])
        h_ref[...] = (a * _sigmoid(a) * b).astype(h_ref.dtype)

    @pl.when(valid[w] == 0)
    def _pad():
        h_ref[...] = jnp.zeros(h_ref.shape, h_ref.dtype)


def _moe_up(xs, w_gate, w_up, tabs, n_work, tn):
    n_slots, d = xs.shape
    f = w_gate.shape[2]
    return pl.pallas_call(
        _moe_up_kernel,
        grid_spec=pltpu.PrefetchScalarGridSpec(
            num_scalar_prefetch=7, grid=(n_work,),
            in_specs=[pl.BlockSpec((MOE_TILE, d), lambda w, xb, e, wj, *_: (xb[w], 0)),
                      pl.BlockSpec((1, d, tn), lambda w, xb, e, wj, *_: (e[w], 0, wj[w])),
                      pl.BlockSpec((1, d, tn), lambda w, xb, e, wj, *_: (e[w], 0, wj[w]))],
            out_specs=pl.BlockSpec((MOE_TILE, tn), lambda w, xb, e, wj, ob, oj, *_: (ob[w], oj[w])),
            scratch_shapes=[pltpu.VMEM((d, tn), BF16), pltpu.VMEM((d, tn), BF16)]),
        out_shape=jax.ShapeDtypeStruct((n_slots, f), BF16),
        compiler_params=_cparams("arbitrary"),
        name="moe_up",
    )(*tabs, xs, w_gate, w_up)


def _moe_down_kernel(eid, fst, valid, h_ref, wd_ref, y_ref, wd_s):
    w = pl.program_id(0)

    @pl.when(valid[w] == 1)
    def _go():
        @pl.when(fst[w] == 1)
        def _cast():
            wd_s[...] = wd_ref[0].astype(BF16)

        y_ref[...] = _dot(h_ref[...], wd_s[...])

    @pl.when(valid[w] == 0)
    def _pad():
        y_ref[...] = jnp.zeros(y_ref.shape, y_ref.dtype)


def _moe_down(hs, w_down, tabs):
    n_slots, f = hs.shape
    d = w_down.shape[2]
    nblk = n_slots // MOE_TILE
    return pl.pallas_call(
        _moe_down_kernel,
        grid_spec=pltpu.PrefetchScalarGridSpec(
            num_scalar_prefetch=3, grid=(nblk,),
            in_specs=[pl.BlockSpec((MOE_TILE, f), lambda w, eid, *_: (w, 0)),
                      pl.BlockSpec((1, f, d), lambda w, eid, *_: (eid[w], 0, 0))],
            out_specs=pl.BlockSpec((MOE_TILE, d), lambda w, eid, *_: (w, 0)),
            scratch_shapes=[pltpu.VMEM((f, d), BF16)]),
        out_shape=jax.ShapeDtypeStruct((n_slots, d), F32),
        compiler_params=_cparams("arbitrary"),
        name="moe_down",
    )(*tabs, hs, w_down)


def _final_kernel(dst_ref, x_ref, wt_ref, mod_ref, fw_ref, ys_hbm, o_ref, buf, sem, *, rows, gate_idx):
    def issue(r, carry):
        for kk in range(TOP_K):
            s = dst_ref[0, 0, TOP_K * r + kk]
            pltpu.make_async_copy(ys_hbm.at[pl.ds(s, 1), :], buf.at[kk, pl.ds(r, 1), :], sem).start()
        return carry

    lax.fori_loop(0, rows, issue, 0)

    def drain(r, carry):
        for kk in range(TOP_K):
            pltpu.make_async_copy(ys_hbm.at[pl.ds(0, 1), :], buf.at[kk, pl.ds(r, 1), :], sem).wait()
        return carry

    lax.fori_loop(0, rows, drain, 0)

    wt = wt_ref[...]
    moe = wt[:, 0:1] * buf[0]
    for kk in range(1, TOP_K):
        moe = moe + wt[:, kk:kk + 1] * buf[kk]
    x = x_ref[...] + mod_ref[0, gate_idx:gate_idx + 1, :] * moe
    o_ref[...] = x * lax.rsqrt(jnp.mean(x * x, axis=-1, keepdims=True) + EPS) * fw_ref[...]


def _final(x1, wt, dest, ys, mod, final_w, row0, n_rows, n_ctx_rows, t_lat, gate_idx):
    d = x1.shape[1]
    tm = _divisor(math.gcd(n_ctx_rows, t_lat), 128)
    t0 = row0 // tm
    mrow = functools.partial(_mod_row, tm=tm, n_ctx_rows=n_ctx_rows, t_lat=t_lat)
    dest3 = dest.reshape(-1, 1, TOP_K * tm)
    return pl.pallas_call(
        functools.partial(_final_kernel, rows=tm, gate_idx=gate_idx),
        grid=(n_rows // tm,),
        in_specs=[pl.BlockSpec((1, 1, TOP_K * tm), lambda i: (t0 + i, 0, 0), memory_space=pltpu.SMEM),
                  pl.BlockSpec((tm, d), lambda i: (t0 + i, 0)),
                  pl.BlockSpec((tm, LANES), lambda i: (t0 + i, 0)),
                  pl.BlockSpec((1, mod.shape[1], d), lambda i: (mrow(t0 + i), 0, 0)),
                  pl.BlockSpec((1, d), lambda i: (0, 0)),
                  pl.BlockSpec(memory_space=pl.ANY)],
        out_specs=pl.BlockSpec((tm, d), lambda i: (i, 0)),
        out_shape=jax.ShapeDtypeStruct((n_rows, d), F32),
        scratch_shapes=[pltpu.VMEM((TOP_K, tm, d), F32), pltpu.SemaphoreType.DMA(())],
        compiler_params=_cparams("arbitrary"),
        name="moe_combine_final",
    )(dest3, x1, wt, mod, final_w.reshape(1, d), ys)


def _moe_plan(eid, n_exp, n_jt):
    t = eid.shape[0]
    n_assign = t * TOP_K
    eflat = eid.reshape(n_assign)
    onehot = (eflat[:, None] == jnp.arange(n_exp, dtype=I32)[None, :]).astype(I32)
    csum = jnp.cumsum(onehot, axis=0)
    counts = csum[-1]
    rank = jnp.sum(onehot * (csum - 1), axis=1)
    nb = (counts + MOE_TILE - 1) // MOE_TILE
    blk_end = jnp.cumsum(nb)
    blk_start = blk_end - nb
    n_blocks = blk_end[-1]
    pad_start = blk_start * MOE_TILE
    dest = pad_start[eflat] + rank

    nblk_max = n_assign // MOE_TILE + n_exp
    n_slots = nblk_max * MOE_TILE
    order = jnp.argsort(eflat, stable=True).astype(I32)
    start = jnp.cumsum(counts) - counts
    slot = jnp.arange(n_slots, dtype=I32)
    sblk = slot // MOE_TILE
    blk_e = jnp.minimum(jnp.searchsorted(blk_end, jnp.arange(nblk_max, dtype=I32), side="right"),
                        n_exp - 1).astype(I32)
    se = blk_e[sblk]
    off = slot - pad_start[se]
    ok = jnp.logical_and(sblk < n_blocks, off < counts[se])
    src_a = order[jnp.clip(start[se] + off, 0, n_assign - 1)]
    slot_tok = jnp.where(ok, src_a // TOP_K, 0).astype(I32)

    b_idx = jnp.arange(nblk_max, dtype=I32)
    d_valid = (b_idx < n_blocks).astype(I32)
    last_e = blk_e[jnp.maximum(n_blocks - 1, 0)]
    d_eid = jnp.where(d_valid == 1, blk_e, last_e).astype(I32)
    d_first = jnp.logical_and(d_valid == 1, b_idx == blk_start[d_eid]).astype(I32)

    n_work = n_jt * nblk_max
    w_idx = jnp.arange(n_work, dtype=I32)
    per_e = n_jt * nb
    w_end = jnp.cumsum(per_e)
    w_valid = w_idx < w_end[-1]
    we = jnp.minimum(jnp.searchsorted(w_end, w_idx, side="right"), n_exp - 1).astype(I32)
    r = w_idx - (w_end[we] - per_e[we])
    nbe = jnp.maximum(nb[we], 1)
    u_jt = r // nbe
    u_t = r - u_jt * nbe
    u_blk = blk_start[we] + u_t
    last_w = jnp.maximum(w_end[-1] - 1, 0)
    spare = w_idx - w_end[-1]
    u_xblk = jnp.where(w_valid, u_blk, u_blk[last_w]).astype(I32)
    u_eid = jnp.where(w_valid, we, we[last_w]).astype(I32)
    u_wjt = jnp.where(w_valid, u_jt, u_jt[last_w]).astype(I32)
    u_oblk = jnp.where(w_valid, u_blk, n_blocks + spare // n_jt).astype(I32)
    u_ojt = jnp.where(w_valid, u_jt, spare % n_jt).astype(I32)
    u_first = jnp.logical_and(w_valid, u_t == 0).astype(I32)
    up_tabs = (u_xblk, u_eid, u_wjt, u_oblk, u_ojt, u_first, w_valid.astype(I32))
    down_tabs = (d_eid, d_first, d_valid)
    return dest.astype(I32), slot_tok, up_tabs, down_tabs, n_work


def kernel(x_prompt, x_sample, state_mlstm_C, state_mlstm_n, state_mlstm_m, state_ret_S, c, c_ctx,
           norm1_w, norm2_w, w_ada, b_ada, w_in, b_mgates, mlstm_norm_w, ret_norm_w, ret_log_decay, w_out,
           w_router_group, b_router_group, w_router_expert, b_router_expert, w_exp_gate, w_exp_up,
           w_exp_down, final_norm_w):
    n_ctx, t_ctx, d_model = x_prompt.shape
    n_lat, t_lat, _ = x_sample.shape
    depth = norm1_w.shape[0]
    assert depth == 1, "single-layer trunk"
    m_heads, m_dk, m_dv = state_mlstm_C.shape[3:]
    r_heads, r_dk, r_dv = state_ret_S.shape[3:]
    n_groups = w_router_group.shape[2]
    n_exp = w_router_expert.shape[2]
    per_group = n_exp // n_groups
    d_exp = w_exp_gate.shape[3]
    n_gates = N_GATE_ROWS * m_heads
    n_ctx_rows = n_ctx * t_ctx
    m_rows = n_ctx_rows + n_lat * t_lat
    assert t_ctx % CHUNK == 0 and t_lat % CHUNK == 0 and n_gates + 0 <= LANES
    assert n_groups + n_exp <= LANES

    cvec = jnp.concatenate([c_ctx[None, :], c, jnp.zeros((SUBLANES - 1 - n_lat, d_model), F32)], axis=0)
    mod = _ada_mod(cvec, w_ada[0], b_ada[0]).reshape(SUBLANES, 6, d_model)

    x = jnp.concatenate([x_prompt.reshape(n_ctx_rows, d_model), x_sample.reshape(n_lat * t_lat, d_model)], axis=0)
    h = _norm_mod(x, norm1_w[0], mod, n_ctx_rows, t_lat, 0, 1, BF16)

    g0 = 2 * m_heads * m_dk + 2 * m_heads * m_dv
    w_main = jnp.concatenate([w_in[0][:, :g0], w_in[0][:, g0 + n_gates:]], axis=1).astype(BF16)
    w_gate = jnp.pad(w_in[0][:, g0:g0 + n_gates], ((0, 0), (0, LANES - n_gates))).astype(BF16)
    z, zg = _inproj(h, w_main, w_gate)

    gbias = jnp.pad(b_mgates[0].reshape(1, n_gates), ((0, 0), (0, LANES - n_gates)))
    st_m = jnp.broadcast_to(state_mlstm_m[:, 0][..., None], (n_lat, 2, m_heads, LANES))
    mdims = (n_ctx, t_ctx, n_lat, t_lat, m_heads, m_dk, m_dv)
    hf, cf, nf, mf = _mlstm_scan(z, zg, gbias, state_mlstm_C[:, 0], state_mlstm_n[:, 0], st_m, mdims, False)
    mix_m, cb, nb, mb = _mlstm_scan(z, zg, gbias, state_mlstm_C[:, 0], state_mlstm_n[:, 0], st_m, mdims, True,
                                    hf=hf, norm_w=mlstm_norm_w[0])

    cos_tab, sin_tab = _rope_tables(t_lat, r_dk)
    ld = jnp.pad(ret_log_decay[0], ((0, SUBLANES - 2), (0, LANES - r_heads)))
    rdims = (n_ctx, t_ctx, n_lat, t_lat, r_heads, r_dk, r_dv)
    of, sf = _ret_scan(z, cos_tab, sin_tab, ld, state_ret_S[:, 0], rdims, g0, False)
    mix_r, sb = _ret_scan(z, cos_tab, sin_tab, ld, state_ret_S[:, 0], rdims, g0, True,
                          of=of, norm_w=ret_norm_w[0])

    x1 = _outproj(mix_m, mix_r, w_out[0].astype(BF16), x, mod, n_ctx_rows, t_lat, 2)

    w_r = jnp.pad(jnp.concatenate([w_router_group[0], w_router_expert[0]], axis=1),
                  ((0, 0), (0, LANES - n_groups - n_exp)))
    b_r = jnp.pad(jnp.concatenate([b_router_group[0], b_router_expert[0]])[None, :],
                  ((0, 0), (0, LANES - n_groups - n_exp)))
    h2, eid, wt = _router(x1, norm2_w[0], mod, w_r, b_r, n_ctx_rows, t_lat, 3, 4, n_groups, per_group)

    tn_up = _divisor(d_exp, 512, LANES)
    dest, slot_tok, up_tabs, down_tabs, n_work = _moe_plan(eid[:, :TOP_K], n_exp, d_exp // tn_up)
    xs = _gather_rows(h2, slot_tok, MOE_TILE)
    hs = _moe_up(xs, w_exp_gate[0], w_exp_up[0], up_tabs, n_work, tn_up)
    ys = _moe_down(hs, w_exp_down[0], down_tabs)

    y_ctx = _final(x1, wt, dest, ys, mod, final_norm_w, 0, n_ctx_rows, n_ctx_rows, t_lat, 5)
    y_lat = _final(x1, wt, dest, ys, mod, final_norm_w, n_ctx_rows, n_lat * t_lat, n_ctx_rows, t_lat, 5)

    new_c = jnp.stack([cf, cb], axis=1)[:, None]
    new_n = jnp.stack([nf, nb], axis=1)[:, None]
    new_m = jnp.stack([mf[..., 0], mb[..., 0]], axis=1)[:, None]
    new_s = jnp.stack([sf, sb], axis=1)[:, None]
    return (y_ctx.reshape(n_ctx, t_ctx, d_model), y_lat.reshape(n_lat, t_lat, d_model),
            new_c, new_n, new_m, new_s)
```

```python
import functools
import math

import numpy as np
import jax
import jax.numpy as jnp
from jax import lax
from jax.experimental import pallas as pl
from jax.experimental.pallas import tpu as pltpu

F32 = jnp.float32
BF16 = jnp.bfloat16
I32 = jnp.int32

CHUNK = 128
GRID_W = 64
ROPE_BASE = 10000.0
EPS = 1e-6
TOP_K = 2
N_GATE_ROWS = 4

LANES = 128
SUBLANES = 8
VMEM_LIMIT_BYTES = 56 * 1024 * 1024
NEG_BIG = -1e30
MOE_TILE = 256


def _divisor(n, pref, mult=SUBLANES):
    if n <= pref:
        return n
    d = (pref // mult) * mult
    while d > mult and n % d:
        d -= mult
    assert n % d == 0, (n, pref, mult)
    return d


def _cparams(*sem):
    return pltpu.CompilerParams(dimension_semantics=sem, vmem_limit_bytes=VMEM_LIMIT_BYTES)


def _dot(a, b):
    return jnp.dot(a, b, preferred_element_type=F32)


def _dot_nt(a, b):
    return lax.dot_general(a, b, (((1,), (1,)), ((), ())), preferred_element_type=F32)


def _dot_tn(a, b):
    return lax.dot_general(a, b, (((0,), (0,)), ((), ())), preferred_element_type=F32)


def _split_bf16(x):
    hi = x.astype(BF16)
    lo = (x - hi.astype(F32)).astype(BF16)
    return hi, lo


def _sigmoid(x):
    return 1.0 / (1.0 + jnp.exp(-x))


def _log_sigmoid(x):
    return jnp.minimum(x, 0.0) - jnp.log(1.0 + jnp.exp(-jnp.abs(x)))


def _mod_row(i, tm, n_ctx_rows, t_lat):
    r0 = i * tm
    return jnp.where(r0 < n_ctx_rows, 0, 1 + (r0 - n_ctx_rows) // t_lat)


def _ada_kernel(c_ref, w_ref, b_ref, o_ref):
    c = c_ref[...]
    s = c * _sigmoid(c)
    s_hi, s_lo = _split_bf16(s)
    w_hi, w_lo = _split_bf16(w_ref[...])
    o_ref[...] = _dot(s_hi, w_hi) + _dot(s_lo, w_hi) + _dot(s_hi, w_lo) + b_ref[...]


def _ada_mod(cvec, w_ada, b_ada):
    r, d = cvec.shape
    n = w_ada.shape[1]
    tn = _divisor(n, 256, LANES)
    return pl.pallas_call(
        _ada_kernel,
        grid=(n // tn,),
        in_specs=[pl.BlockSpec((r, d), lambda j: (0, 0)),
                  pl.BlockSpec((d, tn), lambda j: (0, j)),
                  pl.BlockSpec((1, tn), lambda j: (0, j))],
        out_specs=pl.BlockSpec((r, tn), lambda j: (0, j)),
        out_shape=jax.ShapeDtypeStruct((r, n), F32),
        compiler_params=_cparams("parallel"),
        name="ada_mod",
    )(cvec, w_ada, b_ada.reshape(1, n))


def _norm_mod_kernel(xc_ref, xl_ref, w_ref, mod_ref, o_ref, *, shift_idx, scale_idx, n_ctx_tiles):
    def body(x_ref):
        x = x_ref[...]
        y = x * lax.rsqrt(jnp.mean(x * x, axis=-1, keepdims=True) + EPS) * w_ref[...]
        y = y * (1.0 + mod_ref[0, scale_idx:scale_idx + 1, :]) + mod_ref[0, shift_idx:shift_idx + 1, :]
        o_ref[...] = y.astype(o_ref.dtype)

    i = pl.program_id(0)
    pl.when(i < n_ctx_tiles)(lambda: body(xc_ref))
    pl.when(i >= n_ctx_tiles)(lambda: body(xl_ref))


def _norm_mod(xc, xl, w, mod, t_lat, shift_idx, scale_idx, out_dtype):
    n_ctx_rows, d = xc.shape
    m = n_ctx_rows + xl.shape[0]
    tm = _divisor(math.gcd(n_ctx_rows, t_lat), 256)
    nct = n_ctx_rows // tm
    mrow = functools.partial(_mod_row, tm=tm, n_ctx_rows=n_ctx_rows, t_lat=t_lat)
    return pl.pallas_call(
        functools.partial(_norm_mod_kernel, shift_idx=shift_idx, scale_idx=scale_idx, n_ctx_tiles=nct),
        grid=(m // tm,),
        in_specs=[pl.BlockSpec((tm, d), lambda i: (jnp.minimum(i, nct - 1), 0)),
                  pl.BlockSpec((tm, d), lambda i: (jnp.maximum(i - nct, 0), 0)),
                  pl.BlockSpec((1, d), lambda i: (0, 0)),
                  pl.BlockSpec((1, mod.shape[1], d), lambda i: (mrow(i), 0, 0))],
        out_specs=pl.BlockSpec((tm, d), lambda i: (i, 0)),
        out_shape=jax.ShapeDtypeStruct((m, d), out_dtype),
        compiler_params=_cparams("arbitrary"),
        name="norm_mod",
    )(xc, xl, w.reshape(1, d), mod)


def _inproj_kernel(a_ref, b_ref, wg_ref, z_ref, zg_ref):
    a = a_ref[...]
    z_ref[...] = _dot(a, b_ref[...])

    @pl.when(pl.program_id(1) == 0)
    def _gates():
        zg_ref[...] = _dot(a, wg_ref[...])


def _inproj(h, w_main, w_gate):
    m, k = h.shape
    n = w_main.shape[1]
    tm = _divisor(m, 1024)
    tn = _divisor(n, 1024, LANES)
    return pl.pallas_call(
        _inproj_kernel,
        grid=(m // tm, n // tn),
        in_specs=[pl.BlockSpec((tm, k), lambda i, j: (i, 0)),
                  pl.BlockSpec((k, tn), lambda i, j: (0, j)),
                  pl.BlockSpec((k, LANES), lambda i, j: (0, 0))],
        out_specs=[pl.BlockSpec((tm, tn), lambda i, j: (i, j)),
                   pl.BlockSpec((tm, LANES), lambda i, j: (i, 0))],
        out_shape=[jax.ShapeDtypeStruct((m, n), F32),
                   jax.ShapeDtypeStruct((m, LANES), F32)],
        compiler_params=_cparams("parallel", "arbitrary"),
        name="in_proj",
    )(h, w_main, w_gate)


def _scan_schedule(n_ctx, t_ctx, n_lat, t_lat, reverse):
    rowblk, first, last, ctxb, latb, islat, posblk = [], [], [], [], [], [], []
    nc_ctx, nc_lat = t_ctx // CHUNK, t_lat // CHUNK
    for b in range(n_ctx):
        order = range(nc_ctx - 1, -1, -1) if reverse else range(nc_ctx)
        for pos, c in enumerate(order):
            rowblk.append(b * nc_ctx + c)
            first.append(int(pos == 0))
            last.append(int(pos == nc_ctx - 1))
            ctxb.append(b)
            latb.append(0)
            islat.append(0)
            posblk.append(nc_lat)
    base = n_ctx * nc_ctx
    for b in range(n_lat):
        order = range(nc_lat - 1, -1, -1) if reverse else range(nc_lat)
        for pos, c in enumerate(order):
            rowblk.append(base + b * nc_lat + c)
            first.append(int(pos == 0))
            last.append(int(pos == nc_lat - 1))
            ctxb.append(n_ctx - 1)
            latb.append(b)
            islat.append(1)
            posblk.append(c)
    tabs = (rowblk, first, last, ctxb, latb, islat, posblk)
    return tuple(jnp.asarray(np.asarray(t, np.int32)) for t in tabs)


def _mlstm_kernel(rowblk, first, last, ctxb, latb, islat, posblk,
                  q_ref, k_ref, v_ref, g_ref, gb_ref, c0_ref, n0_ref, m0_ref, *rest,
                  heads, dk, dv, reverse, combine):
    if combine:
        hf_ref, o_ref, nw_ref, out_ref, cout_ref, nout_ref, mout_ref, c_s, n_s, m_s = rest
    else:
        out_ref, cout_ref, nout_ref, mout_ref, c_s, n_s, m_s = rest
    w = pl.program_id(0)
    lat = islat[w] == 1

    @pl.when(first[w] == 1)
    def _init():
        c_s[...] = jnp.where(lat, c0_ref[0, 0], 0.0)
        n_s[...] = jnp.where(lat, n0_ref[0, 0], 0.0)
        m_s[...] = jnp.where(lat, m0_ref[0, 0], 0.0)

    L = CHUNK
    ti = lax.broadcasted_iota(I32, (L, L), 0)
    si = lax.broadcasted_iota(I32, (L, L), 1)
    mask = (si >= ti) if reverse else (si <= ti)
    mask_t = (ti >= si) if reverse else (ti <= si)
    scale = dk ** -0.5
    gate_row = 2 if reverse else 0

    g = g_ref[...] + gb_ref[...]
    gt = g.T
    lg = _log_sigmoid(g)
    lgt = _log_sigmoid(gt)

    for h in range(heads):
        ci = gate_row * heads + h
        cf = (gate_row + 1) * heads + h
        i_col, i_row = g[:, ci:ci + 1], gt[ci:ci + 1, :]
        lf_col, lf_row = lg[:, cf:cf + 1], lgt[cf:cf + 1, :]
        b_col = jnp.sum(jnp.where(mask, lf_row, 0.0), axis=1, keepdims=True)
        b_row = jnp.sum(jnp.where(mask_t, lf_col, 0.0), axis=0, keepdims=True)
        bl = jnp.sum(lf_row, axis=1, keepdims=True)
        m_old = m_s[h:h + 1, 0:1]

        dm = jnp.where(mask, b_col - b_row + i_row, NEG_BIG)
        inter = b_col + m_old
        mt = jnp.maximum(jnp.max(dm, axis=1, keepdims=True), inter)
        p = jnp.exp(dm - mt)
        wi = jnp.exp(inter - mt)

        q = q_ref[:, h * dk:(h + 1) * dk]
        k = k_ref[:, h * dk:(h + 1) * dk]
        qb, kb = q.astype(BF16), k.astype(BF16)
        vb = v_ref[:, h * dv:(h + 1) * dv].astype(BF16)
        c_old = c_s[h]
        n_old = n_s[h:h + 1, :]

        s = _dot_nt(qb, kb) * (scale * p)
        num = _dot(s.astype(BF16), vb) + wi * _dot(qb, c_old.astype(BF16))
        den = jnp.sum(s, axis=1, keepdims=True) + wi * jnp.sum(q * n_old, axis=1, keepdims=True)
        hval = num / jnp.maximum(jnp.abs(den), jnp.exp(-mt))

        g_col = bl - b_col + i_col
        g_row = bl - b_row + i_row
        m_new = jnp.maximum(bl + m_old, jnp.max(g_row, axis=1, keepdims=True))
        a_prev = jnp.exp(bl + m_old - m_new)
        ak = (jnp.exp(g_col - m_new) * scale) * k
        c_s[h] = a_prev * c_old + _dot_tn(ak.astype(BF16), vb)
        n_s[h:h + 1, :] = a_prev * n_old + jnp.sum(ak, axis=0, keepdims=True)
        m_s[h:h + 1, :] = jnp.broadcast_to(m_new, (1, LANES))

        if combine:
            hs = hval + hf_ref[:, h * dv:(h + 1) * dv]
            y = hs * lax.rsqrt(jnp.mean(hs * hs, axis=-1, keepdims=True) + EPS)
            y = y * nw_ref[:, h * dv:(h + 1) * dv] * _sigmoid(o_ref[:, h * dv:(h + 1) * dv])
            out_ref[:, h * dv:(h + 1) * dv] = y.astype(out_ref.dtype)
        else:
            out_ref[:, h * dv:(h + 1) * dv] = hval

    @pl.when(jnp.logical_and(last[w] == 1, jnp.logical_not(lat)))
    def _emit():
        cout_ref[0] = c_s[...]
        nout_ref[0] = n_s[...]
        mout_ref[0] = m_s[...]


def _mlstm_scan(z, zg, gbias, st_c, st_n, st_m, dims, reverse, hf=None, norm_w=None):
    (n_ctx, t_ctx, n_lat, t_lat, heads, dk, dv) = dims
    combine = hf is not None
    m = z.shape[0]
    d = 1 if reverse else 0
    tabs = _scan_schedule(n_ctx, t_ctx, n_lat, t_lat, reverse)
    qw, vw = heads * dk, heads * dv
    assert (2 * qw) % vw == 0
    v_blk = (2 * qw) // vw
    in_specs = [
        pl.BlockSpec((CHUNK, qw), lambda w, rb, *_: (rb[w], 0)),
        pl.BlockSpec((CHUNK, qw), lambda w, rb, *_: (rb[w], 1)),
        pl.BlockSpec((CHUNK, vw), lambda w, rb, *_: (rb[w], v_blk)),
        pl.BlockSpec((CHUNK, LANES), lambda w, rb, *_: (rb[w], 0)),
        pl.BlockSpec((1, LANES), lambda w, *_: (0, 0)),
        pl.BlockSpec((1, 1, heads, dk, dv), lambda w, rb, f, l, cb, lb, *_: (lb[w], d, 0, 0, 0)),
        pl.BlockSpec((1, 1, heads, dk), lambda w, rb, f, l, cb, lb, *_: (lb[w], d, 0, 0)),
        pl.BlockSpec((1, 1, heads, LANES), lambda w, rb, f, l, cb, lb, *_: (lb[w], d, 0, 0)),
    ]
    args = [z, z, z, zg, gbias, st_c, st_n, st_m]
    if combine:
        in_specs += [
            pl.BlockSpec((CHUNK, vw), lambda w, rb, *_: (rb[w], 0)),
            pl.BlockSpec((CHUNK, vw), lambda w, rb, *_: (rb[w], v_blk + 1)),
            pl.BlockSpec((1, vw), lambda w, *_: (0, 0)),
        ]
        args += [hf, z, norm_w.reshape(1, vw)]
    out_dtype = BF16 if combine else F32
    out_specs = [
        pl.BlockSpec((CHUNK, vw), lambda w, rb, *_: (rb[w], 0)),
        pl.BlockSpec((1, heads, dk, dv), lambda w, rb, f, l, cb, *_: (cb[w], 0, 0, 0)),
        pl.BlockSpec((1, heads, dk), lambda w, rb, f, l, cb, *_: (cb[w], 0, 0)),
        pl.BlockSpec((1, heads, LANES), lambda w, rb, f, l, cb, *_: (cb[w], 0, 0)),
    ]
    out_shape = [
        jax.ShapeDtypeStruct((m, vw), out_dtype),
        jax.ShapeDtypeStruct((n_ctx, heads, dk, dv), F32),
        jax.ShapeDtypeStruct((n_ctx, heads, dk), F32),
        jax.ShapeDtypeStruct((n_ctx, heads, LANES), F32),
    ]
    return pl.pallas_call(
        functools.partial(_mlstm_kernel, heads=heads, dk=dk, dv=dv, reverse=reverse, combine=combine),
        grid_spec=pltpu.PrefetchScalarGridSpec(
            num_scalar_prefetch=len(tabs), grid=(m // CHUNK,),
            in_specs=in_specs, out_specs=out_specs,
            scratch_shapes=[pltpu.VMEM((heads, dk, dv), F32),
                            pltpu.VMEM((heads, dk), F32),
                            pltpu.VMEM((heads, LANES), F32)]),
        out_shape=out_shape,
        compiler_params=_cparams("arbitrary"),
        name="mlstm_bwd" if reverse else "mlstm_fwd",
    )(*tabs, *args)


def _rope_partner(x, d):
    q, hlf = d // 4, d // 2
    return jnp.concatenate([x[:, q:hlf], x[:, :q], x[:, hlf + q:], x[:, hlf:hlf + q]], axis=1)


def _ret_kernel(rowblk, first, last, ctxb, latb, islat, posblk,
                q_ref, k_ref, v_ref, cos_ref, sin_ref, ld_ref, s0_ref, *rest,
                heads, dk, dv, reverse, combine):
    if combine:
        of_ref, g_ref, nw_ref, out_ref, sout_ref, s_s = rest
    else:
        out_ref, sout_ref, s_s = rest
    w = pl.program_id(0)
    lat = islat[w] == 1

    @pl.when(first[w] == 1)
    def _init():
        s_s[...] = jnp.where(lat, s0_ref[0, 0], 0.0)

    L = CHUNK
    ti = lax.broadcasted_iota(I32, (L, L), 0).astype(F32)
    si = lax.broadcasted_iota(I32, (L, L), 1).astype(F32)
    diff = (si - ti) if reverse else (ti - si)
    idx = lax.broadcasted_iota(I32, (L, 1), 0).astype(F32)
    q_pow = (L - idx) if reverse else (idx + 1.0)
    k_pow = idx if reverse else (L - 1.0 - idx)
    scale = dk ** -0.5
    drow = 1 if reverse else 0
    log_decay = -jnp.exp(ld_ref[drow:drow + 1, :])
    cos_t, sin_t = cos_ref[...], sin_ref[...]

    for h in range(heads):
        ld = log_decay[:, h:h + 1]
        intra = jnp.where(diff >= 0.0, jnp.exp(jnp.maximum(diff, 0.0) * ld), 0.0)
        q_dec = jnp.exp(q_pow * ld)
        k_dec = jnp.exp(k_pow * ld)
        s_dec = jnp.exp(float(L) * ld)

        q = q_ref[:, h * dk:(h + 1) * dk]
        k = k_ref[:, h * dk:(h + 1) * dk]
        q = q * cos_t + _rope_partner(q, dk) * sin_t
        k = k * cos_t + _rope_partner(k, dk) * sin_t
        qb, kb = q.astype(BF16), k.astype(BF16)
        vb = v_ref[:, h * dv:(h + 1) * dv].astype(BF16)
        s_old = s_s[h]

        a = _dot_nt(qb, kb) * (scale * intra)
        o = _dot(a.astype(BF16), vb) + _dot(qb, s_old.astype(BF16)) * q_dec
        s_s[h] = s_dec * s_old + _dot_tn((k * (scale * k_dec)).astype(BF16), vb)

        if combine:
            hs = o + of_ref[:, h * dv:(h + 1) * dv]
            y = hs * lax.rsqrt(jnp.mean(hs * hs, axis=-1, keepdims=True) + EPS)
            gate = g_ref[:, h * dv:(h + 1) * dv]
            y = y * nw_ref[:, h * dv:(h + 1) * dv] * (gate * _sigmoid(gate))
            out_ref[:, h * dv:(h + 1) * dv] = y.astype(out_ref.dtype)
        else:
            out_ref[:, h * dv:(h + 1) * dv] = o

    @pl.when(jnp.logical_and(last[w] == 1, jnp.logical_not(lat)))
    def _emit():
        sout_ref[0] = s_s[...]


def _ret_scan(z, cos_tab, sin_tab, ld, st_s, dims, col0, reverse, of=None, norm_w=None):
    (n_ctx, t_ctx, n_lat, t_lat, heads, dk, dv) = dims
    combine = of is not None
    m = z.shape[0]
    d = 1 if reverse else 0
    tabs = _scan_schedule(n_ctx, t_ctx, n_lat, t_lat, reverse)
    qw, vw = heads * dk, heads * dv
    assert qw == vw and col0 % qw == 0
    b0 = col0 // qw
    in_specs = [
        pl.BlockSpec((CHUNK, qw), lambda w, rb, *_: (rb[w], b0)),
        pl.BlockSpec((CHUNK, qw), lambda w, rb, *_: (rb[w], b0 + 1)),
        pl.BlockSpec((CHUNK, vw), lambda w, rb, *_: (rb[w], b0 + 2)),
        pl.BlockSpec((CHUNK, dk), lambda w, rb, f, l, cb, lb, il, pb: (pb[w], 0)),
        pl.BlockSpec((CHUNK, dk), lambda w, rb, f, l, cb, lb, il, pb: (pb[w], 0)),
        pl.BlockSpec((SUBLANES, LANES), lambda w, *_: (0, 0)),
        pl.BlockSpec((1, 1, heads, dk, dv), lambda w, rb, f, l, cb, lb, *_: (lb[w], d, 0, 0, 0)),
    ]
    args = [z, z, z, cos_tab, sin_tab, ld, st_s]
    if combine:
        in_specs += [
            pl.BlockSpec((CHUNK, vw), lambda w, rb, *_: (rb[w], 0)),
            pl.BlockSpec((CHUNK, vw), lambda w, rb, *_: (rb[w], b0 + 3)),
            pl.BlockSpec((1, vw), lambda w, *_: (0, 0)),
        ]
        args += [of, z, norm_w.reshape(1, vw)]
    out_dtype = BF16 if combine else F32
    return pl.pallas_call(
        functools.partial(_ret_kernel, heads=heads, dk=dk, dv=dv, reverse=reverse, combine=combine),
        grid_spec=pltpu.PrefetchScalarGridSpec(
            num_scalar_prefetch=len(tabs), grid=(m // CHUNK,),
            in_specs=in_specs,
            out_specs=[pl.BlockSpec((CHUNK, vw), lambda w, rb, *_: (rb[w], 0)),
                       pl.BlockSpec((1, heads, dk, dv), lambda w, rb, f, l, cb, *_: (cb[w], 0, 0, 0))],
            scratch_shapes=[pltpu.VMEM((heads, dk, dv), F32)]),
        out_shape=[jax.ShapeDtypeStruct((m, vw), out_dtype),
                   jax.ShapeDtypeStruct((n_ctx, heads, dk, dv), F32)],
        compiler_params=_cparams("arbitrary"),
        name="ret_bwd" if reverse else "ret_fwd",
    )(*tabs, *args)


def _rope_tables(t_lat, d):
    quarter = d // 4
    rows = t_lat // GRID_W
    row = jnp.repeat(jnp.arange(rows, dtype=F32), GRID_W)
    col = jnp.tile(jnp.arange(GRID_W, dtype=F32), rows)
    inv = ROPE_BASE ** (-jnp.arange(quarter, dtype=F32) / quarter)
    ar = row[:, None] * inv[None, :]
    ac = col[:, None] * inv[None, :]
    cos_t = jnp.concatenate([jnp.cos(ar), jnp.cos(ar), jnp.cos(ac), jnp.cos(ac)], axis=1)
    sin_t = jnp.concatenate([-jnp.sin(ar), jnp.sin(ar), -jnp.sin(ac), jnp.sin(ac)], axis=1)
    cos_t = jnp.concatenate([cos_t, jnp.ones((CHUNK, d), F32)], axis=0)
    sin_t = jnp.concatenate([sin_t, jnp.zeros((CHUNK, d), F32)], axis=0)
    return cos_t, sin_t


def _outproj_kernel(a1_ref, a2_ref, b1_ref, b2_ref, xc_ref, xl_ref, mod_ref, o_ref, *, gate_idx, n_ctx_tiles):
    acc = _dot(a1_ref[...], b1_ref[...]) + _dot(a2_ref[...], b2_ref[...])
    upd = mod_ref[0, gate_idx:gate_idx + 1, :] * acc
    i = pl.program_id(0)

    @pl.when(i < n_ctx_tiles)
    def _ctx():
        o_ref[...] = xc_ref[...] + upd

    @pl.when(i >= n_ctx_tiles)
    def _lat():
        o_ref[...] = xl_ref[...] + upd


def _outproj(a1, a2, w_out, xc, xl, mod, t_lat, gate_idx):
    m, k1 = a1.shape
    k2 = a2.shape[1]
    n = w_out.shape[1]
    n_ctx_rows = xc.shape[0]
    assert k1 == k2
    tm = _divisor(math.gcd(n_ctx_rows, t_lat), 1024)
    tn = _divisor(n, 512, LANES)
    nct = n_ctx_rows // tm
    mrow = functools.partial(_mod_row, tm=tm, n_ctx_rows=n_ctx_rows, t_lat=t_lat)
    return pl.pallas_call(
        functools.partial(_outproj_kernel, gate_idx=gate_idx, n_ctx_tiles=nct),
        grid=(m // tm, n // tn),
        in_specs=[pl.BlockSpec((tm, k1), lambda i, j: (i, 0)),
                  pl.BlockSpec((tm, k2), lambda i, j: (i, 0)),
                  pl.BlockSpec((k1, tn), lambda i, j: (0, j)),
                  pl.BlockSpec((k2, tn), lambda i, j: (1, j)),
                  pl.BlockSpec((tm, tn), lambda i, j: (jnp.minimum(i, nct - 1), jnp.where(i < nct, j, 0))),
                  pl.BlockSpec((tm, tn), lambda i, j: (jnp.maximum(i - nct, 0), jnp.where(i >= nct, j, 0))),
                  pl.BlockSpec((1, mod.shape[1], tn), lambda i, j: (mrow(i), 0, j))],
        out_specs=pl.BlockSpec((tm, tn), lambda i, j: (i, j)),
        out_shape=jax.ShapeDtypeStruct((m, n), F32),
        compiler_params=_cparams("arbitrary", "arbitrary"),
        name="out_proj",
    )(a1, a2, w_out, w_out, xc, xl, mod)


def _router_kernel(x_ref, w_ref, mod_ref, wr_ref, br_ref, h_ref, eid_ref, wt_ref,
                   *, shift_idx, scale_idx, n_groups, per_group):
    x = x_ref[...]
    y = x * lax.rsqrt(jnp.mean(x * x, axis=-1, keepdims=True) + EPS) * w_ref[...]
    y = y * (1.0 + mod_ref[0, scale_idx:scale_idx + 1, :]) + mod_ref[0, shift_idx:shift_idx + 1, :]
    h_ref[...] = y
    y_hi, y_lo = _split_bf16(y)
    w_hi, w_lo = _split_bf16(wr_ref[...])
    logits = _dot(y_hi, w_hi) + _dot(y_lo, w_hi) + _dot(y_hi, w_lo) + br_ref[...]

    n_exp = n_groups * per_group
    lane = lax.broadcasted_iota(I32, logits.shape, 1)
    gmask = lane < n_groups
    gl = jnp.where(gmask, logits, NEG_BIG)
    gmax = jnp.max(gl, axis=1, keepdims=True)
    gsum = jnp.sum(jnp.where(gmask, jnp.exp(gl - gmax), 0.0), axis=1, keepdims=True)
    g_w = 1.0 / gsum
    g_idx = jnp.min(jnp.where(gl == gmax, lane, LANES), axis=1, keepdims=True)

    in_group = jnp.logical_and(lane >= n_groups + g_idx * per_group,
                               lane < n_groups + (g_idx + 1) * per_group)
    in_group = jnp.logical_and(in_group, lane < n_groups + n_exp)
    el = jnp.where(in_group, logits, NEG_BIG)
    m1 = jnp.max(el, axis=1, keepdims=True)
    i1 = jnp.min(jnp.where(el == m1, lane, LANES), axis=1, keepdims=True)
    el2 = jnp.where(lane == i1, NEG_BIG, el)
    m2 = jnp.max(el2, axis=1, keepdims=True)
    i2 = jnp.min(jnp.where(el2 == m2, lane, LANES), axis=1, keepdims=True)
    e2 = jnp.exp(m2 - m1)
    p1 = 1.0 / (1.0 + e2)
    p2 = e2 * p1
    eid_ref[...] = jnp.where(lane == 0, i1 - n_groups, jnp.where(lane == 1, i2 - n_groups, 0))
    wt_ref[...] = jnp.where(lane == 0, g_w * p1, jnp.where(lane == 1, g_w * p2, 0.0))


def _router(x, w, mod, w_r, b_r, n_ctx_rows, t_lat, shift_idx, scale_idx, n_groups, per_group):
    m, d = x.shape
    tm = _divisor(math.gcd(n_ctx_rows, t_lat), 256)
    mrow = functools.partial(_mod_row, tm=tm, n_ctx_rows=n_ctx_rows, t_lat=t_lat)
    return pl.pallas_call(
        functools.partial(_router_kernel, shift_idx=shift_idx, scale_idx=scale_idx,
                          n_groups=n_groups, per_group=per_group),
        grid=(m // tm,),
        in_specs=[pl.BlockSpec((tm, d), lambda i: (i, 0)),
                  pl.BlockSpec((1, d), lambda i: (0, 0)),
                  pl.BlockSpec((1, mod.shape[1], d), lambda i: (mrow(i), 0, 0)),
                  pl.BlockSpec((d, LANES), lambda i: (0, 0)),
                  pl.BlockSpec((1, LANES), lambda i: (0, 0))],
        out_specs=[pl.BlockSpec((tm, d), lambda i: (i, 0)),
                   pl.BlockSpec((tm, LANES), lambda i: (i, 0)),
                   pl.BlockSpec((tm, LANES), lambda i: (i, 0))],
        out_shape=[jax.ShapeDtypeStruct((m, d), F32),
                   jax.ShapeDtypeStruct((m, LANES), I32),
                   jax.ShapeDtypeStruct((m, LANES), F32)],
        compiler_params=_cparams("parallel"),
        name="router",
    )(x, w.reshape(1, d), mod, w_r, b_r)


GATHER_UNROLL = 8


def _row_copy(src_hbm, dst_buf, sem, src_row, slot, dst_row):
    return pltpu.make_async_copy(src_hbm.at[pl.ds(src_row, 1), :],
                                 dst_buf.at[slot, pl.ds(dst_row, 1), :], sem.at[slot])


def _start_row_gather(idx_ref, src_hbm, buf, sem, slot, n_idx):
    def body(j, carry):
        _row_copy(src_hbm, buf, sem, idx_ref[0, 0, j], slot, j).start()
        return carry

    lax.fori_loop(0, n_idx, body, 0, unroll=GATHER_UNROLL)


def _wait_row_gather(src_hbm, buf, sem, slot, n_idx):
    def body(j, carry):
        _row_copy(src_hbm, buf, sem, 0, slot, j).wait()
        return carry

    lax.fori_loop(0, n_idx, body, 0, unroll=GATHER_UNROLL)


def _gather_rows_kernel(idx_ref, nxt_ref, src_hbm, o_ref, buf, sem, *, rows):
    i = pl.program_id(0)
    slot = i % 2

    @pl.when(i == 0)
    def _prime():
        _start_row_gather(idx_ref, src_hbm, buf, sem, slot, rows)

    @pl.when(i + 1 < pl.num_programs(0))
    def _prefetch():
        _start_row_gather(nxt_ref, src_hbm, buf, sem, 1 - slot, rows)

    _wait_row_gather(src_hbm, buf, sem, slot, rows)
    o_ref[...] = buf[slot].astype(o_ref.dtype)


def _gather_rows(src, slot_src, rows, out_dtype):
    n_slots = slot_src.shape[0]
    nblk = n_slots // rows
    d = src.shape[1]
    idx3 = slot_src.reshape(nblk, 1, rows)
    return pl.pallas_call(
        functools.partial(_gather_rows_kernel, rows=rows),
        grid=(nblk,),
        in_specs=[pl.BlockSpec((1, 1, rows), lambda i: (i, 0, 0), memory_space=pltpu.SMEM),
                  pl.BlockSpec((1, 1, rows), lambda i: (jnp.minimum(i + 1, nblk - 1), 0, 0),
                               memory_space=pltpu.SMEM),
                  pl.BlockSpec(memory_space=pl.ANY)],
        out_specs=pl.BlockSpec((rows, d), lambda i: (i, 0)),
        scratch_shapes=[pltpu.VMEM((2, rows, d), src.dtype), pltpu.SemaphoreType.DMA((2,))],
        out_shape=jax.ShapeDtypeStruct((n_slots, d), out_dtype),
        compiler_params=_cparams("arbitrary"),
        name="moe_dispatch",
    )(idx3, idx3, src)


def _moe_up_kernel(xblk, eid, wjt, oblk, ojt, fst, valid, x_ref, wg_ref, wu_ref, h_ref, wg_s, wu_s):
    w = pl.program_id(0)

    @pl.when(valid[w] == 1)
    def _go():
        @pl.when(fst[w] == 1)
        def _cast():
            wg_s[...] = wg_ref[0].astype(BF16)
            wu_s[...] = wu_ref[0].astype(BF16)

        x = x_ref[...]
        a = _dot(x, wg_s[...])
        b = _dot(x, wu_s[...])
        h_ref[...] = (a * _sigmoid(a) * b).astype(h_ref.dtype)

    @pl.when(valid[w] == 0)
    def _pad():
        h_ref[...] = jnp.zeros(h_ref.shape, h_ref.dtype)


def _moe_up(xs, w_gate, w_up, tabs, n_work, tn):
    n_slots, d = xs.shape
    f = w_gate.shape[2]
    return pl.pallas_call(
        _moe_up_kernel,
        grid_spec=pltpu.PrefetchScalarGridSpec(
            num_scalar_prefetch=7, grid=(n_work,),
            in_specs=[pl.BlockSpec((MOE_TILE, d), lambda w, xb, e, wj, *_: (xb[w], 0)),
                      pl.BlockSpec((1, d, tn), lambda w, xb, e, wj, *_: (e[w], 0, wj[w])),
                      pl.BlockSpec((1, d, tn), lambda w, xb, e, wj, *_: (e[w], 0, wj[w]))],
            out_specs=pl.BlockSpec((MOE_TILE, tn), lambda w, xb, e, wj, ob, oj, *_: (ob[w], oj[w])),
            scratch_shapes=[pltpu.VMEM((d, tn), BF16), pltpu.VMEM((d, tn), BF16)]),
        out_shape=jax.ShapeDtypeStruct((n_slots, f), BF16),
        compiler_params=_cparams("arbitrary"),
        name="moe_up",
    )(*tabs, xs, w_gate, w_up)


def _moe_down_kernel(eid, fst, valid, h_ref, wd_ref, y_ref, wd_s):
    w = pl.program_id(0)

    @pl.when(valid[w] == 1)
    def _go():
        @pl.when(fst[w] == 1)
        def _cast():
            wd_s[...] = wd_ref[0].astype(BF16)

        y_ref[...] = _dot(h_ref[...], wd_s[...])

    @pl.when(valid[w] == 0)
    def _pad():
        y_ref[...] = jnp.zeros(y_ref.shape, y_ref.dtype)


def _moe_down(hs, w_down, tabs):
    n_slots, f = hs.shape
    d = w_down.shape[2]
    nblk = n_slots // MOE_TILE
    return pl.pallas_call(
        _moe_down_kernel,
        grid_spec=pltpu.PrefetchScalarGridSpec(
            num_scalar_prefetch=3, grid=(nblk,),
            in_specs=[pl.BlockSpec((MOE_TILE, f), lambda w, eid, *_: (w, 0)),
                      pl.BlockSpec((1, f, d), lambda w, eid, *_: (eid[w], 0, 0))],
            out_specs=pl.BlockSpec((MOE_TILE, d), lambda w, eid, *_: (w, 0)),
            scratch_shapes=[pltpu.VMEM((f, d), BF16)]),
        out_shape=jax.ShapeDtypeStruct((n_slots, d), F32),
        compiler_params=_cparams("arbitrary"),
        name="moe_down",
    )(*tabs, hs, w_down)


def _final_kernel(dst_ref, nxt_ref, x_ref, wt_ref, mod_ref, fw_ref, ys_hbm, o_ref, buf, sem, *, rows, gate_idx):
    i = pl.program_id(0)
    slot = i % 2

    @pl.when(i == 0)
    def _prime():
        _start_row_gather(dst_ref, ys_hbm, buf, sem, slot, TOP_K * rows)

    @pl.when(i + 1 < pl.num_programs(0))
    def _prefetch():
        _start_row_gather(nxt_ref, ys_hbm, buf, sem, 1 - slot, TOP_K * rows)

    _wait_row_gather(ys_hbm, buf, sem, slot, TOP_K * rows)

    wt = wt_ref[...]
    moe = wt[:, 0:1] * buf[slot, 0:rows, :]
    for kk in range(1, TOP_K):
        moe = moe + wt[:, kk:kk + 1] * buf[slot, kk * rows:(kk + 1) * rows, :]
    x = x_ref[...] + mod_ref[0, gate_idx:gate_idx + 1, :] * moe
    o_ref[...] = x * lax.rsqrt(jnp.mean(x * x, axis=-1, keepdims=True) + EPS) * fw_ref[...]


def _final(x1, wt, dest, ys, mod, final_w, row0, n_rows, n_ctx_rows, t_lat, gate_idx):
    d = x1.shape[1]
    tm = _divisor(math.gcd(n_ctx_rows, t_lat), 128)
    t0 = row0 // tm
    nt = n_rows // tm
    mrow = functools.partial(_mod_row, tm=tm, n_ctx_rows=n_ctx_rows, t_lat=t_lat)
    dest3 = dest.reshape(-1, tm, TOP_K).transpose(0, 2, 1).reshape(-1, 1, TOP_K * tm)
    return pl.pallas_call(
        functools.partial(_final_kernel, rows=tm, gate_idx=gate_idx),
        grid=(nt,),
        in_specs=[pl.BlockSpec((1, 1, TOP_K * tm), lambda i: (t0 + i, 0, 0), memory_space=pltpu.SMEM),
                  pl.BlockSpec((1, 1, TOP_K * tm), lambda i: (t0 + jnp.minimum(i + 1, nt - 1), 0, 0),
                               memory_space=pltpu.SMEM),
                  pl.BlockSpec((tm, d), lambda i: (t0 + i, 0)),
                  pl.BlockSpec((tm, LANES), lambda i: (t0 + i, 0)),
                  pl.BlockSpec((1, mod.shape[1], d), lambda i: (mrow(t0 + i), 0, 0)),
                  pl.BlockSpec((1, d), lambda i: (0, 0)),
                  pl.BlockSpec(memory_space=pl.ANY)],
        out_specs=pl.BlockSpec((tm, d), lambda i: (i, 0)),
        out_shape=jax.ShapeDtypeStruct((n_rows, d), F32),
        scratch_shapes=[pltpu.VMEM((2, TOP_K * tm, d), F32), pltpu.SemaphoreType.DMA((2,))],
        compiler_params=_cparams("arbitrary"),
        name="moe_combine_final",
    )(dest3, dest3, x1, wt, mod, final_w.reshape(1, d), ys)


def _count_le(sorted_ends, idx):
    return jnp.sum((sorted_ends[None, :] <= idx[:, None]).astype(I32), axis=1)


def _moe_plan(eid, n_exp, n_jt):
    t = eid.shape[0]
    n_assign = t * TOP_K
    eflat = eid.reshape(n_assign)
    onehot = (eflat[:, None] == jnp.arange(n_exp, dtype=I32)[None, :]).astype(I32)
    csum = jnp.cumsum(onehot, axis=0)
    counts = csum[-1]
    rank = jnp.sum(onehot * (csum - 1), axis=1)
    nb = (counts + MOE_TILE - 1) // MOE_TILE
    blk_end = jnp.cumsum(nb)
    blk_start = blk_end - nb
    n_blocks = blk_end[-1]
    pad_start = blk_start * MOE_TILE
    dest = pad_start[eflat] + rank

    nblk_max = n_assign // MOE_TILE + n_exp
    n_slots = nblk_max * MOE_TILE
    order = jnp.argsort(eflat, stable=True).astype(I32)
    start = jnp.cumsum(counts) - counts
    slot = jnp.arange(n_slots, dtype=I32)
    sblk = slot // MOE_TILE
    blk_e = jnp.minimum(_count_le(blk_end, jnp.arange(nblk_max, dtype=I32)), n_exp - 1)
    se = blk_e[sblk]
    off = slot - pad_start[se]
    ok = jnp.logical_and(sblk < n_blocks, off < counts[se])
    src_a = order[jnp.clip(start[se] + off, 0, n_assign - 1)]
    slot_tok = jnp.where(ok, src_a // TOP_K, 0).astype(I32)

    b_idx = jnp.arange(nblk_max, dtype=I32)
    d_valid = (b_idx < n_blocks).astype(I32)
    last_e = blk_e[jnp.maximum(n_blocks - 1, 0)]
    d_eid = jnp.where(d_valid == 1, blk_e, last_e).astype(I32)
    d_first = jnp.logical_and(d_valid == 1, b_idx == blk_start[d_eid]).astype(I32)

    n_work = n_jt * nblk_max
    w_idx = jnp.arange(n_work, dtype=I32)
    per_e = n_jt * nb
    w_end = jnp.cumsum(per_e)
    w_valid = w_idx < w_end[-1]
    we = jnp.minimum(_count_le(w_end, w_idx), n_exp - 1)
    r = w_idx - (w_end[we] - per_e[we])
    nbe = jnp.maximum(nb[we], 1)
    u_jt = r // nbe
    u_t = r - u_jt * nbe
    u_blk = blk_start[we] + u_t
    last_w = jnp.maximum(w_end[-1] - 1, 0)
    spare = w_idx - w_end[-1]
    u_xblk = jnp.where(w_valid, u_blk, u_blk[last_w]).astype(I32)
    u_eid = jnp.where(w_valid, we, we[last_w]).astype(I32)
    u_wjt = jnp.where(w_valid, u_jt, u_jt[last_w]).astype(I32)
    u_oblk = jnp.where(w_valid, u_blk, n_blocks + spare // n_jt).astype(I32)
    u_ojt = jnp.where(w_valid, u_jt, spare % n_jt).astype(I32)
    u_first = jnp.logical_and(w_valid, u_t == 0).astype(I32)
    up_tabs = (u_xblk, u_eid, u_wjt, u_oblk, u_ojt, u_first, w_valid.astype(I32))
    down_tabs = (d_eid, d_first, d_valid)
    return dest.astype(I32), slot_tok, up_tabs, down_tabs, n_work


def kernel(x_prompt, x_sample, state_mlstm_C, state_mlstm_n, state_mlstm_m, state_ret_S, c, c_ctx,
           norm1_w, norm2_w, w_ada, b_ada, w_in, b_mgates, mlstm_norm_w, ret_norm_w, ret_log_decay, w_out,
           w_router_group, b_router_group, w_router_expert, b_router_expert, w_exp_gate, w_exp_up,
           w_exp_down, final_norm_w):
    n_ctx, t_ctx, d_model = x_prompt.shape
    n_lat, t_lat, _ = x_sample.shape
    depth = norm1_w.shape[0]
    assert depth == 1, "single-layer trunk"
    m_heads, m_dk, m_dv = state_mlstm_C.shape[3:]
    r_heads, r_dk, r_dv = state_ret_S.shape[3:]
    n_groups = w_router_group.shape[2]
    n_exp = w_router_expert.shape[2]
    per_group = n_exp // n_groups
    d_exp = w_exp_gate.shape[3]
    n_gates = N_GATE_ROWS * m_heads
    n_ctx_rows = n_ctx * t_ctx
    m_rows = n_ctx_rows + n_lat * t_lat
    assert t_ctx % CHUNK == 0 and t_lat % CHUNK == 0 and n_gates + 0 <= LANES
    assert n_groups + n_exp <= LANES

    cvec = jnp.concatenate([c_ctx[None, :], c, jnp.zeros((SUBLANES - 1 - n_lat, d_model), F32)], axis=0)
    mod = _ada_mod(cvec, w_ada[0], b_ada[0]).reshape(SUBLANES, 6, d_model)

    xc = x_prompt.reshape(n_ctx_rows, d_model)
    xl = x_sample.reshape(n_lat * t_lat, d_model)
    h = _norm_mod(xc, xl, norm1_w[0], mod, t_lat, 0, 1, BF16)

    g0 = 2 * m_heads * m_dk + 2 * m_heads * m_dv
    w_main = jnp.concatenate([w_in[0][:, :g0], w_in[0][:, g0 + n_gates:]], axis=1).astype(BF16)
    w_gate = jnp.pad(w_in[0][:, g0:g0 + n_gates], ((0, 0), (0, LANES - n_gates))).astype(BF16)
    z, zg = _inproj(h, w_main, w_gate)

    gbias = jnp.pad(b_mgates[0].reshape(1, n_gates), ((0, 0), (0, LANES - n_gates)))
    st_m = jnp.broadcast_to(state_mlstm_m[:, 0][..., None], (n_lat, 2, m_heads, LANES))
    mdims = (n_ctx, t_ctx, n_lat, t_lat, m_heads, m_dk, m_dv)
    hf, cf, nf, mf = _mlstm_scan(z, zg, gbias, state_mlstm_C[:, 0], state_mlstm_n[:, 0], st_m, mdims, False)
    mix_m, cb, nb, mb = _mlstm_scan(z, zg, gbias, state_mlstm_C[:, 0], state_mlstm_n[:, 0], st_m, mdims, True,
                                    hf=hf, norm_w=mlstm_norm_w[0])

    cos_tab, sin_tab = _rope_tables(t_lat, r_dk)
    ld = jnp.pad(ret_log_decay[0], ((0, SUBLANES - 2), (0, LANES - r_heads)))
    rdims = (n_ctx, t_ctx, n_lat, t_lat, r_heads, r_dk, r_dv)
    of, sf = _ret_scan(z, cos_tab, sin_tab, ld, state_ret_S[:, 0], rdims, g0, False)
    mix_r, sb = _ret_scan(z, cos_tab, sin_tab, ld, state_ret_S[:, 0], rdims, g0, True,
                          of=of, norm_w=ret_norm_w[0])

    x1 = _outproj(mix_m, mix_r, w_out[0].astype(BF16), xc, xl, mod, t_lat, 2)

    w_r = jnp.pad(jnp.concatenate([w_router_group[0], w_router_expert[0]], axis=1),
                  ((0, 0), (0, LANES - n_groups - n_exp)))
    b_r = jnp.pad(jnp.concatenate([b_router_group[0], b_router_expert[0]])[None, :],
                  ((0, 0), (0, LANES - n_groups - n_exp)))
    h2, eid, wt = _router(x1, norm2_w[0], mod, w_r, b_r, n_ctx_rows, t_lat, 3, 4, n_groups, per_group)

    tn_up = _divisor(d_exp, 512, LANES)
    dest, slot_tok, up_tabs, down_tabs, n_work = _moe_plan(eid[:, :TOP_K], n_exp, d_exp // tn_up)
    xs = _gather_rows(h2, slot_tok, MOE_TILE, BF16)
    hs = _moe_up(xs, w_exp_gate[0], w_exp_up[0], up_tabs, n_work, tn_up)
    ys = _moe_down(hs, w_exp_down[0], down_tabs)

    y_ctx = _final(x1, wt, dest, ys, mod, final_norm_w, 0, n_ctx_rows, n_ctx_rows, t_lat, 5)
    y_lat = _final(x1, wt, dest, ys, mod, final_norm_w, n_ctx_rows, n_lat * t_lat, n_ctx_rows, t_lat, 5)

    new_c = jnp.stack([cf, cb], axis=1)[:, None]
    new_n = jnp.stack([nf, nb], axis=1)[:, None]
    new_m = jnp.stack([mf[..., 0], mb[..., 0]], axis=1)[:, None]
    new_s = jnp.stack([sf, sb], axis=1)[:, None]
    return (y_ctx.reshape(n_ctx, t_ctx, d_model), y_lat.reshape(n_lat, t_lat, d_model),
            new_c, new_n, new_m, new_s)
```

```python
import functools
import math

import numpy as np
import jax
import jax.numpy as jnp
from jax import lax
from jax.experimental import pallas as pl
from jax.experimental.pallas import tpu as pltpu

F32 = jnp.float32
BF16 = jnp.bfloat16
I32 = jnp.int32

CHUNK = 128
GRID_W = 64
ROPE_BASE = 10000.0
EPS = 1e-6
TOP_K = 2
N_GATE_ROWS = 4

LANES = 128
SUBLANES = 8
VMEM_LIMIT_BYTES = 56 * 1024 * 1024
NEG_BIG = -1e30
MOE_TILE = 256


def _divisor(n, pref, mult=SUBLANES):
    if n <= pref:
        return n
    d = (pref // mult) * mult
    while d > mult and n % d:
        d -= mult
    assert n % d == 0, (n, pref, mult)
    return d


def _cparams(*sem):
    return pltpu.CompilerParams(dimension_semantics=sem, vmem_limit_bytes=VMEM_LIMIT_BYTES)


def _dot(a, b):
    return jnp.dot(a, b, preferred_element_type=F32)


def _dot_nt(a, b):
    return lax.dot_general(a, b, (((1,), (1,)), ((), ())), preferred_element_type=F32)


def _dot_tn(a, b):
    return lax.dot_general(a, b, (((0,), (0,)), ((), ())), preferred_element_type=F32)


def _split_bf16(x):
    hi = x.astype(BF16)
    lo = (x - hi.astype(F32)).astype(BF16)
    return hi, lo


def _sigmoid(x):
    return 1.0 / (1.0 + jnp.exp(-x))


def _log_sigmoid(x):
    return jnp.minimum(x, 0.0) - jnp.log(1.0 + jnp.exp(-jnp.abs(x)))


def _mod_row(i, tm, n_ctx_rows, t_lat):
    r0 = i * tm
    return jnp.where(r0 < n_ctx_rows, 0, 1 + (r0 - n_ctx_rows) // t_lat)


def _store_token_major(ref, y):
    w = ref.shape[-1]
    for s in range(SUBLANES):
        ref[:, s, :] = y[:, s * w:(s + 1) * w]


def _load_token_major(ref, slot, r0, rows):
    return jnp.concatenate([ref[slot, r0:r0 + rows, s, :] for s in range(SUBLANES)], axis=1)


def _ada_kernel(c_ref, w_ref, b_ref, o_ref):
    c = c_ref[...]
    s = c * _sigmoid(c)
    s_hi, s_lo = _split_bf16(s)
    w_hi, w_lo = _split_bf16(w_ref[...])
    o_ref[...] = _dot(s_hi, w_hi) + _dot(s_lo, w_hi) + _dot(s_hi, w_lo) + b_ref[...]


def _ada_mod(cvec, w_ada, b_ada):
    r, d = cvec.shape
    n = w_ada.shape[1]
    tn = _divisor(n, 256, LANES)
    return pl.pallas_call(
        _ada_kernel,
        grid=(n // tn,),
        in_specs=[pl.BlockSpec((r, d), lambda j: (0, 0)),
                  pl.BlockSpec((d, tn), lambda j: (0, j)),
                  pl.BlockSpec((1, tn), lambda j: (0, j))],
        out_specs=pl.BlockSpec((r, tn), lambda j: (0, j)),
        out_shape=jax.ShapeDtypeStruct((r, n), F32),
        compiler_params=_cparams("parallel"),
        name="ada_mod",
    )(cvec, w_ada, b_ada.reshape(1, n))


def _norm_mod_kernel(xc_ref, xl_ref, w_ref, mod_ref, o_ref, *, shift_idx, scale_idx, n_ctx_tiles):
    def body(x_ref):
        x = x_ref[...]
        y = x * lax.rsqrt(jnp.mean(x * x, axis=-1, keepdims=True) + EPS) * w_ref[...]
        y = y * (1.0 + mod_ref[0, scale_idx:scale_idx + 1, :]) + mod_ref[0, shift_idx:shift_idx + 1, :]
        o_ref[...] = y.astype(o_ref.dtype)

    i = pl.program_id(0)
    pl.when(i < n_ctx_tiles)(lambda: body(xc_ref))
    pl.when(i >= n_ctx_tiles)(lambda: body(xl_ref))


def _norm_mod(xc, xl, w, mod, t_lat, shift_idx, scale_idx, out_dtype):
    n_ctx_rows, d = xc.shape
    m = n_ctx_rows + xl.shape[0]
    tm = _divisor(math.gcd(n_ctx_rows, t_lat), 256)
    nct = n_ctx_rows // tm
    mrow = functools.partial(_mod_row, tm=tm, n_ctx_rows=n_ctx_rows, t_lat=t_lat)
    return pl.pallas_call(
        functools.partial(_norm_mod_kernel, shift_idx=shift_idx, scale_idx=scale_idx, n_ctx_tiles=nct),
        grid=(m // tm,),
        in_specs=[pl.BlockSpec((tm, d), lambda i: (jnp.minimum(i, nct - 1), 0)),
                  pl.BlockSpec((tm, d), lambda i: (jnp.maximum(i - nct, 0), 0)),
                  pl.BlockSpec((1, d), lambda i: (0, 0)),
                  pl.BlockSpec((1, mod.shape[1], d), lambda i: (mrow(i), 0, 0))],
        out_specs=pl.BlockSpec((tm, d), lambda i: (i, 0)),
        out_shape=jax.ShapeDtypeStruct((m, d), out_dtype),
        compiler_params=_cparams("arbitrary"),
        name="norm_mod",
    )(xc, xl, w.reshape(1, d), mod)


def _inproj_kernel(a_ref, b_ref, wg_ref, z_ref, zg_ref):
    a = a_ref[...]
    z_ref[...] = _dot(a, b_ref[...]).astype(z_ref.dtype)

    @pl.when(pl.program_id(1) == 0)
    def _gates():
        zg_ref[...] = _dot(a, wg_ref[...])


def _inproj(h, w_main, w_gate):
    m, k = h.shape
    n = w_main.shape[1]
    tm = _divisor(m, 1024)
    tn = _divisor(n, 1024, LANES)
    return pl.pallas_call(
        _inproj_kernel,
        grid=(m // tm, n // tn),
        in_specs=[pl.BlockSpec((tm, k), lambda i, j: (i, 0)),
                  pl.BlockSpec((k, tn), lambda i, j: (0, j)),
                  pl.BlockSpec((k, LANES), lambda i, j: (0, 0))],
        out_specs=[pl.BlockSpec((tm, tn), lambda i, j: (i, j)),
                   pl.BlockSpec((tm, LANES), lambda i, j: (i, 0))],
        out_shape=[jax.ShapeDtypeStruct((m, n), BF16),
                   jax.ShapeDtypeStruct((m, LANES), F32)],
        compiler_params=_cparams("parallel", "arbitrary"),
        name="in_proj",
    )(h, w_main, w_gate)


CHUNKS_PER_STEP = 2


def _scan_schedule(n_ctx, t_ctx, n_lat, t_lat, reverse):
    rowblk, first, last, ctxb, latb, islat, posblk = [], [], [], [], [], [], []
    step_rows = CHUNK * CHUNKS_PER_STEP
    ns_ctx, ns_lat = t_ctx // step_rows, t_lat // step_rows
    for b in range(n_ctx):
        order = range(ns_ctx - 1, -1, -1) if reverse else range(ns_ctx)
        for pos, c in enumerate(order):
            rowblk.append(b * ns_ctx + c)
            first.append(int(pos == 0))
            last.append(int(pos == ns_ctx - 1))
            ctxb.append(b)
            latb.append(0)
            islat.append(0)
            posblk.append(ns_lat)
    base = n_ctx * ns_ctx
    for b in range(n_lat):
        order = range(ns_lat - 1, -1, -1) if reverse else range(ns_lat)
        for pos, c in enumerate(order):
            rowblk.append(base + b * ns_lat + c)
            first.append(int(pos == 0))
            last.append(int(pos == ns_lat - 1))
            ctxb.append(n_ctx - 1)
            latb.append(b)
            islat.append(1)
            posblk.append(c)
    tabs = (rowblk, first, last, ctxb, latb, islat, posblk)
    return tuple(jnp.asarray(np.asarray(t, np.int32)) for t in tabs)


def _chunk_rows(ci, reverse):
    cc = (CHUNKS_PER_STEP - 1 - ci) if reverse else ci
    return pl.ds(pl.multiple_of(cc * CHUNK, CHUNK), CHUNK)


def _mlstm_kernel(rowblk, first, last, ctxb, latb, islat, posblk,
                  q_ref, k_ref, v_ref, g_ref, gb_ref, c0_ref, n0_ref, m0_ref, *rest,
                  heads, dk, dv, reverse, combine):
    if combine:
        (hf_ref, o_ref, nw_ref, cf_ref, nf_ref, mf_ref,
         out_ref, cout_ref, nout_ref, mout_ref, c_s, n_s, m_s) = rest
    else:
        out_ref, cout_ref, nout_ref, mout_ref, c_s, n_s, m_s = rest
    w = pl.program_id(0)
    lat = islat[w] == 1

    @pl.when(first[w] == 1)
    def _init():
        c_s[...] = jnp.where(lat, c0_ref[0, 0], 0.0)
        n_s[...] = jnp.where(lat, n0_ref[0, 0], 0.0)
        m_s[...] = jnp.where(lat, m0_ref[0, 0], 0.0)

    L = CHUNK
    ti = lax.broadcasted_iota(I32, (L, L), 0)
    si = lax.broadcasted_iota(I32, (L, L), 1)
    mask = (si >= ti) if reverse else (si <= ti)
    mask_t = (ti >= si) if reverse else (ti <= si)
    scale = dk ** -0.5
    gate_row = 2 if reverse else 0
    hs_ = range(heads)

    def chunk(ci, carry):
        rows = _chunk_rows(ci, reverse)
        g = g_ref[rows, :] + gb_ref[...]
        gt = g.T
        lg = _log_sigmoid(g)
        lgt = _log_sigmoid(gt)

        st = []
        for h in hs_:
            ci_, cf_ = gate_row * heads + h, (gate_row + 1) * heads + h
            i_col, i_row = g[:, ci_:ci_ + 1], gt[ci_:ci_ + 1, :]
            lf_col, lf_row = lg[:, cf_:cf_ + 1], lgt[cf_:cf_ + 1, :]
            b_col = jnp.sum(jnp.where(mask, lf_row, 0.0), axis=1, keepdims=True)
            b_row = jnp.sum(jnp.where(mask_t, lf_col, 0.0), axis=0, keepdims=True)
            bl = jnp.sum(lf_row, axis=1, keepdims=True)
            m_old = m_s[h:h + 1, 0:1]
            dm = jnp.where(mask, b_col - b_row + i_row, NEG_BIG)
            inter = b_col + m_old
            mt = jnp.maximum(jnp.max(dm, axis=1, keepdims=True), inter)
            p = jnp.exp(dm - mt)
            wi = jnp.exp(inter - mt)
            g_col = bl - b_col + i_col
            g_row = bl - b_row + i_row
            m_new = jnp.maximum(bl + m_old, jnp.max(g_row, axis=1, keepdims=True))
            a_prev = jnp.exp(bl + m_old - m_new)
            a_col = jnp.exp(g_col - m_new) * scale
            st.append((p, wi, mt, m_new, a_prev, a_col))

        qbs = [q_ref[rows, h * dk:(h + 1) * dk] for h in hs_]
        kbs = [k_ref[rows, h * dk:(h + 1) * dk] for h in hs_]
        vbs = [v_ref[rows, h * dv:(h + 1) * dv] for h in hs_]
        ss = [_dot_nt(qbs[h], kbs[h]) * (scale * st[h][0]) for h in hs_]
        qcs = [_dot(qbs[h], c_s[h].astype(BF16)) for h in hs_]
        n_olds = [n_s[h:h + 1, :] for h in hs_]
        for h in hs_:
            p, wi, mt, m_new, a_prev, a_col = st[h]
            s_ = ss[h]
            num = _dot(s_.astype(BF16), vbs[h]) + wi * qcs[h]
            qn = jnp.sum(qbs[h].astype(F32) * n_olds[h], axis=1, keepdims=True)
            den = jnp.sum(s_, axis=1, keepdims=True) + wi * qn
            hval = num / jnp.maximum(jnp.abs(den), jnp.exp(-mt))
            if combine:
                hs = hval + hf_ref[rows, h * dv:(h + 1) * dv].astype(F32)
                y = hs * lax.rsqrt(jnp.mean(hs * hs, axis=-1, keepdims=True) + EPS)
                gate = o_ref[rows, h * dv:(h + 1) * dv].astype(F32)
                y = y * nw_ref[:, h * dv:(h + 1) * dv] * _sigmoid(gate)
                out_ref[rows, h * dv:(h + 1) * dv] = y.astype(out_ref.dtype)
            else:
                out_ref[rows, h * dv:(h + 1) * dv] = hval.astype(out_ref.dtype)
        for h in hs_:
            p, wi, mt, m_new, a_prev, a_col = st[h]
            ak = a_col * kbs[h].astype(F32)
            c_s[h] = a_prev * c_s[h] + _dot_tn(ak.astype(BF16), vbs[h])
            n_s[h:h + 1, :] = a_prev * n_olds[h] + jnp.sum(ak, axis=0, keepdims=True)
            m_s[h:h + 1, :] = jnp.broadcast_to(m_new, (1, LANES))
        return carry

    lax.fori_loop(0, CHUNKS_PER_STEP, chunk, 0)

    @pl.when(jnp.logical_and(last[w] == 1, jnp.logical_not(lat)))
    def _emit():
        if combine:
            cout_ref[0, 0] = cf_ref[0]
            nout_ref[0, 0] = nf_ref[0]
            mout_ref[0, 0] = mf_ref[0]
            cout_ref[0, 1] = c_s[...]
            nout_ref[0, 1] = n_s[...]
            mout_ref[0, 1] = m_s[...]
        else:
            cout_ref[0] = c_s[...]
            nout_ref[0] = n_s[...]
            mout_ref[0] = m_s[...]


def _mlstm_scan(z, zg, gbias, st_c, st_n, st_m, dims, reverse, fwd=None, norm_w=None):
    (n_ctx, t_ctx, n_lat, t_lat, heads, dk, dv) = dims
    combine = fwd is not None
    m = z.shape[0]
    d = 1 if reverse else 0
    step_rows = CHUNK * CHUNKS_PER_STEP
    tabs = _scan_schedule(n_ctx, t_ctx, n_lat, t_lat, reverse)
    qw, vw = heads * dk, heads * dv
    assert (2 * qw) % vw == 0
    v_blk = (2 * qw) // vw
    in_specs = [
        pl.BlockSpec((step_rows, qw), lambda w, rb, *_: (rb[w], 0)),
        pl.BlockSpec((step_rows, qw), lambda w, rb, *_: (rb[w], 1)),
        pl.BlockSpec((step_rows, vw), lambda w, rb, *_: (rb[w], v_blk)),
        pl.BlockSpec((step_rows, LANES), lambda w, rb, *_: (rb[w], 0)),
        pl.BlockSpec((1, LANES), lambda w, *_: (0, 0)),
        pl.BlockSpec((1, 1, heads, dk, dv), lambda w, rb, f, l, cb, lb, *_: (lb[w], d, 0, 0, 0)),
        pl.BlockSpec((1, 1, heads, dk), lambda w, rb, f, l, cb, lb, *_: (lb[w], d, 0, 0)),
        pl.BlockSpec((1, 1, heads, LANES), lambda w, rb, f, l, cb, lb, *_: (lb[w], d, 0, 0)),
    ]
    args = [z, z, z, zg, gbias, st_c, st_n, st_m]
    if combine:
        hf, cf, nf, mf = fwd
        in_specs += [
            pl.BlockSpec((step_rows, vw), lambda w, rb, *_: (rb[w], 0)),
            pl.BlockSpec((step_rows, vw), lambda w, rb, *_: (rb[w], v_blk + 1)),
            pl.BlockSpec((1, vw), lambda w, *_: (0, 0)),
            pl.BlockSpec((1, heads, dk, dv), lambda w, rb, f, l, cb, *_: (cb[w], 0, 0, 0)),
            pl.BlockSpec((1, heads, dk), lambda w, rb, f, l, cb, *_: (cb[w], 0, 0)),
            pl.BlockSpec((1, heads, LANES), lambda w, rb, f, l, cb, *_: (cb[w], 0, 0)),
        ]
        args += [hf, z, norm_w.reshape(1, vw), cf, nf, mf]
        nd = (2,)
        st_idx = lambda w, rb, f, l, cb, *_: (cb[w], 0, 0, 0)
        c_idx = lambda w, rb, f, l, cb, *_: (cb[w], 0, 0, 0, 0)
    else:
        nd = ()
        st_idx = lambda w, rb, f, l, cb, *_: (cb[w], 0, 0)
        c_idx = lambda w, rb, f, l, cb, *_: (cb[w], 0, 0, 0)
    out_specs = [
        pl.BlockSpec((step_rows, vw), lambda w, rb, *_: (rb[w], 0)),
        pl.BlockSpec((1,) + nd + (heads, dk, dv), c_idx),
        pl.BlockSpec((1,) + nd + (heads, dk), st_idx),
        pl.BlockSpec((1,) + nd + (heads, LANES), st_idx),
    ]
    out_shape = [
        jax.ShapeDtypeStruct((m, vw), BF16),
        jax.ShapeDtypeStruct((n_ctx,) + nd + (heads, dk, dv), F32),
        jax.ShapeDtypeStruct((n_ctx,) + nd + (heads, dk), F32),
        jax.ShapeDtypeStruct((n_ctx,) + nd + (heads, LANES), F32),
    ]
    return pl.pallas_call(
        functools.partial(_mlstm_kernel, heads=heads, dk=dk, dv=dv, reverse=reverse, combine=combine),
        grid_spec=pltpu.PrefetchScalarGridSpec(
            num_scalar_prefetch=len(tabs), grid=(m // step_rows,),
            in_specs=in_specs, out_specs=out_specs,
            scratch_shapes=[pltpu.VMEM((heads, dk, dv), F32),
                            pltpu.VMEM((heads, dk), F32),
                            pltpu.VMEM((heads, LANES), F32)]),
        out_shape=out_shape,
        compiler_params=_cparams("arbitrary"),
        name="mlstm_bwd" if reverse else "mlstm_fwd",
    )(*tabs, *args)


def _rope_partner(x, d):
    q, hlf = d // 4, d // 2
    return jnp.concatenate([x[:, q:hlf], x[:, :q], x[:, hlf + q:], x[:, hlf:hlf + q]], axis=1)


def _ret_kernel(rowblk, first, last, ctxb, latb, islat, posblk,
                q_ref, k_ref, v_ref, cos_ref, sin_ref, ld_ref, s0_ref, *rest,
                heads, dk, dv, reverse, combine):
    if combine:
        of_ref, g_ref, nw_ref, sf_ref, out_ref, sout_ref, s_s = rest
    else:
        out_ref, sout_ref, s_s = rest
    w = pl.program_id(0)
    lat = islat[w] == 1

    @pl.when(first[w] == 1)
    def _init():
        s_s[...] = jnp.where(lat, s0_ref[0, 0], 0.0)

    L = CHUNK
    ti = lax.broadcasted_iota(I32, (L, L), 0).astype(F32)
    si = lax.broadcasted_iota(I32, (L, L), 1).astype(F32)
    diff = (si - ti) if reverse else (ti - si)
    idx = lax.broadcasted_iota(I32, (L, 1), 0).astype(F32)
    q_pow = (L - idx) if reverse else (idx + 1.0)
    k_pow = idx if reverse else (L - 1.0 - idx)
    scale = dk ** -0.5
    drow = 1 if reverse else 0
    log_decay = -jnp.exp(ld_ref[drow:drow + 1, :])
    hs_ = range(heads)

    def chunk(ci, carry):
        rows = _chunk_rows(ci, reverse)
        cos_t, sin_t = cos_ref[rows, :], sin_ref[rows, :]
        qbs, kbs, kds, vbs = [], [], [], []
        for h in hs_:
            ld = log_decay[:, h:h + 1]
            q = q_ref[rows, h * dk:(h + 1) * dk].astype(F32)
            k = k_ref[rows, h * dk:(h + 1) * dk].astype(F32)
            q = q * cos_t + _rope_partner(q, dk) * sin_t
            k = k * cos_t + _rope_partner(k, dk) * sin_t
            qbs.append(q.astype(BF16))
            kbs.append(k.astype(BF16))
            kds.append((k * (scale * jnp.exp(k_pow * ld))).astype(BF16))
            vbs.append(v_ref[rows, h * dv:(h + 1) * dv])
        scores = [_dot_nt(qbs[h], kbs[h]) for h in hs_]
        inter = [_dot(qbs[h], s_s[h].astype(BF16)) for h in hs_]
        for h in hs_:
            ld = log_decay[:, h:h + 1]
            intra = jnp.where(diff >= 0.0, jnp.exp(jnp.maximum(diff, 0.0) * ld), 0.0)
            a = scores[h] * (scale * intra)
            o = _dot(a.astype(BF16), vbs[h]) + inter[h] * jnp.exp(q_pow * ld)
            if combine:
                hs = o + of_ref[rows, h * dv:(h + 1) * dv].astype(F32)
                y = hs * lax.rsqrt(jnp.mean(hs * hs, axis=-1, keepdims=True) + EPS)
                gate = g_ref[rows, h * dv:(h + 1) * dv].astype(F32)
                y = y * nw_ref[:, h * dv:(h + 1) * dv] * (gate * _sigmoid(gate))
                out_ref[rows, h * dv:(h + 1) * dv] = y.astype(out_ref.dtype)
            else:
                out_ref[rows, h * dv:(h + 1) * dv] = o.astype(out_ref.dtype)
        for h in hs_:
            ld = log_decay[:, h:h + 1]
            s_s[h] = jnp.exp(float(L) * ld) * s_s[h] + _dot_tn(kds[h], vbs[h])
        return carry

    lax.fori_loop(0, CHUNKS_PER_STEP, chunk, 0)

    @pl.when(jnp.logical_and(last[w] == 1, jnp.logical_not(lat)))
    def _emit():
        if combine:
            sout_ref[0, 0] = sf_ref[0]
            sout_ref[0, 1] = s_s[...]
        else:
            sout_ref[0] = s_s[...]


def _ret_scan(z, cos_tab, sin_tab, ld, st_s, dims, col0, reverse, fwd=None, norm_w=None):
    (n_ctx, t_ctx, n_lat, t_lat, heads, dk, dv) = dims
    combine = fwd is not None
    m = z.shape[0]
    d = 1 if reverse else 0
    step_rows = CHUNK * CHUNKS_PER_STEP
    tabs = _scan_schedule(n_ctx, t_ctx, n_lat, t_lat, reverse)
    qw, vw = heads * dk, heads * dv
    assert qw == vw and col0 % qw == 0
    b0 = col0 // qw
    in_specs = [
        pl.BlockSpec((step_rows, qw), lambda w, rb, *_: (rb[w], b0)),
        pl.BlockSpec((step_rows, qw), lambda w, rb, *_: (rb[w], b0 + 1)),
        pl.BlockSpec((step_rows, vw), lambda w, rb, *_: (rb[w], b0 + 2)),
        pl.BlockSpec((step_rows, dk), lambda w, rb, f, l, cb, lb, il, pb: (pb[w], 0)),
        pl.BlockSpec((step_rows, dk), lambda w, rb, f, l, cb, lb, il, pb: (pb[w], 0)),
        pl.BlockSpec((SUBLANES, LANES), lambda w, *_: (0, 0)),
        pl.BlockSpec((1, 1, heads, dk, dv), lambda w, rb, f, l, cb, lb, *_: (lb[w], d, 0, 0, 0)),
    ]
    args = [z, z, z, cos_tab, sin_tab, ld, st_s]
    if combine:
        of, sf = fwd
        in_specs += [
            pl.BlockSpec((step_rows, vw), lambda w, rb, *_: (rb[w], 0)),
            pl.BlockSpec((step_rows, vw), lambda w, rb, *_: (rb[w], b0 + 3)),
            pl.BlockSpec((1, vw), lambda w, *_: (0, 0)),
            pl.BlockSpec((1, heads, dk, dv), lambda w, rb, f, l, cb, *_: (cb[w], 0, 0, 0)),
        ]
        args += [of, z, norm_w.reshape(1, vw), sf]
        s_spec = pl.BlockSpec((1, 2, heads, dk, dv), lambda w, rb, f, l, cb, *_: (cb[w], 0, 0, 0, 0))
        s_shape = jax.ShapeDtypeStruct((n_ctx, 2, heads, dk, dv), F32)
    else:
        s_spec = pl.BlockSpec((1, heads, dk, dv), lambda w, rb, f, l, cb, *_: (cb[w], 0, 0, 0))
        s_shape = jax.ShapeDtypeStruct((n_ctx, heads, dk, dv), F32)
    return pl.pallas_call(
        functools.partial(_ret_kernel, heads=heads, dk=dk, dv=dv, reverse=reverse, combine=combine),
        grid_spec=pltpu.PrefetchScalarGridSpec(
            num_scalar_prefetch=len(tabs), grid=(m // step_rows,),
            in_specs=in_specs,
            out_specs=[pl.BlockSpec((step_rows, vw), lambda w, rb, *_: (rb[w], 0)), s_spec],
            scratch_shapes=[pltpu.VMEM((heads, dk, dv), F32)]),
        out_shape=[jax.ShapeDtypeStruct((m, vw), BF16), s_shape],
        compiler_params=_cparams("arbitrary"),
        name="ret_bwd" if reverse else "ret_fwd",
    )(*tabs, *args)


def _rope_tables(t_lat, d):
    quarter = d // 4
    rows = t_lat // GRID_W
    row = jnp.repeat(jnp.arange(rows, dtype=F32), GRID_W)
    col = jnp.tile(jnp.arange(GRID_W, dtype=F32), rows)
    inv = ROPE_BASE ** (-jnp.arange(quarter, dtype=F32) / quarter)
    ar = row[:, None] * inv[None, :]
    ac = col[:, None] * inv[None, :]
    cos_t = jnp.concatenate([jnp.cos(ar), jnp.cos(ar), jnp.cos(ac), jnp.cos(ac)], axis=1)
    sin_t = jnp.concatenate([-jnp.sin(ar), jnp.sin(ar), -jnp.sin(ac), jnp.sin(ac)], axis=1)
    pad_rows = CHUNK * CHUNKS_PER_STEP
    cos_t = jnp.concatenate([cos_t, jnp.ones((pad_rows, d), F32)], axis=0)
    sin_t = jnp.concatenate([sin_t, jnp.zeros((pad_rows, d), F32)], axis=0)
    return cos_t, sin_t


def _outproj_kernel(a1_ref, a2_ref, b1_ref, b2_ref, xc_ref, xl_ref, mod_ref, o_ref, *, gate_idx, n_ctx_tiles):
    acc = _dot(a1_ref[...], b1_ref[...]) + _dot(a2_ref[...], b2_ref[...])
    upd = mod_ref[0, gate_idx:gate_idx + 1, :] * acc
    i = pl.program_id(0)

    @pl.when(i < n_ctx_tiles)
    def _ctx():
        o_ref[...] = xc_ref[...] + upd

    @pl.when(i >= n_ctx_tiles)
    def _lat():
        o_ref[...] = xl_ref[...] + upd


def _outproj(a1, a2, w_out, xc, xl, mod, t_lat, gate_idx):
    m, k1 = a1.shape
    k2 = a2.shape[1]
    n = w_out.shape[1]
    n_ctx_rows = xc.shape[0]
    assert k1 == k2
    tm = _divisor(math.gcd(n_ctx_rows, t_lat), 1024)
    tn = _divisor(n, 512, LANES)
    nct = n_ctx_rows // tm
    mrow = functools.partial(_mod_row, tm=tm, n_ctx_rows=n_ctx_rows, t_lat=t_lat)
    return pl.pallas_call(
        functools.partial(_outproj_kernel, gate_idx=gate_idx, n_ctx_tiles=nct),
        grid=(m // tm, n // tn),
        in_specs=[pl.BlockSpec((tm, k1), lambda i, j: (i, 0)),
                  pl.BlockSpec((tm, k2), lambda i, j: (i, 0)),
                  pl.BlockSpec((k1, tn), lambda i, j: (0, j)),
                  pl.BlockSpec((k2, tn), lambda i, j: (1, j)),
                  pl.BlockSpec((tm, tn), lambda i, j: (jnp.minimum(i, nct - 1), jnp.where(i < nct, j, 0))),
                  pl.BlockSpec((tm, tn), lambda i, j: (jnp.maximum(i - nct, 0), jnp.where(i >= nct, j, 0))),
                  pl.BlockSpec((1, mod.shape[1], tn), lambda i, j: (mrow(i), 0, j))],
        out_specs=pl.BlockSpec((tm, tn), lambda i, j: (i, j)),
        out_shape=jax.ShapeDtypeStruct((m, n), F32),
        compiler_params=_cparams("arbitrary", "arbitrary"),
        name="out_proj",
    )(a1, a2, w_out, w_out, xc, xl, mod)


def _router_kernel(x_ref, w_ref, mod_ref, wr_ref, br_ref, h_ref, eid_ref, wt_ref,
                   *, shift_idx, scale_idx, n_groups, per_group):
    x = x_ref[...]
    y = x * lax.rsqrt(jnp.mean(x * x, axis=-1, keepdims=True) + EPS) * w_ref[...]
    y = y * (1.0 + mod_ref[0, scale_idx:scale_idx + 1, :]) + mod_ref[0, shift_idx:shift_idx + 1, :]
    _store_token_major(h_ref, y)
    y_hi, y_lo = _split_bf16(y)
    w_hi, w_lo = _split_bf16(wr_ref[...])
    logits = _dot(y_hi, w_hi) + _dot(y_lo, w_hi) + _dot(y_hi, w_lo) + br_ref[...]

    n_exp = n_groups * per_group
    lane = lax.broadcasted_iota(I32, logits.shape, 1)
    gmask = lane < n_groups
    gl = jnp.where(gmask, logits, NEG_BIG)
    gmax = jnp.max(gl, axis=1, keepdims=True)
    gsum = jnp.sum(jnp.where(gmask, jnp.exp(gl - gmax), 0.0), axis=1, keepdims=True)
    g_w = 1.0 / gsum
    g_idx = jnp.min(jnp.where(gl == gmax, lane, LANES), axis=1, keepdims=True)

    in_group = jnp.logical_and(lane >= n_groups + g_idx * per_group,
                               lane < n_groups + (g_idx + 1) * per_group)
    in_group = jnp.logical_and(in_group, lane < n_groups + n_exp)
    el = jnp.where(in_group, logits, NEG_BIG)
    m1 = jnp.max(el, axis=1, keepdims=True)
    i1 = jnp.min(jnp.where(el == m1, lane, LANES), axis=1, keepdims=True)
    el2 = jnp.where(lane == i1, NEG_BIG, el)
    m2 = jnp.max(el2, axis=1, keepdims=True)
    i2 = jnp.min(jnp.where(el2 == m2, lane, LANES), axis=1, keepdims=True)
    e2 = jnp.exp(m2 - m1)
    p1 = 1.0 / (1.0 + e2)
    p2 = e2 * p1
    eid_ref[...] = jnp.where(lane == 0, i1 - n_groups, jnp.where(lane == 1, i2 - n_groups, 0))
    wt_ref[...] = jnp.where(lane == 0, g_w * p1, jnp.where(lane == 1, g_w * p2, 0.0))


def _router(x, w, mod, w_r, b_r, n_ctx_rows, t_lat, shift_idx, scale_idx, n_groups, per_group):
    m, d = x.shape
    tm = _divisor(math.gcd(n_ctx_rows, t_lat), 256)
    mrow = functools.partial(_mod_row, tm=tm, n_ctx_rows=n_ctx_rows, t_lat=t_lat)
    return pl.pallas_call(
        functools.partial(_router_kernel, shift_idx=shift_idx, scale_idx=scale_idx,
                          n_groups=n_groups, per_group=per_group),
        grid=(m // tm,),
        in_specs=[pl.BlockSpec((tm, d), lambda i: (i, 0)),
                  pl.BlockSpec((1, d), lambda i: (0, 0)),
                  pl.BlockSpec((1, mod.shape[1], d), lambda i: (mrow(i), 0, 0)),
                  pl.BlockSpec((d, LANES), lambda i: (0, 0)),
                  pl.BlockSpec((1, LANES), lambda i: (0, 0))],
        out_specs=[pl.BlockSpec((tm, SUBLANES, d // SUBLANES), lambda i: (i, 0, 0)),
                   pl.BlockSpec((tm, LANES), lambda i: (i, 0)),
                   pl.BlockSpec((tm, LANES), lambda i: (i, 0))],
        out_shape=[jax.ShapeDtypeStruct((m, SUBLANES, d // SUBLANES), F32),
                   jax.ShapeDtypeStruct((m, LANES), I32),
                   jax.ShapeDtypeStruct((m, LANES), F32)],
        compiler_params=_cparams("parallel"),
        name="router",
    )(x, w.reshape(1, d), mod, w_r, b_r)


GATHER_UNROLL = 8


def _row_copy(src_hbm, dst_buf, sem, src_row, slot, dst_row):
    return pltpu.make_async_copy(src_hbm.at[src_row], dst_buf.at[slot, dst_row], sem.at[slot])


def _start_row_gather(idx_ref, src_hbm, buf, sem, slot, n_idx):
    def body(j, carry):
        _row_copy(src_hbm, buf, sem, idx_ref[0, 0, j], slot, j).start()
        return carry

    lax.fori_loop(0, n_idx, body, 0, unroll=GATHER_UNROLL)


def _wait_row_gather(src_hbm, buf, sem, slot, n_idx):
    def body(j, carry):
        _row_copy(src_hbm, buf, sem, 0, slot, j).wait()
        return carry

    lax.fori_loop(0, n_idx, body, 0, unroll=GATHER_UNROLL)


def _ring_step(idx_ref, nxt_ref, src_hbm, buf, sem, n_idx, consume):
    i = pl.program_id(0)

    @pl.when(i == 0)
    def _prime():
        _start_row_gather(idx_ref, src_hbm, buf, sem, 0, n_idx)

    for slot in range(2):
        @pl.when(i % 2 == slot)
        def _work(slot=slot):
            @pl.when(i + 1 < pl.num_programs(0))
            def _prefetch():
                _start_row_gather(nxt_ref, src_hbm, buf, sem, 1 - slot, n_idx)

            _wait_row_gather(src_hbm, buf, sem, slot, n_idx)
            consume(slot)


def _gather_rows_kernel(idx_ref, nxt_ref, src_hbm, o_ref, buf, sem, flat, *, rows):
    def consume(slot):
        w = buf.shape[-1]
        for s in range(SUBLANES):
            flat[:, s * w:(s + 1) * w] = buf[slot, :, s, :]
        o_ref[...] = flat[...].astype(o_ref.dtype)

    _ring_step(idx_ref, nxt_ref, src_hbm, buf, sem, rows, consume)


def _gather_rows(src, slot_src, rows, out_dtype):
    n_slots = slot_src.shape[0]
    nblk = n_slots // rows
    d = src.shape[1] * src.shape[2]
    idx3 = slot_src.reshape(nblk, 1, rows)
    return pl.pallas_call(
        functools.partial(_gather_rows_kernel, rows=rows),
        grid=(nblk,),
        in_specs=[pl.BlockSpec((1, 1, rows), lambda i: (i, 0, 0), memory_space=pltpu.SMEM),
                  pl.BlockSpec((1, 1, rows), lambda i: (jnp.minimum(i + 1, nblk - 1), 0, 0),
                               memory_space=pltpu.SMEM),
                  pl.BlockSpec(memory_space=pl.ANY)],
        out_specs=pl.BlockSpec((rows, d), lambda i: (i, 0)),
        scratch_shapes=[pltpu.VMEM((2, rows) + src.shape[1:], src.dtype), pltpu.SemaphoreType.DMA((2,)),
                        pltpu.VMEM((rows, d), src.dtype)],
        out_shape=jax.ShapeDtypeStruct((n_slots, d), out_dtype),
        compiler_params=_cparams("arbitrary"),
        name="moe_dispatch",
    )(idx3, idx3, src)


def _moe_up_kernel(xblk, eid, wjt, oblk, ojt, fst, valid, x_ref, wg_ref, wu_ref, h_ref, wg_s, wu_s):
    w = pl.program_id(0)

    @pl.when(valid[w] == 1)
    def _go():
        @pl.when(fst[w] == 1)
        def _cast():
            wg_s[...] = wg_ref[0].astype(BF16)
            wu_s[...] = wu_ref[0].astype(BF16)

        x = x_ref[...]
        a = _dot(x, wg_s[...])
        b = _dot(x, wu_s[...])
        h_ref[...] = (a * _sigmoid(a) * b).astype(h_ref.dtype)

    @pl.when(valid[w] == 0)
    def _pad():
        h_ref[...] = jnp.zeros(h_ref.shape, h_ref.dtype)


def _moe_up(xs, w_gate, w_up, tabs, n_work, tn):
    n_slots, d = xs.shape
    f = w_gate.shape[2]
    return pl.pallas_call(
        _moe_up_kernel,
        grid_spec=pltpu.PrefetchScalarGridSpec(
            num_scalar_prefetch=7, grid=(n_work,),
            in_specs=[pl.BlockSpec((MOE_TILE, d), lambda w, xb, e, wj, *_: (xb[w], 0)),
                      pl.BlockSpec((1, d, tn), lambda w, xb, e, wj, *_: (e[w], 0, wj[w])),
                      pl.BlockSpec((1, d, tn), lambda w, xb, e, wj, *_: (e[w], 0, wj[w]))],
            out_specs=pl.BlockSpec((MOE_TILE, tn), lambda w, xb, e, wj, ob, oj, *_: (ob[w], oj[w])),
            scratch_shapes=[pltpu.VMEM((d, tn), BF16), pltpu.VMEM((d, tn), BF16)]),
        out_shape=jax.ShapeDtypeStruct((n_slots, f), BF16),
        compiler_params=_cparams("arbitrary"),
        name="moe_up",
    )(*tabs, xs, w_gate, w_up)


def _moe_down_kernel(eid, fst, valid, h_ref, wd_ref, y_ref, wd_s):
    w = pl.program_id(0)

    @pl.when(valid[w] == 1)
    def _go():
        @pl.when(fst[w] == 1)
        def _cast():
            wd_s[...] = wd_ref[0].astype(BF16)

        _store_token_major(y_ref, _dot(h_ref[...], wd_s[...]))

    @pl.when(valid[w] == 0)
    def _pad():
        y_ref[...] = jnp.zeros(y_ref.shape, y_ref.dtype)


def _moe_down(hs, w_down, tabs):
    n_slots, f = hs.shape
    d = w_down.shape[2]
    nblk = n_slots // MOE_TILE
    return pl.pallas_call(
        _moe_down_kernel,
        grid_spec=pltpu.PrefetchScalarGridSpec(
            num_scalar_prefetch=3, grid=(nblk,),
            in_specs=[pl.BlockSpec((MOE_TILE, f), lambda w, eid, *_: (w, 0)),
                      pl.BlockSpec((1, f, d), lambda w, eid, *_: (eid[w], 0, 0))],
            out_specs=pl.BlockSpec((MOE_TILE, SUBLANES, d // SUBLANES), lambda w, eid, *_: (w, 0, 0)),
            scratch_shapes=[pltpu.VMEM((f, d), BF16)]),
        out_shape=jax.ShapeDtypeStruct((n_slots, SUBLANES, d // SUBLANES), F32),
        compiler_params=_cparams("arbitrary"),
        name="moe_down",
    )(*tabs, hs, w_down)


def _final_kernel(dst_ref, nxt_ref, x_ref, wt_ref, mod_ref, fw_ref, ys_hbm, o_ref, buf, sem, *, rows, gate_idx):
    def consume(slot):
        wt = wt_ref[...]
        moe = wt[:, 0:1] * _load_token_major(buf, slot, 0, rows)
        for kk in range(1, TOP_K):
            moe = moe + wt[:, kk:kk + 1] * _load_token_major(buf, slot, kk * rows, rows)
        x = x_ref[...] + mod_ref[0, gate_idx:gate_idx + 1, :] * moe
        o_ref[...] = x * lax.rsqrt(jnp.mean(x * x, axis=-1, keepdims=True) + EPS) * fw_ref[...]

    _ring_step(dst_ref, nxt_ref, ys_hbm, buf, sem, TOP_K * rows, consume)


def _final(x1, wt, dest, ys, mod, final_w, row0, n_rows, n_ctx_rows, t_lat, gate_idx):
    d = x1.shape[1]
    tm = _divisor(math.gcd(n_ctx_rows, t_lat), 128)
    t0 = row0 // tm
    nt = n_rows // tm
    mrow = functools.partial(_mod_row, tm=tm, n_ctx_rows=n_ctx_rows, t_lat=t_lat)
    dest3 = dest.reshape(-1, tm, TOP_K).transpose(0, 2, 1).reshape(-1, 1, TOP_K * tm)
    return pl.pallas_call(
        functools.partial(_final_kernel, rows=tm, gate_idx=gate_idx),
        grid=(nt,),
        in_specs=[pl.BlockSpec((1, 1, TOP_K * tm), lambda i: (t0 + i, 0, 0), memory_space=pltpu.SMEM),
                  pl.BlockSpec((1, 1, TOP_K * tm), lambda i: (t0 + jnp.minimum(i + 1, nt - 1), 0, 0),
                               memory_space=pltpu.SMEM),
                  pl.BlockSpec((tm, d), lambda i: (t0 + i, 0)),
                  pl.BlockSpec((tm, LANES), lambda i: (t0 + i, 0)),
                  pl.BlockSpec((1, mod.shape[1], d), lambda i: (mrow(t0 + i), 0, 0)),
                  pl.BlockSpec((1, d), lambda i: (0, 0)),
                  pl.BlockSpec(memory_space=pl.ANY)],
        out_specs=pl.BlockSpec((tm, d), lambda i: (i, 0)),
        out_shape=jax.ShapeDtypeStruct((n_rows, d), F32),
        scratch_shapes=[pltpu.VMEM((2, TOP_K * tm) + ys.shape[1:], F32), pltpu.SemaphoreType.DMA((2,))],
        compiler_params=_cparams("arbitrary"),
        name="moe_combine_final",
    )(dest3, dest3, x1, wt, mod, final_w.reshape(1, d), ys)


def _count_le(sorted_ends, idx):
    return jnp.sum((sorted_ends[None, :] <= idx[:, None]).astype(I32), axis=1)


def _moe_plan(eid, n_exp, n_jt):
    t = eid.shape[0]
    n_assign = t * TOP_K
    eflat = eid.reshape(n_assign)
    onehot = (eflat[:, None] == jnp.arange(n_exp, dtype=I32)[None, :]).astype(I32)
    csum = jnp.cumsum(onehot, axis=0)
    counts = csum[-1]
    nb = (counts + MOE_TILE - 1) // MOE_TILE
    blk_end = jnp.cumsum(nb)
    blk_start = blk_end - nb
    n_blocks = blk_end[-1]
    pad_start = blk_start * MOE_TILE
    dest = jnp.sum(onehot * (csum - 1 + pad_start[None, :]), axis=1)

    nblk_max = n_assign // MOE_TILE + n_exp
    order = jnp.argsort(eflat, stable=True).astype(I32)
    order = jnp.concatenate([order, jnp.zeros((MOE_TILE,), I32)])
    start = jnp.cumsum(counts) - counts
    b_idx = jnp.arange(nblk_max, dtype=I32)
    blk_e = jnp.minimum(_count_le(blk_end, b_idx), n_exp - 1)
    blk_off = (b_idx - blk_start[blk_e]) * MOE_TILE
    win = jnp.clip(start[blk_e] + blk_off, 0, n_assign)
    rows = jax.vmap(lambda st: lax.dynamic_slice(order, (st,), (MOE_TILE,)))(win)
    in_blk = jnp.arange(MOE_TILE, dtype=I32)[None, :]
    ok = jnp.logical_and((b_idx < n_blocks)[:, None], blk_off[:, None] + in_blk < counts[blk_e][:, None])
    slot_tok = jnp.where(ok, rows // TOP_K, 0).astype(I32).reshape(nblk_max * MOE_TILE)

    d_valid = (b_idx < n_blocks).astype(I32)
    last_e = blk_e[jnp.maximum(n_blocks - 1, 0)]
    d_eid = jnp.where(d_valid == 1, blk_e, last_e).astype(I32)
    d_first = jnp.logical_and(d_valid == 1, b_idx == blk_start[d_eid]).astype(I32)

    n_work = n_jt * nblk_max
    w_idx = jnp.arange(n_work, dtype=I32)
    per_e = n_jt * nb
    w_end = jnp.cumsum(per_e)
    w_valid = w_idx < w_end[-1]
    we = jnp.minimum(_count_le(w_end, w_idx), n_exp - 1)
    r = w_idx - (w_end[we] - per_e[we])
    nbe = jnp.maximum(nb[we], 1)
    u_jt = r // nbe
    u_t = r - u_jt * nbe
    u_blk = blk_start[we] + u_t
    last_w = jnp.maximum(w_end[-1] - 1, 0)
    spare = w_idx - w_end[-1]
    u_xblk = jnp.where(w_valid, u_blk, u_blk[last_w]).astype(I32)
    u_eid = jnp.where(w_valid, we, we[last_w]).astype(I32)
    u_wjt = jnp.where(w_valid, u_jt, u_jt[last_w]).astype(I32)
    u_oblk = jnp.where(w_valid, u_blk, n_blocks + spare // n_jt).astype(I32)
    u_ojt = jnp.where(w_valid, u_jt, spare % n_jt).astype(I32)
    u_first = jnp.logical_and(w_valid, u_t == 0).astype(I32)
    up_tabs = (u_xblk, u_eid, u_wjt, u_oblk, u_ojt, u_first, w_valid.astype(I32))
    down_tabs = (d_eid, d_first, d_valid)
    return dest.astype(I32), slot_tok, up_tabs, down_tabs, n_work


def kernel(x_prompt, x_sample, state_mlstm_C, state_mlstm_n, state_mlstm_m, state_ret_S, c, c_ctx,
           norm1_w, norm2_w, w_ada, b_ada, w_in, b_mgates, mlstm_norm_w, ret_norm_w, ret_log_decay, w_out,
           w_router_group, b_router_group, w_router_expert, b_router_expert, w_exp_gate, w_exp_up,
           w_exp_down, final_norm_w):
    n_ctx, t_ctx, d_model = x_prompt.shape
    n_lat, t_lat, _ = x_sample.shape
    depth = norm1_w.shape[0]
    assert depth == 1, "single-layer trunk"
    m_heads, m_dk, m_dv = state_mlstm_C.shape[3:]
    r_heads, r_dk, r_dv = state_ret_S.shape[3:]
    n_groups = w_router_group.shape[2]
    n_exp = w_router_expert.shape[2]
    per_group = n_exp // n_groups
    d_exp = w_exp_gate.shape[3]
    n_gates = N_GATE_ROWS * m_heads
    n_ctx_rows = n_ctx * t_ctx
    step_rows = CHUNK * CHUNKS_PER_STEP
    assert t_ctx % step_rows == 0 and t_lat % step_rows == 0 and n_gates <= LANES
    assert n_groups + n_exp <= LANES

    cvec = jnp.concatenate([c_ctx[None, :], c, jnp.zeros((SUBLANES - 1 - n_lat, d_model), F32)], axis=0)
    mod = _ada_mod(cvec, w_ada[0], b_ada[0]).reshape(SUBLANES, 6, d_model)

    xc = x_prompt.reshape(n_ctx_rows, d_model)
    xl = x_sample.reshape(n_lat * t_lat, d_model)
    h = _norm_mod(xc, xl, norm1_w[0], mod, t_lat, 0, 1, BF16)

    g0 = 2 * m_heads * m_dk + 2 * m_heads * m_dv
    w_main = jnp.concatenate([w_in[0][:, :g0], w_in[0][:, g0 + n_gates:]], axis=1).astype(BF16)
    w_gate = jnp.pad(w_in[0][:, g0:g0 + n_gates], ((0, 0), (0, LANES - n_gates))).astype(BF16)
    z, zg = _inproj(h, w_main, w_gate)

    gbias = jnp.pad(b_mgates[0].reshape(1, n_gates), ((0, 0), (0, LANES - n_gates)))
    st_m = jnp.broadcast_to(state_mlstm_m[:, 0][..., None], (n_lat, 2, m_heads, LANES))
    mdims = (n_ctx, t_ctx, n_lat, t_lat, m_heads, m_dk, m_dv)
    fwd_m = _mlstm_scan(z, zg, gbias, state_mlstm_C[:, 0], state_mlstm_n[:, 0], st_m, mdims, False)
    mix_m, new_c, new_n, new_m = _mlstm_scan(z, zg, gbias, state_mlstm_C[:, 0], state_mlstm_n[:, 0], st_m, mdims,
                                             True, fwd=fwd_m, norm_w=mlstm_norm_w[0])

    cos_tab, sin_tab = _rope_tables(t_lat, r_dk)
    ld = jnp.pad(ret_log_decay[0], ((0, SUBLANES - 2), (0, LANES - r_heads)))
    rdims = (n_ctx, t_ctx, n_lat, t_lat, r_heads, r_dk, r_dv)
    fwd_r = _ret_scan(z, cos_tab, sin_tab, ld, state_ret_S[:, 0], rdims, g0, False)
    mix_r, new_s = _ret_scan(z, cos_tab, sin_tab, ld, state_ret_S[:, 0], rdims, g0, True,
                             fwd=fwd_r, norm_w=ret_norm_w[0])

    x1 = _outproj(mix_m, mix_r, w_out[0].astype(BF16), xc, xl, mod, t_lat, 2)

    w_r = jnp.pad(jnp.concatenate([w_router_group[0], w_router_expert[0]], axis=1),
                  ((0, 0), (0, LANES - n_groups - n_exp)))
    b_r = jnp.pad(jnp.concatenate([b_router_group[0], b_router_expert[0]])[None, :],
                  ((0, 0), (0, LANES - n_groups - n_exp)))
    h2, eid, wt = _router(x1, norm2_w[0], mod, w_r, b_r, n_ctx_rows, t_lat, 3, 4, n_groups, per_group)

    tn_up = _divisor(d_exp, 512, LANES)
    dest, slot_tok, up_tabs, down_tabs, n_work = _moe_plan(eid[:, :TOP_K], n_exp, d_exp // tn_up)
    xs = _gather_rows(h2, slot_tok, MOE_TILE, BF16)
    hs = _moe_up(xs, w_exp_gate[0], w_exp_up[0], up_tabs, n_work, tn_up)
    ys = _moe_down(hs, w_exp_down[0], down_tabs)

    y_ctx = _final(x1, wt, dest, ys, mod, final_norm_w, 0, n_ctx_rows, n_ctx_rows, t_lat, 5)
    y_lat = _final(x1, wt, dest, ys, mod, final_norm_w, n_ctx_rows, n_lat * t_lat, n_ctx_rows, t_lat, 5)

    return (y_ctx.reshape(n_ctx, t_ctx, d_model), y_lat.reshape(n_lat, t_lat, d_model),
            new_c[:, None], new_n[:, None], new_m[:, None, :, :, 0], new_s[:, None])
```

```python
import functools
import math

import numpy as np
import jax
import jax.numpy as jnp
from jax import lax
from jax.experimental import pallas as pl
from jax.experimental.pallas import tpu as pltpu

F32 = jnp.float32
BF16 = jnp.bfloat16
I32 = jnp.int32
U32 = jnp.uint32

CHUNK = 128
GRID_W = 64
ROPE_BASE = 10000.0
EPS = 1e-6
TOP_K = 2
N_GATE_ROWS = 4

LANES = 128
SUBLANES = 8
VMEM_LIMIT_BYTES = 56 * 1024 * 1024
NEG_BIG = -1e30
MOE_TILE = 256


def _divisor(n, pref, mult=SUBLANES):
    if n <= pref:
        return n
    d = (pref // mult) * mult
    while d > mult and n % d:
        d -= mult
    assert n % d == 0, (n, pref, mult)
    return d


def _cparams(*sem):
    return pltpu.CompilerParams(dimension_semantics=sem, vmem_limit_bytes=VMEM_LIMIT_BYTES)


def _dot(a, b):
    return jnp.dot(a, b, preferred_element_type=F32)


def _dot_nt(a, b):
    return lax.dot_general(a, b, (((1,), (1,)), ((), ())), preferred_element_type=F32)


def _dot_tn(a, b):
    return lax.dot_general(a, b, (((0,), (0,)), ((), ())), preferred_element_type=F32)


def _split_bf16(x):
    hi = x.astype(BF16)
    lo = (x - hi.astype(F32)).astype(BF16)
    return hi, lo


def _sigmoid(x):
    return 1.0 / (1.0 + jnp.exp(-x))


def _log_sigmoid(x):
    return jnp.minimum(x, 0.0) - jnp.log(1.0 + jnp.exp(-jnp.abs(x)))


def _mod_row(i, tm, n_ctx_rows, t_lat):
    r0 = i * tm
    return jnp.where(r0 < n_ctx_rows, 0, 1 + (r0 - n_ctx_rows) // t_lat)


def _store_token_major(ref, y):
    w = ref.shape[-1]
    for s in range(SUBLANES):
        ref[:, s, :] = y[:, s * w:(s + 1) * w]


def _load_token_major(ref, slot, r0, rows):
    return jnp.concatenate([ref[slot, r0:r0 + rows, s, :] for s in range(SUBLANES)], axis=1)


def _pack_bf16_pair(y):
    n = y.shape[1] // 2
    hi = lax.bitcast_convert_type(y[:, :n].astype(BF16).astype(F32), U32)
    lo = lax.bitcast_convert_type(y[:, n:].astype(BF16).astype(F32), U32)
    return hi | (lo >> 16)


def _unpack_bf16_pair(p):
    hi = lax.bitcast_convert_type(p & jnp.uint32(0xFFFF0000), F32)
    lo = lax.bitcast_convert_type(p << 16, F32)
    return hi, lo


def _ada_kernel(c_ref, w_ref, b_ref, o_ref):
    c = c_ref[...]
    s = c * _sigmoid(c)
    s_hi, s_lo = _split_bf16(s)
    w_hi, w_lo = _split_bf16(w_ref[...])
    o_ref[...] = _dot(s_hi, w_hi) + _dot(s_lo, w_hi) + _dot(s_hi, w_lo) + b_ref[...]


def _ada_mod(cvec, w_ada, b_ada):
    r, d = cvec.shape
    n = w_ada.shape[1]
    tn = _divisor(n, 256, LANES)
    return pl.pallas_call(
        _ada_kernel,
        grid=(n // tn,),
        in_specs=[pl.BlockSpec((r, d), lambda j: (0, 0)),
                  pl.BlockSpec((d, tn), lambda j: (0, j)),
                  pl.BlockSpec((1, tn), lambda j: (0, j))],
        out_specs=pl.BlockSpec((r, tn), lambda j: (0, j)),
        out_shape=jax.ShapeDtypeStruct((r, n), F32),
        compiler_params=_cparams("parallel"),
        name="ada_mod",
    )(cvec, w_ada, b_ada.reshape(1, n))


def _norm_mod_kernel(xc_ref, xl_ref, w_ref, mod_ref, o_ref, *, shift_idx, scale_idx, n_ctx_tiles):
    def body(x_ref):
        x = x_ref[...]
        y = x * lax.rsqrt(jnp.mean(x * x, axis=-1, keepdims=True) + EPS) * w_ref[...]
        y = y * (1.0 + mod_ref[0, scale_idx:scale_idx + 1, :]) + mod_ref[0, shift_idx:shift_idx + 1, :]
        o_ref[...] = y.astype(o_ref.dtype)

    i = pl.program_id(0)
    pl.when(i < n_ctx_tiles)(lambda: body(xc_ref))
    pl.when(i >= n_ctx_tiles)(lambda: body(xl_ref))


def _norm_mod(xc, xl, w, mod, t_lat, shift_idx, scale_idx, out_dtype):
    n_ctx_rows, d = xc.shape
    m = n_ctx_rows + xl.shape[0]
    tm = _divisor(math.gcd(n_ctx_rows, t_lat), 256)
    nct = n_ctx_rows // tm
    mrow = functools.partial(_mod_row, tm=tm, n_ctx_rows=n_ctx_rows, t_lat=t_lat)
    return pl.pallas_call(
        functools.partial(_norm_mod_kernel, shift_idx=shift_idx, scale_idx=scale_idx, n_ctx_tiles=nct),
        grid=(m // tm,),
        in_specs=[pl.BlockSpec((tm, d), lambda i: (jnp.minimum(i, nct - 1), 0)),
                  pl.BlockSpec((tm, d), lambda i: (jnp.maximum(i - nct, 0), 0)),
                  pl.BlockSpec((1, d), lambda i: (0, 0)),
                  pl.BlockSpec((1, mod.shape[1], d), lambda i: (mrow(i), 0, 0))],
        out_specs=pl.BlockSpec((tm, d), lambda i: (i, 0)),
        out_shape=jax.ShapeDtypeStruct((m, d), out_dtype),
        compiler_params=_cparams("arbitrary"),
        name="norm_mod",
    )(xc, xl, w.reshape(1, d), mod)


def _inproj_kernel(a_ref, b_ref, wg_ref, z_ref, zg_ref):
    a = a_ref[...]
    z_ref[...] = _dot(a, b_ref[...]).astype(z_ref.dtype)

    @pl.when(pl.program_id(1) == 0)
    def _gates():
        zg_ref[...] = _dot(a, wg_ref[...])


def _inproj(h, w_main, w_gate):
    m, k = h.shape
    n = w_main.shape[1]
    tm = _divisor(m, 1024)
    tn = _divisor(n, 1024, LANES)
    return pl.pallas_call(
        _inproj_kernel,
        grid=(m // tm, n // tn),
        in_specs=[pl.BlockSpec((tm, k), lambda i, j: (i, 0)),
                  pl.BlockSpec((k, tn), lambda i, j: (0, j)),
                  pl.BlockSpec((k, LANES), lambda i, j: (0, 0))],
        out_specs=[pl.BlockSpec((tm, tn), lambda i, j: (i, j)),
                   pl.BlockSpec((tm, LANES), lambda i, j: (i, 0))],
        out_shape=[jax.ShapeDtypeStruct((m, n), BF16),
                   jax.ShapeDtypeStruct((m, LANES), F32)],
        compiler_params=_cparams("parallel", "arbitrary"),
        name="in_proj",
    )(h, w_main, w_gate)


CHUNKS_PER_STEP = 2


def _scan_schedule(n_ctx, t_ctx, n_lat, t_lat, reverse):
    rowblk, first, last, ctxb, latb, islat, posblk = [], [], [], [], [], [], []
    step_rows = CHUNK * CHUNKS_PER_STEP
    ns_ctx, ns_lat = t_ctx // step_rows, t_lat // step_rows
    for b in range(n_ctx):
        order = range(ns_ctx - 1, -1, -1) if reverse else range(ns_ctx)
        for pos, c in enumerate(order):
            rowblk.append(b * ns_ctx + c)
            first.append(int(pos == 0))
            last.append(int(pos == ns_ctx - 1))
            ctxb.append(b)
            latb.append(0)
            islat.append(0)
            posblk.append(ns_lat)
    base = n_ctx * ns_ctx
    for b in range(n_lat):
        order = range(ns_lat - 1, -1, -1) if reverse else range(ns_lat)
        for pos, c in enumerate(order):
            rowblk.append(base + b * ns_lat + c)
            first.append(int(pos == 0))
            last.append(int(pos == ns_lat - 1))
            ctxb.append(n_ctx - 1)
            latb.append(b)
            islat.append(1)
            posblk.append(c)
    tabs = (rowblk, first, last, ctxb, latb, islat, posblk)
    return tuple(jnp.asarray(np.asarray(t, np.int32)) for t in tabs)


def _chunk_rows(ci, reverse):
    cc = (CHUNKS_PER_STEP - 1 - ci) if reverse else ci
    return pl.ds(pl.multiple_of(cc * CHUNK, CHUNK), CHUNK)


def _mlstm_kernel(rowblk, first, last, ctxb, latb, islat, posblk,
                  q_ref, k_ref, v_ref, g_ref, gb_ref, c0_ref, n0_ref, m0_ref, *rest,
                  heads, dk, dv, reverse, combine):
    if combine:
        (hf_ref, o_ref, nw_ref, cf_ref, nf_ref, mf_ref,
         out_ref, cout_ref, nout_ref, mout_ref, c_s, n_s, m_s) = rest
    else:
        out_ref, cout_ref, nout_ref, mout_ref, c_s, n_s, m_s = rest
    w = pl.program_id(0)
    lat = islat[w] == 1

    @pl.when(first[w] == 1)
    def _init():
        c_s[...] = jnp.where(lat, c0_ref[0, 0], 0.0)
        n_s[...] = jnp.where(lat, n0_ref[0, 0], 0.0)
        m_s[...] = jnp.where(lat, m0_ref[0, 0], 0.0)

    L = CHUNK
    ti = lax.broadcasted_iota(I32, (L, L), 0)
    si = lax.broadcasted_iota(I32, (L, L), 1)
    mask = (si >= ti) if reverse else (si <= ti)
    mask_t = (ti >= si) if reverse else (ti <= si)
    scale = dk ** -0.5
    gate_row = 2 if reverse else 0
    hs_ = range(heads)

    def chunk(ci, carry):
        rows = _chunk_rows(ci, reverse)
        g = g_ref[rows, :] + gb_ref[...]
        gt = g.T
        lg = _log_sigmoid(g)
        lgt = _log_sigmoid(gt)

        st = []
        for h in hs_:
            ci_, cf_ = gate_row * heads + h, (gate_row + 1) * heads + h
            i_col, i_row = g[:, ci_:ci_ + 1], gt[ci_:ci_ + 1, :]
            lf_col, lf_row = lg[:, cf_:cf_ + 1], lgt[cf_:cf_ + 1, :]
            b_col = jnp.sum(jnp.where(mask, lf_row, 0.0), axis=1, keepdims=True)
            b_row = jnp.sum(jnp.where(mask_t, lf_col, 0.0), axis=0, keepdims=True)
            bl = jnp.sum(lf_row, axis=1, keepdims=True)
            m_old = m_s[h:h + 1, 0:1]
            dm = jnp.where(mask, b_col - b_row + i_row, NEG_BIG)
            inter = b_col + m_old
            mt = jnp.maximum(jnp.max(dm, axis=1, keepdims=True), inter)
            p = jnp.exp(dm - mt)
            wi = jnp.exp(inter - mt)
            g_col = bl - b_col + i_col
            g_row = bl - b_row + i_row
            m_new = jnp.maximum(bl + m_old, jnp.max(g_row, axis=1, keepdims=True))
            a_prev = jnp.exp(bl + m_old - m_new)
            a_col = jnp.exp(g_col - m_new) * scale
            st.append((p, wi, mt, m_new, a_prev, a_col))

        qbs = [q_ref[rows, h * dk:(h + 1) * dk] for h in hs_]
        kbs = [k_ref[rows, h * dk:(h + 1) * dk] for h in hs_]
        vbs = [v_ref[rows, h * dv:(h + 1) * dv] for h in hs_]
        ss = [_dot_nt(qbs[h], kbs[h]) * (scale * st[h][0]) for h in hs_]
        qcs = [_dot(qbs[h], c_s[h].astype(BF16)) for h in hs_]
        n_olds = [n_s[h:h + 1, :] for h in hs_]
        for h in hs_:
            p, wi, mt, m_new, a_prev, a_col = st[h]
            s_ = ss[h]
            num = _dot(s_.astype(BF16), vbs[h]) + wi * qcs[h]
            qn = jnp.sum(qbs[h].astype(F32) * n_olds[h], axis=1, keepdims=True)
            den = jnp.sum(s_, axis=1, keepdims=True) + wi * qn
            hval = num / jnp.maximum(jnp.abs(den), jnp.exp(-mt))
            if combine:
                hs = hval + hf_ref[rows, h * dv:(h + 1) * dv].astype(F32)
                y = hs * lax.rsqrt(jnp.mean(hs * hs, axis=-1, keepdims=True) + EPS)
                gate = o_ref[rows, h * dv:(h + 1) * dv].astype(F32)
                y = y * nw_ref[:, h * dv:(h + 1) * dv] * _sigmoid(gate)
                out_ref[rows, h * dv:(h + 1) * dv] = y.astype(out_ref.dtype)
            else:
                out_ref[rows, h * dv:(h + 1) * dv] = hval.astype(out_ref.dtype)
        for h in hs_:
            p, wi, mt, m_new, a_prev, a_col = st[h]
            ak = a_col * kbs[h].astype(F32)
            c_s[h] = a_prev * c_s[h] + _dot_tn(ak.astype(BF16), vbs[h])
            n_s[h:h + 1, :] = a_prev * n_olds[h] + jnp.sum(ak, axis=0, keepdims=True)
            m_s[h:h + 1, :] = jnp.broadcast_to(m_new, (1, LANES))
        return carry

    lax.fori_loop(0, CHUNKS_PER_STEP, chunk, 0)

    @pl.when(jnp.logical_and(last[w] == 1, jnp.logical_not(lat)))
    def _emit():
        if combine:
            cout_ref[0, 0] = cf_ref[0]
            nout_ref[0, 0] = nf_ref[0]
            mout_ref[0, 0] = mf_ref[0]
            cout_ref[0, 1] = c_s[...]
            nout_ref[0, 1] = n_s[...]
            mout_ref[0, 1] = m_s[...]
        else:
            cout_ref[0] = c_s[...]
            nout_ref[0] = n_s[...]
            mout_ref[0] = m_s[...]


def _mlstm_scan(z, zg, gbias, st_c, st_n, st_m, dims, reverse, fwd=None, norm_w=None):
    (n_ctx, t_ctx, n_lat, t_lat, heads, dk, dv) = dims
    combine = fwd is not None
    m = z.shape[0]
    d = 1 if reverse else 0
    step_rows = CHUNK * CHUNKS_PER_STEP
    tabs = _scan_schedule(n_ctx, t_ctx, n_lat, t_lat, reverse)
    qw, vw = heads * dk, heads * dv
    assert (2 * qw) % vw == 0
    v_blk = (2 * qw) // vw
    in_specs = [
        pl.BlockSpec((step_rows, qw), lambda w, rb, *_: (rb[w], 0)),
        pl.BlockSpec((step_rows, qw), lambda w, rb, *_: (rb[w], 1)),
        pl.BlockSpec((step_rows, vw), lambda w, rb, *_: (rb[w], v_blk)),
        pl.BlockSpec((step_rows, LANES), lambda w, rb, *_: (rb[w], 0)),
        pl.BlockSpec((1, LANES), lambda w, *_: (0, 0)),
        pl.BlockSpec((1, 1, heads, dk, dv), lambda w, rb, f, l, cb, lb, *_: (lb[w], d, 0, 0, 0)),
        pl.BlockSpec((1, 1, heads, dk), lambda w, rb, f, l, cb, lb, *_: (lb[w], d, 0, 0)),
        pl.BlockSpec((1, 1, heads, LANES), lambda w, rb, f, l, cb, lb, *_: (lb[w], d, 0, 0)),
    ]
    args = [z, z, z, zg, gbias, st_c, st_n, st_m]
    if combine:
        hf, cf, nf, mf = fwd
        in_specs += [
            pl.BlockSpec((step_rows, vw), lambda w, rb, *_: (rb[w], 0)),
            pl.BlockSpec((step_rows, vw), lambda w, rb, *_: (rb[w], v_blk + 1)),
            pl.BlockSpec((1, vw), lambda w, *_: (0, 0)),
            pl.BlockSpec((1, heads, dk, dv), lambda w, rb, f, l, cb, *_: (cb[w], 0, 0, 0)),
            pl.BlockSpec((1, heads, dk), lambda w, rb, f, l, cb, *_: (cb[w], 0, 0)),
            pl.BlockSpec((1, heads, LANES), lambda w, rb, f, l, cb, *_: (cb[w], 0, 0)),
        ]
        args += [hf, z, norm_w.reshape(1, vw), cf, nf, mf]
        nd = (2,)
        st_idx = lambda w, rb, f, l, cb, *_: (cb[w], 0, 0, 0)
        c_idx = lambda w, rb, f, l, cb, *_: (cb[w], 0, 0, 0, 0)
    else:
        nd = ()
        st_idx = lambda w, rb, f, l, cb, *_: (cb[w], 0, 0)
        c_idx = lambda w, rb, f, l, cb, *_: (cb[w], 0, 0, 0)
    out_specs = [
        pl.BlockSpec((step_rows, vw), lambda w, rb, *_: (rb[w], 0)),
        pl.BlockSpec((1,) + nd + (heads, dk, dv), c_idx),
        pl.BlockSpec((1,) + nd + (heads, dk), st_idx),
        pl.BlockSpec((1,) + nd + (heads, LANES), st_idx),
    ]
    out_shape = [
        jax.ShapeDtypeStruct((m, vw), BF16),
        jax.ShapeDtypeStruct((n_ctx,) + nd + (heads, dk, dv), F32),
        jax.ShapeDtypeStruct((n_ctx,) + nd + (heads, dk), F32),
        jax.ShapeDtypeStruct((n_ctx,) + nd + (heads, LANES), F32),
    ]
    return pl.pallas_call(
        functools.partial(_mlstm_kernel, heads=heads, dk=dk, dv=dv, reverse=reverse, combine=combine),
        grid_spec=pltpu.PrefetchScalarGridSpec(
            num_scalar_prefetch=len(tabs), grid=(m // step_rows,),
            in_specs=in_specs, out_specs=out_specs,
            scratch_shapes=[pltpu.VMEM((heads, dk, dv), F32),
                            pltpu.VMEM((heads, dk), F32),
                            pltpu.VMEM((heads, LANES), F32)]),
        out_shape=out_shape,
        compiler_params=_cparams("arbitrary"),
        name="mlstm_bwd" if reverse else "mlstm_fwd",
    )(*tabs, *args)


def _rope_partner(x, d):
    q, hlf = d // 4, d // 2
    return jnp.concatenate([x[:, q:hlf], x[:, :q], x[:, hlf + q:], x[:, hlf:hlf + q]], axis=1)


def _ret_kernel(rowblk, first, last, ctxb, latb, islat, posblk,
                q_ref, k_ref, v_ref, cos_ref, sin_ref, ld_ref, s0_ref, *rest,
                heads, dk, dv, reverse, combine):
    if combine:
        of_ref, g_ref, nw_ref, sf_ref, out_ref, sout_ref, s_s = rest
    else:
        out_ref, sout_ref, s_s = rest
    w = pl.program_id(0)
    lat = islat[w] == 1

    @pl.when(first[w] == 1)
    def _init():
        s_s[...] = jnp.where(lat, s0_ref[0, 0], 0.0)

    L = CHUNK
    ti = lax.broadcasted_iota(I32, (L, L), 0).astype(F32)
    si = lax.broadcasted_iota(I32, (L, L), 1).astype(F32)
    diff = (si - ti) if reverse else (ti - si)
    idx = lax.broadcasted_iota(I32, (L, 1), 0).astype(F32)
    q_pow = (L - idx) if reverse else (idx + 1.0)
    k_pow = idx if reverse else (L - 1.0 - idx)
    scale = dk ** -0.5
    drow = 1 if reverse else 0
    log_decay = -jnp.exp(ld_ref[drow:drow + 1, :])
    hs_ = range(heads)

    def chunk(ci, carry):
        rows = _chunk_rows(ci, reverse)
        cos_t, sin_t = cos_ref[rows, :], sin_ref[rows, :]
        qbs, kbs, kds, vbs = [], [], [], []
        for h in hs_:
            ld = log_decay[:, h:h + 1]
            q = q_ref[rows, h * dk:(h + 1) * dk].astype(F32)
            k = k_ref[rows, h * dk:(h + 1) * dk].astype(F32)
            q = q * cos_t + _rope_partner(q, dk) * sin_t
            k = k * cos_t + _rope_partner(k, dk) * sin_t
            qbs.append(q.astype(BF16))
            kbs.append(k.astype(BF16))
            kds.append((k * (scale * jnp.exp(k_pow * ld))).astype(BF16))
            vbs.append(v_ref[rows, h * dv:(h + 1) * dv])
        scores = [_dot_nt(qbs[h], kbs[h]) for h in hs_]
        inter = [_dot(qbs[h], s_s[h].astype(BF16)) for h in hs_]
        for h in hs_:
            ld = log_decay[:, h:h + 1]
            intra = jnp.where(diff >= 0.0, jnp.exp(jnp.maximum(diff, 0.0) * ld), 0.0)
            a = scores[h] * (scale * intra)
            o = _dot(a.astype(BF16), vbs[h]) + inter[h] * jnp.exp(q_pow * ld)
            if combine:
                hs = o + of_ref[rows, h * dv:(h + 1) * dv].astype(F32)
                y = hs * lax.rsqrt(jnp.mean(hs * hs, axis=-1, keepdims=True) + EPS)
                gate = g_ref[rows, h * dv:(h + 1) * dv].astype(F32)
                y = y * nw_ref[:, h * dv:(h + 1) * dv] * (gate * _sigmoid(gate))
                out_ref[rows, h * dv:(h + 1) * dv] = y.astype(out_ref.dtype)
            else:
                out_ref[rows, h * dv:(h + 1) * dv] = o.astype(out_ref.dtype)
        for h in hs_:
            ld = log_decay[:, h:h + 1]
            s_s[h] = jnp.exp(float(L) * ld) * s_s[h] + _dot_tn(kds[h], vbs[h])
        return carry

    lax.fori_loop(0, CHUNKS_PER_STEP, chunk, 0)

    @pl.when(jnp.logical_and(last[w] == 1, jnp.logical_not(lat)))
    def _emit():
        if combine:
            sout_ref[0, 0] = sf_ref[0]
            sout_ref[0, 1] = s_s[...]
        else:
            sout_ref[0] = s_s[...]


def _ret_scan(z, cos_tab, sin_tab, ld, st_s, dims, col0, reverse, fwd=None, norm_w=None):
    (n_ctx, t_ctx, n_lat, t_lat, heads, dk, dv) = dims
    combine = fwd is not None
    m = z.shape[0]
    d = 1 if reverse else 0
    step_rows = CHUNK * CHUNKS_PER_STEP
    tabs = _scan_schedule(n_ctx, t_ctx, n_lat, t_lat, reverse)
    qw, vw = heads * dk, heads * dv
    assert qw == vw and col0 % qw == 0
    b0 = col0 // qw
    in_specs = [
        pl.BlockSpec((step_rows, qw), lambda w, rb, *_: (rb[w], b0)),
        pl.BlockSpec((step_rows, qw), lambda w, rb, *_: (rb[w], b0 + 1)),
        pl.BlockSpec((step_rows, vw), lambda w, rb, *_: (rb[w], b0 + 2)),
        pl.BlockSpec((step_rows, dk), lambda w, rb, f, l, cb, lb, il, pb: (pb[w], 0)),
        pl.BlockSpec((step_rows, dk), lambda w, rb, f, l, cb, lb, il, pb: (pb[w], 0)),
        pl.BlockSpec((SUBLANES, LANES), lambda w, *_: (0, 0)),
        pl.BlockSpec((1, 1, heads, dk, dv), lambda w, rb, f, l, cb, lb, *_: (lb[w], d, 0, 0, 0)),
    ]
    args = [z, z, z, cos_tab, sin_tab, ld, st_s]
    if combine:
        of, sf = fwd
        in_specs += [
            pl.BlockSpec((step_rows, vw), lambda w, rb, *_: (rb[w], 0)),
            pl.BlockSpec((step_rows, vw), lambda w, rb, *_: (rb[w], b0 + 3)),
            pl.BlockSpec((1, vw), lambda w, *_: (0, 0)),
            pl.BlockSpec((1, heads, dk, dv), lambda w, rb, f, l, cb, *_: (cb[w], 0, 0, 0)),
        ]
        args += [of, z, norm_w.reshape(1, vw), sf]
        s_spec = pl.BlockSpec((1, 2, heads, dk, dv), lambda w, rb, f, l, cb, *_: (cb[w], 0, 0, 0, 0))
        s_shape = jax.ShapeDtypeStruct((n_ctx, 2, heads, dk, dv), F32)
    else:
        s_spec = pl.BlockSpec((1, heads, dk, dv), lambda w, rb, f, l, cb, *_: (cb[w], 0, 0, 0))
        s_shape = jax.ShapeDtypeStruct((n_ctx, heads, dk, dv), F32)
    return pl.pallas_call(
        functools.partial(_ret_kernel, heads=heads, dk=dk, dv=dv, reverse=reverse, combine=combine),
        grid_spec=pltpu.PrefetchScalarGridSpec(
            num_scalar_prefetch=len(tabs), grid=(m // step_rows,),
            in_specs=in_specs,
            out_specs=[pl.BlockSpec((step_rows, vw), lambda w, rb, *_: (rb[w], 0)), s_spec],
            scratch_shapes=[pltpu.VMEM((heads, dk, dv), F32)]),
        out_shape=[jax.ShapeDtypeStruct((m, vw), BF16), s_shape],
        compiler_params=_cparams("arbitrary"),
        name="ret_bwd" if reverse else "ret_fwd",
    )(*tabs, *args)


def _rope_tables(t_lat, d):
    quarter = d // 4
    rows = t_lat // GRID_W
    row = jnp.repeat(jnp.arange(rows, dtype=F32), GRID_W)
    col = jnp.tile(jnp.arange(GRID_W, dtype=F32), rows)
    inv = ROPE_BASE ** (-jnp.arange(quarter, dtype=F32) / quarter)
    ar = row[:, None] * inv[None, :]
    ac = col[:, None] * inv[None, :]
    cos_t = jnp.concatenate([jnp.cos(ar), jnp.cos(ar), jnp.cos(ac), jnp.cos(ac)], axis=1)
    sin_t = jnp.concatenate([-jnp.sin(ar), jnp.sin(ar), -jnp.sin(ac), jnp.sin(ac)], axis=1)
    pad_rows = CHUNK * CHUNKS_PER_STEP
    cos_t = jnp.concatenate([cos_t, jnp.ones((pad_rows, d), F32)], axis=0)
    sin_t = jnp.concatenate([sin_t, jnp.zeros((pad_rows, d), F32)], axis=0)
    return cos_t, sin_t


def _outproj_kernel(a1_ref, a2_ref, b1_ref, b2_ref, xc_ref, xl_ref, mod_ref, o_ref, *, gate_idx, n_ctx_tiles):
    acc = _dot(a1_ref[...], b1_ref[...]) + _dot(a2_ref[...], b2_ref[...])
    upd = mod_ref[0, gate_idx:gate_idx + 1, :] * acc
    i = pl.program_id(0)

    @pl.when(i < n_ctx_tiles)
    def _ctx():
        o_ref[...] = xc_ref[...] + upd

    @pl.when(i >= n_ctx_tiles)
    def _lat():
        o_ref[...] = xl_ref[...] + upd


def _outproj(a1, a2, w_out, xc, xl, mod, t_lat, gate_idx):
    m, k1 = a1.shape
    k2 = a2.shape[1]
    n = w_out.shape[1]
    n_ctx_rows = xc.shape[0]
    assert k1 == k2
    tm = _divisor(math.gcd(n_ctx_rows, t_lat), 1024)
    tn = _divisor(n, 512, LANES)
    nct = n_ctx_rows // tm
    mrow = functools.partial(_mod_row, tm=tm, n_ctx_rows=n_ctx_rows, t_lat=t_lat)
    return pl.pallas_call(
        functools.partial(_outproj_kernel, gate_idx=gate_idx, n_ctx_tiles=nct),
        grid=(m // tm, n // tn),
        in_specs=[pl.BlockSpec((tm, k1), lambda i, j: (i, 0)),
                  pl.BlockSpec((tm, k2), lambda i, j: (i, 0)),
                  pl.BlockSpec((k1, tn), lambda i, j: (0, j)),
                  pl.BlockSpec((k2, tn), lambda i, j: (1, j)),
                  pl.BlockSpec((tm, tn), lambda i, j: (jnp.minimum(i, nct - 1), jnp.where(i < nct, j, 0))),
                  pl.BlockSpec((tm, tn), lambda i, j: (jnp.maximum(i - nct, 0), jnp.where(i >= nct, j, 0))),
                  pl.BlockSpec((1, mod.shape[1], tn), lambda i, j: (mrow(i), 0, j))],
        out_specs=pl.BlockSpec((tm, tn), lambda i, j: (i, j)),
        out_shape=jax.ShapeDtypeStruct((m, n), F32),
        compiler_params=_cparams("arbitrary", "arbitrary"),
        name="out_proj",
    )(a1, a2, w_out, w_out, xc, xl, mod)


def _router_kernel(x_ref, w_ref, mod_ref, wr_ref, br_ref, h_ref, eid_ref, wt_ref,
                   *, shift_idx, scale_idx, n_groups, per_group):
    x = x_ref[...]
    y = x * lax.rsqrt(jnp.mean(x * x, axis=-1, keepdims=True) + EPS) * w_ref[...]
    y = y * (1.0 + mod_ref[0, scale_idx:scale_idx + 1, :]) + mod_ref[0, shift_idx:shift_idx + 1, :]
    _store_token_major(h_ref, _pack_bf16_pair(y))
    y_hi, y_lo = _split_bf16(y)
    w_hi, w_lo = _split_bf16(wr_ref[...])
    logits = _dot(y_hi, w_hi) + _dot(y_lo, w_hi) + _dot(y_hi, w_lo) + br_ref[...]

    n_exp = n_groups * per_group
    lane = lax.broadcasted_iota(I32, logits.shape, 1)
    gmask = lane < n_groups
    gl = jnp.where(gmask, logits, NEG_BIG)
    gmax = jnp.max(gl, axis=1, keepdims=True)
    gsum = jnp.sum(jnp.where(gmask, jnp.exp(gl - gmax), 0.0), axis=1, keepdims=True)
    g_w = 1.0 / gsum
    g_idx = jnp.min(jnp.where(gl == gmax, lane, LANES), axis=1, keepdims=True)

    in_group = jnp.logical_and(lane >= n_groups + g_idx * per_group,
                               lane < n_groups + (g_idx + 1) * per_group)
    in_group = jnp.logical_and(in_group, lane < n_groups + n_exp)
    el = jnp.where(in_group, logits, NEG_BIG)
    m1 = jnp.max(el, axis=1, keepdims=True)
    i1 = jnp.min(jnp.where(el == m1, lane, LANES), axis=1, keepdims=True)
    el2 = jnp.where(lane == i1, NEG_BIG, el)
    m2 = jnp.max(el2, axis=1, keepdims=True)
    i2 = jnp.min(jnp.where(el2 == m2, lane, LANES), axis=1, keepdims=True)
    e2 = jnp.exp(m2 - m1)
    p1 = 1.0 / (1.0 + e2)
    p2 = e2 * p1
    eid_ref[...] = jnp.where(lane == 0, i1 - n_groups, jnp.where(lane == 1, i2 - n_groups, 0))
    wt_ref[...] = jnp.where(lane == 0, g_w * p1, jnp.where(lane == 1, g_w * p2, 0.0))


def _router(x, w, mod, w_r, b_r, n_ctx_rows, t_lat, shift_idx, scale_idx, n_groups, per_group):
    m, d = x.shape
    tm = _divisor(math.gcd(n_ctx_rows, t_lat), 256)
    mrow = functools.partial(_mod_row, tm=tm, n_ctx_rows=n_ctx_rows, t_lat=t_lat)
    return pl.pallas_call(
        functools.partial(_router_kernel, shift_idx=shift_idx, scale_idx=scale_idx,
                          n_groups=n_groups, per_group=per_group),
        grid=(m // tm,),
        in_specs=[pl.BlockSpec((tm, d), lambda i: (i, 0)),
                  pl.BlockSpec((1, d), lambda i: (0, 0)),
                  pl.BlockSpec((1, mod.shape[1], d), lambda i: (mrow(i), 0, 0)),
                  pl.BlockSpec((d, LANES), lambda i: (0, 0)),
                  pl.BlockSpec((1, LANES), lambda i: (0, 0))],
        out_specs=[pl.BlockSpec((tm, SUBLANES, d // (2 * SUBLANES)), lambda i: (i, 0, 0)),
                   pl.BlockSpec((tm, LANES), lambda i: (i, 0)),
                   pl.BlockSpec((tm, LANES), lambda i: (i, 0))],
        out_shape=[jax.ShapeDtypeStruct((m, SUBLANES, d // (2 * SUBLANES)), U32),
                   jax.ShapeDtypeStruct((m, LANES), I32),
                   jax.ShapeDtypeStruct((m, LANES), F32)],
        compiler_params=_cparams("parallel"),
        name="router",
    )(x, w.reshape(1, d), mod, w_r, b_r)


GATHER_UNROLL = 8


def _row_copy(src_hbm, dst_buf, sem, src_row, slot, dst_row):
    return pltpu.make_async_copy(src_hbm.at[src_row], dst_buf.at[slot, dst_row], sem.at[slot])


def _start_row_gather(tok_fn, src_hbm, buf, sem, slot, n_idx):
    def body(jj, carry):
        for u in range(GATHER_UNROLL):
            j = jj * GATHER_UNROLL + u
            _row_copy(src_hbm, buf, sem, tok_fn(j), slot, j).start(priority=u % 2)
        return carry

    lax.fori_loop(0, n_idx // GATHER_UNROLL, body, 0)


def _wait_row_gather(src_hbm, buf, sem, slot, n_idx):
    def body(j, carry):
        _row_copy(src_hbm, buf, sem, 0, slot, j).wait()
        return carry

    lax.fori_loop(0, n_idx, body, 0, unroll=GATHER_UNROLL)


def _ring_step(cur_fn, nxt_fn, src_hbm, buf, sem, n_idx, consume):
    i = pl.program_id(0)

    @pl.when(i == 0)
    def _prime():
        _start_row_gather(cur_fn, src_hbm, buf, sem, 0, n_idx)

    for slot in range(2):
        @pl.when(i % 2 == slot)
        def _work(slot=slot):
            @pl.when(i + 1 < pl.num_programs(0))
            def _prefetch():
                _start_row_gather(nxt_fn, src_hbm, buf, sem, 1 - slot, n_idx)

            _wait_row_gather(src_hbm, buf, sem, slot, n_idx)
            consume(slot)


def _gather_rows_kernel(wblk, woff, cnt, a_ref, b_ref, na_ref, nb_ref, src_hbm, o_ref, buf, sem, flat, *, rows):
    def window(a, b, blk):
        off, c = woff[blk], cnt[blk]

        def tok(j):
            p = off + j
            v = jnp.where(p < rows, a[0, 0, jnp.minimum(p, rows - 1)], b[0, 0, jnp.maximum(p - rows, 0)])
            return jnp.where(j < c, v, 0)
        return tok

    def consume(slot):
        w = buf.shape[-1]
        for s in range(SUBLANES):
            flat[:, s * w:(s + 1) * w] = buf[slot, :, s, :]
        hi, lo = _unpack_bf16_pair(flat[...])
        n = hi.shape[1]
        o_ref[:, :n] = hi.astype(o_ref.dtype)
        o_ref[:, n:] = lo.astype(o_ref.dtype)

    i = pl.program_id(0)
    nxt = jnp.minimum(i + 1, pl.num_programs(0) - 1)
    _ring_step(window(a_ref, b_ref, i), window(na_ref, nb_ref, nxt), src_hbm, buf, sem, rows, consume)


def _gather_rows(src, sorted_tok, tabs, rows, out_dtype):
    nblk = tabs[0].shape[0]
    dw = src.shape[1] * src.shape[2]
    d = 2 * dw
    smem = functools.partial(pl.BlockSpec, (1, 1, rows), memory_space=pltpu.SMEM)
    last = nblk - 1
    return pl.pallas_call(
        functools.partial(_gather_rows_kernel, rows=rows),
        grid_spec=pltpu.PrefetchScalarGridSpec(
            num_scalar_prefetch=3, grid=(nblk,),
            in_specs=[smem(lambda i, wb, *_: (wb[i], 0, 0)),
                      smem(lambda i, wb, *_: (wb[i] + 1, 0, 0)),
                      smem(lambda i, wb, *_: (wb[jnp.minimum(i + 1, last)], 0, 0)),
                      smem(lambda i, wb, *_: (wb[jnp.minimum(i + 1, last)] + 1, 0, 0)),
                      pl.BlockSpec(memory_space=pl.ANY)],
            out_specs=pl.BlockSpec((rows, d), lambda i, *_: (i, 0)),
            scratch_shapes=[pltpu.VMEM((2, rows) + src.shape[1:], src.dtype), pltpu.SemaphoreType.DMA((2,)),
                            pltpu.VMEM((rows, dw), src.dtype)]),
        out_shape=jax.ShapeDtypeStruct((nblk * rows, d), out_dtype),
        compiler_params=_cparams("arbitrary"),
        name="moe_dispatch",
    )(*tabs, sorted_tok, sorted_tok, sorted_tok, sorted_tok, src)


def _moe_up_kernel(xblk, eid, wjt, oblk, ojt, fst, valid, x_ref, wg_ref, wu_ref, h_ref, wg_s, wu_s):
    w = pl.program_id(0)

    @pl.when(valid[w] == 1)
    def _go():
        @pl.when(fst[w] == 1)
        def _cast():
            wg_s[...] = wg_ref[0].astype(BF16)
            wu_s[...] = wu_ref[0].astype(BF16)

        x = x_ref[...]
        a = _dot(x, wg_s[...])
        b = _dot(x, wu_s[...])
        h_ref[...] = (a * _sigmoid(a) * b).astype(h_ref.dtype)

    @pl.when(valid[w] == 0)
    def _pad():
        h_ref[...] = jnp.zeros(h_ref.shape, h_ref.dtype)


def _moe_up(xs, w_gate, w_up, tabs, n_work, tn):
    n_slots, d = xs.shape
    f = w_gate.shape[2]
    return pl.pallas_call(
        _moe_up_kernel,
        grid_spec=pltpu.PrefetchScalarGridSpec(
            num_scalar_prefetch=7, grid=(n_work,),
            in_specs=[pl.BlockSpec((MOE_TILE, d), lambda w, xb, e, wj, *_: (xb[w], 0)),
                      pl.BlockSpec((1, d, tn), lambda w, xb, e, wj, *_: (e[w], 0, wj[w])),
                      pl.BlockSpec((1, d, tn), lambda w, xb, e, wj, *_: (e[w], 0, wj[w]))],
            out_specs=pl.BlockSpec((MOE_TILE, tn), lambda w, xb, e, wj, ob, oj, *_: (ob[w], oj[w])),
            scratch_shapes=[pltpu.VMEM((d, tn), BF16), pltpu.VMEM((d, tn), BF16)]),
        out_shape=jax.ShapeDtypeStruct((n_slots, f), BF16),
        compiler_params=_cparams("arbitrary"),
        name="moe_up",
    )(*tabs, xs, w_gate, w_up)


def _moe_down_kernel(eid, fst, valid, h_ref, wd_ref, y_ref, wd_s):
    w = pl.program_id(0)

    @pl.when(valid[w] == 1)
    def _go():
        @pl.when(fst[w] == 1)
        def _cast():
            wd_s[...] = wd_ref[0].astype(BF16)

        _store_token_major(y_ref, _pack_bf16_pair(_dot(h_ref[...], wd_s[...])))

    @pl.when(valid[w] == 0)
    def _pad():
        y_ref[...] = jnp.zeros(y_ref.shape, y_ref.dtype)


def _moe_down(hs, w_down, tabs):
    n_slots, f = hs.shape
    d = w_down.shape[2]
    nblk = n_slots // MOE_TILE
    return pl.pallas_call(
        _moe_down_kernel,
        grid_spec=pltpu.PrefetchScalarGridSpec(
            num_scalar_prefetch=3, grid=(nblk,),
            in_specs=[pl.BlockSpec((MOE_TILE, f), lambda w, eid, *_: (w, 0)),
                      pl.BlockSpec((1, f, d), lambda w, eid, *_: (eid[w], 0, 0))],
            out_specs=pl.BlockSpec((MOE_TILE, SUBLANES, d // (2 * SUBLANES)), lambda w, eid, *_: (w, 0, 0)),
            scratch_shapes=[pltpu.VMEM((f, d), BF16)]),
        out_shape=jax.ShapeDtypeStruct((n_slots, SUBLANES, d // (2 * SUBLANES)), U32),
        compiler_params=_cparams("arbitrary"),
        name="moe_down",
    )(*tabs, hs, w_down)


def _final_kernel(dst_ref, nxt_ref, x_ref, wt_ref, mod_ref, fw_ref, ys_hbm, o_ref, buf, sem, *, rows, gate_idx):
    def consume(slot):
        wt = wt_ref[...]
        moe = None
        for kk in range(TOP_K):
            hi, lo = _unpack_bf16_pair(_load_token_major(buf, slot, kk * rows, rows))
            term = wt[:, kk:kk + 1] * jnp.concatenate([hi, lo], axis=1)
            moe = term if moe is None else moe + term
        x = x_ref[...] + mod_ref[0, gate_idx:gate_idx + 1, :] * moe
        o_ref[...] = x * lax.rsqrt(jnp.mean(x * x, axis=-1, keepdims=True) + EPS) * fw_ref[...]

    _ring_step(lambda j: dst_ref[0, 0, j], lambda j: nxt_ref[0, 0, j], ys_hbm, buf, sem, TOP_K * rows, consume)


def _final(x1, wt, dest, ys, mod, final_w, row0, n_rows, n_ctx_rows, t_lat, gate_idx):
    d = x1.shape[1]
    tm = _divisor(math.gcd(n_ctx_rows, t_lat), 128)
    t0 = row0 // tm
    nt = n_rows // tm
    mrow = functools.partial(_mod_row, tm=tm, n_ctx_rows=n_ctx_rows, t_lat=t_lat)
    dest3 = dest.reshape(-1, tm, TOP_K).transpose(0, 2, 1).reshape(-1, 1, TOP_K * tm)
    return pl.pallas_call(
        functools.partial(_final_kernel, rows=tm, gate_idx=gate_idx),
        grid=(nt,),
        in_specs=[pl.BlockSpec((1, 1, TOP_K * tm), lambda i: (t0 + i, 0, 0), memory_space=pltpu.SMEM),
                  pl.BlockSpec((1, 1, TOP_K * tm), lambda i: (t0 + jnp.minimum(i + 1, nt - 1), 0, 0),
                               memory_space=pltpu.SMEM),
                  pl.BlockSpec((tm, d), lambda i: (t0 + i, 0)),
                  pl.BlockSpec((tm, LANES), lambda i: (t0 + i, 0)),
                  pl.BlockSpec((1, mod.shape[1], d), lambda i: (mrow(t0 + i), 0, 0)),
                  pl.BlockSpec((1, d), lambda i: (0, 0)),
                  pl.BlockSpec(memory_space=pl.ANY)],
        out_specs=pl.BlockSpec((tm, d), lambda i: (i, 0)),
        out_shape=jax.ShapeDtypeStruct((n_rows, d), F32),
        scratch_shapes=[pltpu.VMEM((2, TOP_K * tm) + ys.shape[1:], ys.dtype), pltpu.SemaphoreType.DMA((2,))],
        compiler_params=_cparams("arbitrary"),
        name="moe_combine_final",
    )(dest3, dest3, x1, wt, mod, final_w.reshape(1, d), ys)


def _count_le(sorted_ends, idx):
    return jnp.sum((sorted_ends[None, :] <= idx[:, None]).astype(I32), axis=1)


def _moe_plan(eid, n_exp, n_jt):
    t = eid.shape[0]
    n_assign = t * TOP_K
    eflat = eid.reshape(n_assign)
    onehot = (eflat[:, None] == jnp.arange(n_exp, dtype=I32)[None, :]).astype(I32)
    csum = jnp.cumsum(onehot, axis=0)
    counts = csum[-1]
    nb = (counts + MOE_TILE - 1) // MOE_TILE
    blk_end = jnp.cumsum(nb)
    blk_start = blk_end - nb
    n_blocks = blk_end[-1]
    pad_start = blk_start * MOE_TILE
    dest = jnp.sum(onehot * (csum - 1 + pad_start[None, :]), axis=1)

    assert n_assign % MOE_TILE == 0
    nblk_max = n_assign // MOE_TILE + n_exp
    n_list_blk = n_assign // MOE_TILE + 2
    sorted_tok = (jnp.argsort(eflat, stable=True) // TOP_K).astype(I32)
    sorted_tok = jnp.concatenate([sorted_tok, jnp.zeros((n_list_blk * MOE_TILE - n_assign,), I32)])
    sorted_tok = sorted_tok.reshape(n_list_blk, 1, MOE_TILE)
    start = jnp.cumsum(counts) - counts
    b_idx = jnp.arange(nblk_max, dtype=I32)
    blk_e = jnp.minimum(_count_le(blk_end, b_idx), n_exp - 1)
    blk_off = (b_idx - blk_start[blk_e]) * MOE_TILE
    win = jnp.clip(start[blk_e] + blk_off, 0, n_assign)
    g_cnt = jnp.where(b_idx < n_blocks, jnp.clip(counts[blk_e] - blk_off, 0, MOE_TILE), 0)
    gather_tabs = ((win // MOE_TILE).astype(I32), (win % MOE_TILE).astype(I32), g_cnt.astype(I32))

    d_valid = (b_idx < n_blocks).astype(I32)
    last_e = blk_e[jnp.maximum(n_blocks - 1, 0)]
    d_eid = jnp.where(d_valid == 1, blk_e, last_e).astype(I32)
    d_first = jnp.logical_and(d_valid == 1, b_idx == blk_start[d_eid]).astype(I32)

    n_work = n_jt * nblk_max
    w_idx = jnp.arange(n_work, dtype=I32)
    per_e = n_jt * nb
    w_end = jnp.cumsum(per_e)
    w_valid = w_idx < w_end[-1]
    we = jnp.minimum(_count_le(w_end, w_idx), n_exp - 1)
    r = w_idx - (w_end[we] - per_e[we])
    nbe = jnp.maximum(nb[we], 1)
    u_jt = r // nbe
    u_t = r - u_jt * nbe
    u_blk = blk_start[we] + u_t
    last_w = jnp.maximum(w_end[-1] - 1, 0)
    spare = w_idx - w_end[-1]
    u_xblk = jnp.where(w_valid, u_blk, u_blk[last_w]).astype(I32)
    u_eid = jnp.where(w_valid, we, we[last_w]).astype(I32)
    u_wjt = jnp.where(w_valid, u_jt, u_jt[last_w]).astype(I32)
    u_oblk = jnp.where(w_valid, u_blk, n_blocks + spare // n_jt).astype(I32)
    u_ojt = jnp.where(w_valid, u_jt, spare % n_jt).astype(I32)
    u_first = jnp.logical_and(w_valid, u_t == 0).astype(I32)
    up_tabs = (u_xblk, u_eid, u_wjt, u_oblk, u_ojt, u_first, w_valid.astype(I32))
    down_tabs = (d_eid, d_first, d_valid)
    return dest.astype(I32), sorted_tok, gather_tabs, up_tabs, down_tabs, n_work


def kernel(x_prompt, x_sample, state_mlstm_C, state_mlstm_n, state_mlstm_m, state_ret_S, c, c_ctx,
           norm1_w, norm2_w, w_ada, b_ada, w_in, b_mgates, mlstm_norm_w, ret_norm_w, ret_log_decay, w_out,
           w_router_group, b_router_group, w_router_expert, b_router_expert, w_exp_gate, w_exp_up,
           w_exp_down, final_norm_w):
    n_ctx, t_ctx, d_model = x_prompt.shape
    n_lat, t_lat, _ = x_sample.shape
    depth = norm1_w.shape[0]
    assert depth == 1, "single-layer trunk"
    m_heads, m_dk, m_dv = state_mlstm_C.shape[3:]
    r_heads, r_dk, r_dv = state_ret_S.shape[3:]
    n_groups = w_router_group.shape[2]
    n_exp = w_router_expert.shape[2]
    per_group = n_exp // n_groups
    d_exp = w_exp_gate.shape[3]
    n_gates = N_GATE_ROWS * m_heads
    n_ctx_rows = n_ctx * t_ctx
    step_rows = CHUNK * CHUNKS_PER_STEP
    assert t_ctx % step_rows == 0 and t_lat % step_rows == 0 and n_gates <= LANES
    assert n_groups + n_exp <= LANES

    cvec = jnp.concatenate([c_ctx[None, :], c, jnp.zeros((SUBLANES - 1 - n_lat, d_model), F32)], axis=0)
    mod = _ada_mod(cvec, w_ada[0], b_ada[0]).reshape(SUBLANES, 6, d_model)

    xc = x_prompt.reshape(n_ctx_rows, d_model)
    xl = x_sample.reshape(n_lat * t_lat, d_model)
    h = _norm_mod(xc, xl, norm1_w[0], mod, t_lat, 0, 1, BF16)

    g0 = 2 * m_heads * m_dk + 2 * m_heads * m_dv
    w_main = jnp.concatenate([w_in[0][:, :g0].astype(BF16), w_in[0][:, g0 + n_gates:].astype(BF16)], axis=1)
    w_gate = jnp.pad(w_in[0][:, g0:g0 + n_gates], ((0, 0), (0, LANES - n_gates))).astype(BF16)
    z, zg = _inproj(h, w_main, w_gate)

    gbias = jnp.pad(b_mgates[0].reshape(1, n_gates), ((0, 0), (0, LANES - n_gates)))
    st_m = jnp.broadcast_to(state_mlstm_m[:, 0][..., None], (n_lat, 2, m_heads, LANES))
    mdims = (n_ctx, t_ctx, n_lat, t_lat, m_heads, m_dk, m_dv)
    fwd_m = _mlstm_scan(z, zg, gbias, state_mlstm_C[:, 0], state_mlstm_n[:, 0], st_m, mdims, False)
    mix_m, new_c, new_n, new_m = _mlstm_scan(z, zg, gbias, state_mlstm_C[:, 0], state_mlstm_n[:, 0], st_m, mdims,
                                             True, fwd=fwd_m, norm_w=mlstm_norm_w[0])

    cos_tab, sin_tab = _rope_tables(t_lat, r_dk)
    ld = jnp.pad(ret_log_decay[0], ((0, SUBLANES - 2), (0, LANES - r_heads)))
    rdims = (n_ctx, t_ctx, n_lat, t_lat, r_heads, r_dk, r_dv)
    fwd_r = _ret_scan(z, cos_tab, sin_tab, ld, state_ret_S[:, 0], rdims, g0, False)
    mix_r, new_s = _ret_scan(z, cos_tab, sin_tab, ld, state_ret_S[:, 0], rdims, g0, True,
                             fwd=fwd_r, norm_w=ret_norm_w[0])

    x1 = _outproj(mix_m, mix_r, w_out[0].astype(BF16), xc, xl, mod, t_lat, 2)

    w_r = jnp.pad(jnp.concatenate([w_router_group[0], w_router_expert[0]], axis=1),
                  ((0, 0), (0, LANES - n_groups - n_exp)))
    b_r = jnp.pad(jnp.concatenate([b_router_group[0], b_router_expert[0]])[None, :],
                  ((0, 0), (0, LANES - n_groups - n_exp)))
    h2, eid, wt = _router(x1, norm2_w[0], mod, w_r, b_r, n_ctx_rows, t_lat, 3, 4, n_groups, per_group)

    tn_up = _divisor(d_exp, 512, LANES)
    dest, sorted_tok, gather_tabs, up_tabs, down_tabs, n_work = _moe_plan(eid[:, :TOP_K], n_exp, d_exp // tn_up)
    xs = _gather_rows(h2, sorted_tok, gather_tabs, MOE_TILE, BF16)
    hs = _moe_up(xs, w_exp_gate[0], w_exp_up[0], up_tabs, n_work, tn_up)
    ys = _moe_down(hs, w_exp_down[0], down_tabs)

    y_ctx = _final(x1, wt, dest, ys, mod, final_norm_w, 0, n_ctx_rows, n_ctx_rows, t_lat, 5)
    y_lat = _final(x1, wt, dest, ys, mod, final_norm_w, n_ctx_rows, n_lat * t_lat, n_ctx_rows, t_lat, 5)

    return (y_ctx.reshape(n_ctx, t_ctx, d_model), y_lat.reshape(n_lat, t_lat, d_model),
            new_c[:, None], new_n[:, None], new_m[:, None, :, :, 0], new_s[:, None])
```

```python
import functools
import math

import numpy as np
import jax
import jax.numpy as jnp
from jax import lax
from jax.experimental import pallas as pl
from jax.experimental.pallas import tpu as pltpu

F32 = jnp.float32
BF16 = jnp.bfloat16
I32 = jnp.int32
U32 = jnp.uint32

CHUNK = 128
GRID_W = 64
ROPE_BASE = 10000.0
EPS = 1e-6
TOP_K = 2
N_GATE_ROWS = 4

LANES = 128
SUBLANES = 8
VMEM_LIMIT_BYTES = 56 * 1024 * 1024
NEG_BIG = -1e30
MOE_TILE = 256


def _divisor(n, pref, mult=SUBLANES):
    if n <= pref:
        return n
    d = (pref // mult) * mult
    while d > mult and n % d:
        d -= mult
    assert n % d == 0, (n, pref, mult)
    return d


def _cparams(*sem):
    return pltpu.CompilerParams(dimension_semantics=sem, vmem_limit_bytes=VMEM_LIMIT_BYTES)


def _dot(a, b):
    return jnp.dot(a, b, preferred_element_type=F32)


def _dot_nt(a, b):
    return lax.dot_general(a, b, (((1,), (1,)), ((), ())), preferred_element_type=F32)


def _dot_tn(a, b):
    return lax.dot_general(a, b, (((0,), (0,)), ((), ())), preferred_element_type=F32)


def _split_bf16(x):
    hi = x.astype(BF16)
    lo = (x - hi.astype(F32)).astype(BF16)
    return hi, lo


def _sigmoid(x):
    return 1.0 / (1.0 + jnp.exp(-x))


def _log_sigmoid(x):
    return jnp.minimum(x, 0.0) - jnp.log(1.0 + jnp.exp(-jnp.abs(x)))


def _mod_row(i, tm, n_ctx_rows, t_lat):
    r0 = i * tm
    return jnp.where(r0 < n_ctx_rows, 0, 1 + (r0 - n_ctx_rows) // t_lat)


def _store_token_major(ref, y):
    w = ref.shape[-1]
    for s in range(SUBLANES):
        ref[:, s, :] = y[:, s * w:(s + 1) * w]


def _load_token_major(ref, slot, r0, rows):
    return jnp.concatenate([ref[slot, r0:r0 + rows, s, :] for s in range(SUBLANES)], axis=1)


def _pack_bf16_pair(y):
    n = y.shape[1] // 2
    hi = lax.bitcast_convert_type(y[:, :n].astype(BF16).astype(F32), U32)
    lo = lax.bitcast_convert_type(y[:, n:].astype(BF16).astype(F32), U32)
    return hi | (lo >> 16)


def _unpack_bf16_pair(p):
    hi = lax.bitcast_convert_type(p & jnp.uint32(0xFFFF0000), F32)
    lo = lax.bitcast_convert_type(p << 16, F32)
    return hi, lo


def _ada_kernel(c_ref, w_ref, b_ref, o_ref):
    c = c_ref[...]
    s = c * _sigmoid(c)
    s_hi, s_lo = _split_bf16(s)
    w_hi, w_lo = _split_bf16(w_ref[...])
    o_ref[...] = _dot(s_hi, w_hi) + _dot(s_lo, w_hi) + _dot(s_hi, w_lo) + b_ref[...]


def _ada_mod(cvec, w_ada, b_ada):
    r, d = cvec.shape
    n = w_ada.shape[1]
    tn = _divisor(n, 256, LANES)
    return pl.pallas_call(
        _ada_kernel,
        grid=(n // tn,),
        in_specs=[pl.BlockSpec((r, d), lambda j: (0, 0)),
                  pl.BlockSpec((d, tn), lambda j: (0, j)),
                  pl.BlockSpec((1, tn), lambda j: (0, j))],
        out_specs=pl.BlockSpec((r, tn), lambda j: (0, j)),
        out_shape=jax.ShapeDtypeStruct((r, n), F32),
        compiler_params=_cparams("parallel"),
        name="ada_mod",
    )(cvec, w_ada, b_ada.reshape(1, n))


def _norm_mod_kernel(xc_ref, xl_ref, w_ref, mod_ref, o_ref, *, shift_idx, scale_idx, n_ctx_tiles):
    def body(x_ref):
        x = x_ref[...]
        y = x * lax.rsqrt(jnp.mean(x * x, axis=-1, keepdims=True) + EPS) * w_ref[...]
        y = y * (1.0 + mod_ref[0, scale_idx:scale_idx + 1, :]) + mod_ref[0, shift_idx:shift_idx + 1, :]
        o_ref[...] = y.astype(o_ref.dtype)

    i = pl.program_id(0)
    pl.when(i < n_ctx_tiles)(lambda: body(xc_ref))
    pl.when(i >= n_ctx_tiles)(lambda: body(xl_ref))


def _norm_mod(xc, xl, w, mod, t_lat, shift_idx, scale_idx, out_dtype):
    n_ctx_rows, d = xc.shape
    m = n_ctx_rows + xl.shape[0]
    tm = _divisor(math.gcd(n_ctx_rows, t_lat), 256)
    nct = n_ctx_rows // tm
    mrow = functools.partial(_mod_row, tm=tm, n_ctx_rows=n_ctx_rows, t_lat=t_lat)
    return pl.pallas_call(
        functools.partial(_norm_mod_kernel, shift_idx=shift_idx, scale_idx=scale_idx, n_ctx_tiles=nct),
        grid=(m // tm,),
        in_specs=[pl.BlockSpec((tm, d), lambda i: (jnp.minimum(i, nct - 1), 0)),
                  pl.BlockSpec((tm, d), lambda i: (jnp.maximum(i - nct, 0), 0)),
                  pl.BlockSpec((1, d), lambda i: (0, 0)),
                  pl.BlockSpec((1, mod.shape[1], d), lambda i: (mrow(i), 0, 0))],
        out_specs=pl.BlockSpec((tm, d), lambda i: (i, 0)),
        out_shape=jax.ShapeDtypeStruct((m, d), out_dtype),
        compiler_params=_cparams("arbitrary"),
        name="norm_mod",
    )(xc, xl, w.reshape(1, d), mod)


def _inproj_kernel(a_ref, b_ref, wg_ref, z_ref, zg_ref):
    a = a_ref[...]
    z_ref[...] = _dot(a, b_ref[...]).astype(z_ref.dtype)

    @pl.when(pl.program_id(1) == 0)
    def _gates():
        zg_ref[...] = _dot(a, wg_ref[...])


def _inproj(h, w_main, w_gate):
    m, k = h.shape
    n = w_main.shape[1]
    tm = _divisor(m, 1024)
    tn = _divisor(n, 1024, LANES)
    return pl.pallas_call(
        _inproj_kernel,
        grid=(m // tm, n // tn),
        in_specs=[pl.BlockSpec((tm, k), lambda i, j: (i, 0)),
                  pl.BlockSpec((k, tn), lambda i, j: (0, j)),
                  pl.BlockSpec((k, LANES), lambda i, j: (0, 0))],
        out_specs=[pl.BlockSpec((tm, tn), lambda i, j: (i, j)),
                   pl.BlockSpec((tm, LANES), lambda i, j: (i, 0))],
        out_shape=[jax.ShapeDtypeStruct((m, n), BF16),
                   jax.ShapeDtypeStruct((m, LANES), F32)],
        compiler_params=_cparams("parallel", "arbitrary"),
        name="in_proj",
    )(h, w_main, w_gate)


CHUNKS_PER_STEP = 2


def _scan_schedule(n_ctx, t_ctx, n_lat, t_lat, reverse):
    rowblk, first, last, ctxb, latb, islat, posblk = [], [], [], [], [], [], []
    step_rows = CHUNK * CHUNKS_PER_STEP
    ns_ctx, ns_lat = t_ctx // step_rows, t_lat // step_rows
    for b in range(n_ctx):
        order = range(ns_ctx - 1, -1, -1) if reverse else range(ns_ctx)
        for pos, c in enumerate(order):
            rowblk.append(b * ns_ctx + c)
            first.append(int(pos == 0))
            last.append(int(pos == ns_ctx - 1))
            ctxb.append(b)
            latb.append(0)
            islat.append(0)
            posblk.append(ns_lat)
    base = n_ctx * ns_ctx
    for b in range(n_lat):
        order = range(ns_lat - 1, -1, -1) if reverse else range(ns_lat)
        for pos, c in enumerate(order):
            rowblk.append(base + b * ns_lat + c)
            first.append(int(pos == 0))
            last.append(int(pos == ns_lat - 1))
            ctxb.append(n_ctx - 1)
            latb.append(b)
            islat.append(1)
            posblk.append(c)
    tabs = (rowblk, first, last, ctxb, latb, islat, posblk)
    return tuple(jnp.asarray(np.asarray(t, np.int32)) for t in tabs)


def _chunk_rows(ci, reverse):
    cc = (CHUNKS_PER_STEP - 1 - ci) if reverse else ci
    return pl.ds(pl.multiple_of(cc * CHUNK, CHUNK), CHUNK)


def _mlstm_kernel(rowblk, first, last, ctxb, latb, islat, posblk,
                  q_ref, k_ref, v_ref, g_ref, gb_ref, c0_ref, n0_ref, m0_ref, *rest,
                  heads, dk, dv, reverse, combine):
    if combine:
        (hf_ref, o_ref, nw_ref, cf_ref, nf_ref, mf_ref,
         out_ref, cout_ref, nout_ref, mout_ref, c_s, n_s, m_s) = rest
    else:
        out_ref, cout_ref, nout_ref, mout_ref, c_s, n_s, m_s = rest
    w = pl.program_id(0)
    lat = islat[w] == 1

    @pl.when(first[w] == 1)
    def _init():
        c_s[...] = jnp.where(lat, c0_ref[0, 0], 0.0)
        n_s[...] = jnp.where(lat, n0_ref[0, 0], 0.0)
        m_s[...] = jnp.where(lat, m0_ref[0, 0], 0.0)

    L = CHUNK
    ti = lax.broadcasted_iota(I32, (L, L), 0)
    si = lax.broadcasted_iota(I32, (L, L), 1)
    mask = (si >= ti) if reverse else (si <= ti)
    mask_t = (ti >= si) if reverse else (ti <= si)
    scale = dk ** -0.5
    gate_row = 2 if reverse else 0
    hs_ = range(heads)

    def chunk(ci, carry):
        rows = _chunk_rows(ci, reverse)
        g = g_ref[rows, :] + gb_ref[...]
        gt = g.T
        lg = _log_sigmoid(g)
        lgt = _log_sigmoid(gt)

        st = []
        for h in hs_:
            ci_, cf_ = gate_row * heads + h, (gate_row + 1) * heads + h
            i_col, i_row = g[:, ci_:ci_ + 1], gt[ci_:ci_ + 1, :]
            lf_col, lf_row = lg[:, cf_:cf_ + 1], lgt[cf_:cf_ + 1, :]
            b_col = jnp.sum(jnp.where(mask, lf_row, 0.0), axis=1, keepdims=True)
            b_row = jnp.sum(jnp.where(mask_t, lf_col, 0.0), axis=0, keepdims=True)
            bl = jnp.sum(lf_row, axis=1, keepdims=True)
            m_old = m_s[h:h + 1, 0:1]
            dm = jnp.where(mask, b_col - b_row + i_row, NEG_BIG)
            inter = b_col + m_old
            mt = jnp.maximum(jnp.max(dm, axis=1, keepdims=True), inter)
            p = jnp.exp(dm - mt)
            wi = jnp.exp(inter - mt)
            g_col = bl - b_col + i_col
            g_row = bl - b_row + i_row
            m_new = jnp.maximum(bl + m_old, jnp.max(g_row, axis=1, keepdims=True))
            a_prev = jnp.exp(bl + m_old - m_new)
            a_col = jnp.exp(g_col - m_new) * scale
            st.append((p, wi, mt, m_new, a_prev, a_col))

        qbs = [q_ref[rows, h * dk:(h + 1) * dk] for h in hs_]
        kbs = [k_ref[rows, h * dk:(h + 1) * dk] for h in hs_]
        vbs = [v_ref[rows, h * dv:(h + 1) * dv] for h in hs_]
        ss = [_dot_nt(qbs[h], kbs[h]) * (scale * st[h][0]) for h in hs_]
        qcs = [_dot(qbs[h], c_s[h].astype(BF16)) for h in hs_]
        n_olds = [n_s[h:h + 1, :] for h in hs_]
        for h in hs_:
            p, wi, mt, m_new, a_prev, a_col = st[h]
            s_ = ss[h]
            num = _dot(s_.astype(BF16), vbs[h]) + wi * qcs[h]
            qn = jnp.sum(qbs[h].astype(F32) * n_olds[h], axis=1, keepdims=True)
            den = jnp.sum(s_, axis=1, keepdims=True) + wi * qn
            hval = num / jnp.maximum(jnp.abs(den), jnp.exp(-mt))
            if combine:
                hs = hval + hf_ref[rows, h * dv:(h + 1) * dv].astype(F32)
                y = hs * lax.rsqrt(jnp.mean(hs * hs, axis=-1, keepdims=True) + EPS)
                gate = o_ref[rows, h * dv:(h + 1) * dv].astype(F32)
                y = y * nw_ref[:, h * dv:(h + 1) * dv] * _sigmoid(gate)
                out_ref[rows, h * dv:(h + 1) * dv] = y.astype(out_ref.dtype)
            else:
                out_ref[rows, h * dv:(h + 1) * dv] = hval.astype(out_ref.dtype)
        for h in hs_:
            p, wi, mt, m_new, a_prev, a_col = st[h]
            ak = a_col * kbs[h].astype(F32)
            c_s[h] = a_prev * c_s[h] + _dot_tn(ak.astype(BF16), vbs[h])
            n_s[h:h + 1, :] = a_prev * n_olds[h] + jnp.sum(ak, axis=0, keepdims=True)
            m_s[h:h + 1, :] = jnp.broadcast_to(m_new, (1, LANES))
        return carry

    lax.fori_loop(0, CHUNKS_PER_STEP, chunk, 0)

    @pl.when(jnp.logical_and(last[w] == 1, jnp.logical_not(lat)))
    def _emit():
        if combine:
            cout_ref[0, 0] = cf_ref[0]
            nout_ref[0, 0] = nf_ref[0]
            mout_ref[0, 0] = mf_ref[0]
            cout_ref[0, 1] = c_s[...]
            nout_ref[0, 1] = n_s[...]
            mout_ref[0, 1] = m_s[...]
        else:
            cout_ref[0] = c_s[...]
            nout_ref[0] = n_s[...]
            mout_ref[0] = m_s[...]


def _mlstm_scan(z, zg, gbias, st_c, st_n, st_m, dims, reverse, fwd=None, norm_w=None):
    (n_ctx, t_ctx, n_lat, t_lat, heads, dk, dv) = dims
    combine = fwd is not None
    m = z.shape[0]
    d = 1 if reverse else 0
    step_rows = CHUNK * CHUNKS_PER_STEP
    tabs = _scan_schedule(n_ctx, t_ctx, n_lat, t_lat, reverse)
    qw, vw = heads * dk, heads * dv
    assert (2 * qw) % vw == 0
    v_blk = (2 * qw) // vw
    in_specs = [
        pl.BlockSpec((step_rows, qw), lambda w, rb, *_: (rb[w], 0)),
        pl.BlockSpec((step_rows, qw), lambda w, rb, *_: (rb[w], 1)),
        pl.BlockSpec((step_rows, vw), lambda w, rb, *_: (rb[w], v_blk)),
        pl.BlockSpec((step_rows, LANES), lambda w, rb, *_: (rb[w], 0)),
        pl.BlockSpec((1, LANES), lambda w, *_: (0, 0)),
        pl.BlockSpec((1, 1, heads, dk, dv), lambda w, rb, f, l, cb, lb, *_: (lb[w], d, 0, 0, 0)),
        pl.BlockSpec((1, 1, heads, dk), lambda w, rb, f, l, cb, lb, *_: (lb[w], d, 0, 0)),
        pl.BlockSpec((1, 1, heads, LANES), lambda w, rb, f, l, cb, lb, *_: (lb[w], d, 0, 0)),
    ]
    args = [z, z, z, zg, gbias, st_c, st_n, st_m]
    if combine:
        hf, cf, nf, mf = fwd
        in_specs += [
            pl.BlockSpec((step_rows, vw), lambda w, rb, *_: (rb[w], 0)),
            pl.BlockSpec((step_rows, vw), lambda w, rb, *_: (rb[w], v_blk + 1)),
            pl.BlockSpec((1, vw), lambda w, *_: (0, 0)),
            pl.BlockSpec((1, heads, dk, dv), lambda w, rb, f, l, cb, *_: (cb[w], 0, 0, 0)),
            pl.BlockSpec((1, heads, dk), lambda w, rb, f, l, cb, *_: (cb[w], 0, 0)),
            pl.BlockSpec((1, heads, LANES), lambda w, rb, f, l, cb, *_: (cb[w], 0, 0)),
        ]
        args += [hf, z, norm_w.reshape(1, vw), cf, nf, mf]
        nd = (2,)
        st_idx = lambda w, rb, f, l, cb, *_: (cb[w], 0, 0, 0)
        c_idx = lambda w, rb, f, l, cb, *_: (cb[w], 0, 0, 0, 0)
    else:
        nd = ()
        st_idx = lambda w, rb, f, l, cb, *_: (cb[w], 0, 0)
        c_idx = lambda w, rb, f, l, cb, *_: (cb[w], 0, 0, 0)
    out_specs = [
        pl.BlockSpec((step_rows, vw), lambda w, rb, *_: (rb[w], 0)),
        pl.BlockSpec((1,) + nd + (heads, dk, dv), c_idx),
        pl.BlockSpec((1,) + nd + (heads, dk), st_idx),
        pl.BlockSpec((1,) + nd + (heads, LANES), st_idx),
    ]
    out_shape = [
        jax.ShapeDtypeStruct((m, vw), BF16),
        jax.ShapeDtypeStruct((n_ctx,) + nd + (heads, dk, dv), F32),
        jax.ShapeDtypeStruct((n_ctx,) + nd + (heads, dk), F32),
        jax.ShapeDtypeStruct((n_ctx,) + nd + (heads, LANES), F32),
    ]
    return pl.pallas_call(
        functools.partial(_mlstm_kernel, heads=heads, dk=dk, dv=dv, reverse=reverse, combine=combine),
        grid_spec=pltpu.PrefetchScalarGridSpec(
            num_scalar_prefetch=len(tabs), grid=(m // step_rows,),
            in_specs=in_specs, out_specs=out_specs,
            scratch_shapes=[pltpu.VMEM((heads, dk, dv), F32),
                            pltpu.VMEM((heads, dk), F32),
                            pltpu.VMEM((heads, LANES), F32)]),
        out_shape=out_shape,
        compiler_params=_cparams("arbitrary"),
        name="mlstm_bwd" if reverse else "mlstm_fwd",
    )(*tabs, *args)


def _rope_partner(x, d):
    q, hlf = d // 4, d // 2
    return jnp.concatenate([x[:, q:hlf], x[:, :q], x[:, hlf + q:], x[:, hlf:hlf + q]], axis=1)


def _ret_kernel(rowblk, first, last, ctxb, latb, islat, posblk,
                q_ref, k_ref, v_ref, cos_ref, sin_ref, ld_ref, s0_ref, *rest,
                heads, dk, dv, reverse, combine):
    if combine:
        of_ref, g_ref, nw_ref, sf_ref, out_ref, sout_ref, s_s = rest
    else:
        out_ref, sout_ref, s_s = rest
    w = pl.program_id(0)
    lat = islat[w] == 1

    @pl.when(first[w] == 1)
    def _init():
        s_s[...] = jnp.where(lat, s0_ref[0, 0], 0.0)

    L = CHUNK
    ti = lax.broadcasted_iota(I32, (L, L), 0).astype(F32)
    si = lax.broadcasted_iota(I32, (L, L), 1).astype(F32)
    diff = (si - ti) if reverse else (ti - si)
    idx = lax.broadcasted_iota(I32, (L, 1), 0).astype(F32)
    q_pow = (L - idx) if reverse else (idx + 1.0)
    k_pow = idx if reverse else (L - 1.0 - idx)
    scale = dk ** -0.5
    drow = 1 if reverse else 0
    log_decay = -jnp.exp(ld_ref[drow:drow + 1, :])
    hs_ = range(heads)

    def chunk(ci, carry):
        rows = _chunk_rows(ci, reverse)
        cos_t, sin_t = cos_ref[rows, :], sin_ref[rows, :]
        qbs, kbs, kds, vbs = [], [], [], []
        for h in hs_:
            ld = log_decay[:, h:h + 1]
            q = q_ref[rows, h * dk:(h + 1) * dk].astype(F32)
            k = k_ref[rows, h * dk:(h + 1) * dk].astype(F32)
            q = q * cos_t + _rope_partner(q, dk) * sin_t
            k = k * cos_t + _rope_partner(k, dk) * sin_t
            qbs.append(q.astype(BF16))
            kbs.append(k.astype(BF16))
            kds.append((k * (scale * jnp.exp(k_pow * ld))).astype(BF16))
            vbs.append(v_ref[rows, h * dv:(h + 1) * dv])
        scores = [_dot_nt(qbs[h], kbs[h]) for h in hs_]
        inter = [_dot(qbs[h], s_s[h].astype(BF16)) for h in hs_]
        for h in hs_:
            ld = log_decay[:, h:h + 1]
            intra = jnp.where(diff >= 0.0, jnp.exp(jnp.maximum(diff, 0.0) * ld), 0.0)
            a = scores[h] * (scale * intra)
            o = _dot(a.astype(BF16), vbs[h]) + inter[h] * jnp.exp(q_pow * ld)
            if combine:
                hs = o + of_ref[rows, h * dv:(h + 1) * dv].astype(F32)
                y = hs * lax.rsqrt(jnp.mean(hs * hs, axis=-1, keepdims=True) + EPS)
                gate = g_ref[rows, h * dv:(h + 1) * dv].astype(F32)
                y = y * nw_ref[:, h * dv:(h + 1) * dv] * (gate * _sigmoid(gate))
                out_ref[rows, h * dv:(h + 1) * dv] = y.astype(out_ref.dtype)
            else:
                out_ref[rows, h * dv:(h + 1) * dv] = o.astype(out_ref.dtype)
        for h in hs_:
            ld = log_decay[:, h:h + 1]
            s_s[h] = jnp.exp(float(L) * ld) * s_s[h] + _dot_tn(kds[h], vbs[h])
        return carry

    lax.fori_loop(0, CHUNKS_PER_STEP, chunk, 0)

    @pl.when(jnp.logical_and(last[w] == 1, jnp.logical_not(lat)))
    def _emit():
        if combine:
            sout_ref[0, 0] = sf_ref[0]
            sout_ref[0, 1] = s_s[...]
        else:
            sout_ref[0] = s_s[...]


def _ret_scan(z, cos_tab, sin_tab, ld, st_s, dims, col0, reverse, fwd=None, norm_w=None):
    (n_ctx, t_ctx, n_lat, t_lat, heads, dk, dv) = dims
    combine = fwd is not None
    m = z.shape[0]
    d = 1 if reverse else 0
    step_rows = CHUNK * CHUNKS_PER_STEP
    tabs = _scan_schedule(n_ctx, t_ctx, n_lat, t_lat, reverse)
    qw, vw = heads * dk, heads * dv
    assert qw == vw and col0 % qw == 0
    b0 = col0 // qw
    in_specs = [
        pl.BlockSpec((step_rows, qw), lambda w, rb, *_: (rb[w], b0)),
        pl.BlockSpec((step_rows, qw), lambda w, rb, *_: (rb[w], b0 + 1)),
        pl.BlockSpec((step_rows, vw), lambda w, rb, *_: (rb[w], b0 + 2)),
        pl.BlockSpec((step_rows, dk), lambda w, rb, f, l, cb, lb, il, pb: (pb[w], 0)),
        pl.BlockSpec((step_rows, dk), lambda w, rb, f, l, cb, lb, il, pb: (pb[w], 0)),
        pl.BlockSpec((SUBLANES, LANES), lambda w, *_: (0, 0)),
        pl.BlockSpec((1, 1, heads, dk, dv), lambda w, rb, f, l, cb, lb, *_: (lb[w], d, 0, 0, 0)),
    ]
    args = [z, z, z, cos_tab, sin_tab, ld, st_s]
    if combine:
        of, sf = fwd
        in_specs += [
            pl.BlockSpec((step_rows, vw), lambda w, rb, *_: (rb[w], 0)),
            pl.BlockSpec((step_rows, vw), lambda w, rb, *_: (rb[w], b0 + 3)),
            pl.BlockSpec((1, vw), lambda w, *_: (0, 0)),
            pl.BlockSpec((1, heads, dk, dv), lambda w, rb, f, l, cb, *_: (cb[w], 0, 0, 0)),
        ]
        args += [of, z, norm_w.reshape(1, vw), sf]
        s_spec = pl.BlockSpec((1, 2, heads, dk, dv), lambda w, rb, f, l, cb, *_: (cb[w], 0, 0, 0, 0))
        s_shape = jax.ShapeDtypeStruct((n_ctx, 2, heads, dk, dv), F32)
    else:
        s_spec = pl.BlockSpec((1, heads, dk, dv), lambda w, rb, f, l, cb, *_: (cb[w], 0, 0, 0))
        s_shape = jax.ShapeDtypeStruct((n_ctx, heads, dk, dv), F32)
    return pl.pallas_call(
        functools.partial(_ret_kernel, heads=heads, dk=dk, dv=dv, reverse=reverse, combine=combine),
        grid_spec=pltpu.PrefetchScalarGridSpec(
            num_scalar_prefetch=len(tabs), grid=(m // step_rows,),
            in_specs=in_specs,
            out_specs=[pl.BlockSpec((step_rows, vw), lambda w, rb, *_: (rb[w], 0)), s_spec],
            scratch_shapes=[pltpu.VMEM((heads, dk, dv), F32)]),
        out_shape=[jax.ShapeDtypeStruct((m, vw), BF16), s_shape],
        compiler_params=_cparams("arbitrary"),
        name="ret_bwd" if reverse else "ret_fwd",
    )(*tabs, *args)


def _rope_tables(t_lat, d):
    quarter = d // 4
    rows = t_lat // GRID_W
    row = jnp.repeat(jnp.arange(rows, dtype=F32), GRID_W)
    col = jnp.tile(jnp.arange(GRID_W, dtype=F32), rows)
    inv = ROPE_BASE ** (-jnp.arange(quarter, dtype=F32) / quarter)
    ar = row[:, None] * inv[None, :]
    ac = col[:, None] * inv[None, :]
    cos_t = jnp.concatenate([jnp.cos(ar), jnp.cos(ar), jnp.cos(ac), jnp.cos(ac)], axis=1)
    sin_t = jnp.concatenate([-jnp.sin(ar), jnp.sin(ar), -jnp.sin(ac), jnp.sin(ac)], axis=1)
    pad_rows = CHUNK * CHUNKS_PER_STEP
    cos_t = jnp.concatenate([cos_t, jnp.ones((pad_rows, d), F32)], axis=0)
    sin_t = jnp.concatenate([sin_t, jnp.zeros((pad_rows, d), F32)], axis=0)
    return cos_t, sin_t


def _outproj_kernel(a1_ref, a2_ref, b1_ref, b2_ref, xc_ref, xl_ref, mod_ref, o_ref, *, gate_idx, n_ctx_tiles):
    acc = _dot(a1_ref[...], b1_ref[...]) + _dot(a2_ref[...], b2_ref[...])
    upd = mod_ref[0, gate_idx:gate_idx + 1, :] * acc
    i = pl.program_id(0)

    @pl.when(i < n_ctx_tiles)
    def _ctx():
        o_ref[...] = xc_ref[...] + upd

    @pl.when(i >= n_ctx_tiles)
    def _lat():
        o_ref[...] = xl_ref[...] + upd


def _outproj(a1, a2, w_out, xc, xl, mod, t_lat, gate_idx):
    m, k1 = a1.shape
    k2 = a2.shape[1]
    n = w_out.shape[1]
    n_ctx_rows = xc.shape[0]
    assert k1 == k2
    tm = _divisor(math.gcd(n_ctx_rows, t_lat), 1024)
    tn = _divisor(n, 512, LANES)
    nct = n_ctx_rows // tm
    mrow = functools.partial(_mod_row, tm=tm, n_ctx_rows=n_ctx_rows, t_lat=t_lat)
    return pl.pallas_call(
        functools.partial(_outproj_kernel, gate_idx=gate_idx, n_ctx_tiles=nct),
        grid=(m // tm, n // tn),
        in_specs=[pl.BlockSpec((tm, k1), lambda i, j: (i, 0)),
                  pl.BlockSpec((tm, k2), lambda i, j: (i, 0)),
                  pl.BlockSpec((k1, tn), lambda i, j: (0, j)),
                  pl.BlockSpec((k2, tn), lambda i, j: (1, j)),
                  pl.BlockSpec((tm, tn), lambda i, j: (jnp.minimum(i, nct - 1), jnp.where(i < nct, j, 0))),
                  pl.BlockSpec((tm, tn), lambda i, j: (jnp.maximum(i - nct, 0), jnp.where(i >= nct, j, 0))),
                  pl.BlockSpec((1, mod.shape[1], tn), lambda i, j: (mrow(i), 0, j))],
        out_specs=pl.BlockSpec((tm, tn), lambda i, j: (i, j)),
        out_shape=jax.ShapeDtypeStruct((m, n), F32),
        compiler_params=_cparams("arbitrary", "arbitrary"),
        name="out_proj",
    )(a1, a2, w_out, w_out, xc, xl, mod)


def _router_kernel(x_ref, w_ref, mod_ref, wr_ref, br_ref, h_ref, eid_ref, wt_ref,
                   *, shift_idx, scale_idx, n_groups, per_group):
    x = x_ref[...]
    y = x * lax.rsqrt(jnp.mean(x * x, axis=-1, keepdims=True) + EPS) * w_ref[...]
    y = y * (1.0 + mod_ref[0, scale_idx:scale_idx + 1, :]) + mod_ref[0, shift_idx:shift_idx + 1, :]
    _store_token_major(h_ref, _pack_bf16_pair(y))
    y_hi, y_lo = _split_bf16(y)
    w_hi, w_lo = _split_bf16(wr_ref[...])
    logits = _dot(y_hi, w_hi) + _dot(y_lo, w_hi) + _dot(y_hi, w_lo) + br_ref[...]

    n_exp = n_groups * per_group
    lane = lax.broadcasted_iota(I32, logits.shape, 1)
    gmask = lane < n_groups
    gl = jnp.where(gmask, logits, NEG_BIG)
    gmax = jnp.max(gl, axis=1, keepdims=True)
    gsum = jnp.sum(jnp.where(gmask, jnp.exp(gl - gmax), 0.0), axis=1, keepdims=True)
    g_w = 1.0 / gsum
    g_idx = jnp.min(jnp.where(gl == gmax, lane, LANES), axis=1, keepdims=True)

    in_group = jnp.logical_and(lane >= n_groups + g_idx * per_group,
                               lane < n_groups + (g_idx + 1) * per_group)
    in_group = jnp.logical_and(in_group, lane < n_groups + n_exp)
    el = jnp.where(in_group, logits, NEG_BIG)
    m1 = jnp.max(el, axis=1, keepdims=True)
    i1 = jnp.min(jnp.where(el == m1, lane, LANES), axis=1, keepdims=True)
    el2 = jnp.where(lane == i1, NEG_BIG, el)
    m2 = jnp.max(el2, axis=1, keepdims=True)
    i2 = jnp.min(jnp.where(el2 == m2, lane, LANES), axis=1, keepdims=True)
    e2 = jnp.exp(m2 - m1)
    p1 = 1.0 / (1.0 + e2)
    p2 = e2 * p1
    eid_ref[...] = jnp.where(lane == 0, i1 - n_groups, jnp.where(lane == 1, i2 - n_groups, 0))
    wt_ref[...] = jnp.where(lane == 0, g_w * p1, jnp.where(lane == 1, g_w * p2, 0.0))


def _router(x, w, mod, w_r, b_r, n_ctx_rows, t_lat, shift_idx, scale_idx, n_groups, per_group):
    m, d = x.shape
    tm = _divisor(math.gcd(n_ctx_rows, t_lat), 256)
    mrow = functools.partial(_mod_row, tm=tm, n_ctx_rows=n_ctx_rows, t_lat=t_lat)
    return pl.pallas_call(
        functools.partial(_router_kernel, shift_idx=shift_idx, scale_idx=scale_idx,
                          n_groups=n_groups, per_group=per_group),
        grid=(m // tm,),
        in_specs=[pl.BlockSpec((tm, d), lambda i: (i, 0)),
                  pl.BlockSpec((1, d), lambda i: (0, 0)),
                  pl.BlockSpec((1, mod.shape[1], d), lambda i: (mrow(i), 0, 0)),
                  pl.BlockSpec((d, LANES), lambda i: (0, 0)),
                  pl.BlockSpec((1, LANES), lambda i: (0, 0))],
        out_specs=[pl.BlockSpec((tm, SUBLANES, d // (2 * SUBLANES)), lambda i: (i, 0, 0)),
                   pl.BlockSpec((tm, LANES), lambda i: (i, 0)),
                   pl.BlockSpec((tm, LANES), lambda i: (i, 0))],
        out_shape=[jax.ShapeDtypeStruct((m, SUBLANES, d // (2 * SUBLANES)), U32),
                   jax.ShapeDtypeStruct((m, LANES), I32),
                   jax.ShapeDtypeStruct((m, LANES), F32)],
        compiler_params=_cparams("parallel"),
        name="router",
    )(x, w.reshape(1, d), mod, w_r, b_r)


GATHER_UNROLL = 8


def _row_copy(src_hbm, dst_buf, sem, src_row, slot, dst_row):
    return pltpu.make_async_copy(src_hbm.at[src_row], dst_buf.at[slot, dst_row], sem.at[slot])


def _start_row_gather(tok_fn, src_hbm, buf, sem, slot, n_idx):
    def body(jj, carry):
        for u in range(GATHER_UNROLL):
            j = jj * GATHER_UNROLL + u
            _row_copy(src_hbm, buf, sem, tok_fn(j), slot, j).start(priority=u % 2)
        return carry

    lax.fori_loop(0, n_idx // GATHER_UNROLL, body, 0)


def _wait_row_gather(src_hbm, buf, sem, slot, n_idx):
    def body(j, carry):
        _row_copy(src_hbm, buf, sem, 0, slot, j).wait()
        return carry

    lax.fori_loop(0, n_idx, body, 0, unroll=GATHER_UNROLL)


def _ring_step(cur_fn, nxt_fn, src_hbm, buf, sem, n_idx, consume):
    i = pl.program_id(0)

    @pl.when(i == 0)
    def _prime():
        _start_row_gather(cur_fn, src_hbm, buf, sem, 0, n_idx)

    for slot in range(2):
        @pl.when(i % 2 == slot)
        def _work(slot=slot):
            @pl.when(i + 1 < pl.num_programs(0))
            def _prefetch():
                _start_row_gather(nxt_fn, src_hbm, buf, sem, 1 - slot, n_idx)

            _wait_row_gather(src_hbm, buf, sem, slot, n_idx)
            consume(slot)


def _scatter_rows_kernel(cnt, dst_ref, h_ref, xs_hbm, zero, sem, *, rows, blk_rows):
    n_blk = cnt.shape[0]

    def pad_copy(b, r):
        return pltpu.make_async_copy(zero.at[0], xs_hbm.at[b * blk_rows + r], sem.at[1])

    def for_each_pad(fn):
        def blk(b, carry):
            def row(r, c2):
                fn(pad_copy(b, r))
                return c2
            return lax.fori_loop(cnt[b], blk_rows, row, carry)
        lax.fori_loop(0, n_blk, blk, 0)

    @pl.when(pl.program_id(0) == 0)
    def _pad():
        zero[...] = jnp.zeros(zero.shape, zero.dtype)
        for_each_pad(lambda cp: cp.start())
        for_each_pad(lambda cp: cp.wait())

    def row_copy(j, r):
        return pltpu.make_async_copy(h_ref.at[r], xs_hbm.at[dst_ref[0, 0, j]], sem.at[0])

    def issue(rr, carry):
        for u in range(GATHER_UNROLL // TOP_K):
            r = rr * (GATHER_UNROLL // TOP_K) + u
            for kk in range(TOP_K):
                row_copy(r * TOP_K + kk, r).start(priority=kk % 2)
        return carry

    lax.fori_loop(0, rows * TOP_K // GATHER_UNROLL, issue, 0)

    def drain(r, carry):
        for kk in range(TOP_K):
            row_copy(r * TOP_K + kk, r).wait()
        return carry

    lax.fori_loop(0, rows, drain, 0, unroll=GATHER_UNROLL // TOP_K)


def _scatter_rows(src, dest, blk_cnt, blk_rows):
    t = src.shape[0]
    rows = _divisor(t, 256)
    n_slots = blk_cnt.shape[0] * blk_rows
    return pl.pallas_call(
        functools.partial(_scatter_rows_kernel, rows=rows, blk_rows=blk_rows),
        grid_spec=pltpu.PrefetchScalarGridSpec(
            num_scalar_prefetch=1, grid=(t // rows,),
            in_specs=[pl.BlockSpec((1, 1, TOP_K * rows), lambda i, *_: (i, 0, 0), memory_space=pltpu.SMEM),
                      pl.BlockSpec((rows,) + src.shape[1:], lambda i, *_: (i, 0, 0))],
            out_specs=pl.BlockSpec(memory_space=pl.ANY),
            scratch_shapes=[pltpu.VMEM((1,) + src.shape[1:], src.dtype), pltpu.SemaphoreType.DMA((2,))]),
        out_shape=jax.ShapeDtypeStruct((n_slots,) + src.shape[1:], src.dtype),
        compiler_params=_cparams("arbitrary"),
        name="moe_dispatch",
    )(blk_cnt, dest.reshape(t // rows, 1, TOP_K * rows), src)


def _moe_up_kernel(xblk, eid, wjt, oblk, ojt, fst, valid, x_ref, wg_ref, wu_ref, h_ref, wg_s, wu_s, flat):
    w = pl.program_id(0)

    @pl.when(valid[w] == 1)
    def _go():
        @pl.when(fst[w] == 1)
        def _cast():
            wg_s[...] = wg_ref[0].astype(BF16)
            wu_s[...] = wu_ref[0].astype(BF16)

        wd = x_ref.shape[-1]
        for sl in range(SUBLANES):
            flat[:, sl * wd:(sl + 1) * wd] = x_ref[:, sl, :]
        hi, lo = _unpack_bf16_pair(flat[...])
        xh, xl = hi.astype(BF16), lo.astype(BF16)
        n = xh.shape[1]
        a = _dot(xh, wg_s[:n, :]) + _dot(xl, wg_s[n:, :])
        b = _dot(xh, wu_s[:n, :]) + _dot(xl, wu_s[n:, :])
        h_ref[...] = (a * _sigmoid(a) * b).astype(h_ref.dtype)

    @pl.when(valid[w] == 0)
    def _pad():
        h_ref[...] = jnp.zeros(h_ref.shape, h_ref.dtype)


def _moe_up(xs, w_gate, w_up, tabs, n_work, tn):
    n_slots = xs.shape[0]
    d = w_gate.shape[1]
    f = w_gate.shape[2]
    return pl.pallas_call(
        _moe_up_kernel,
        grid_spec=pltpu.PrefetchScalarGridSpec(
            num_scalar_prefetch=7, grid=(n_work,),
            in_specs=[pl.BlockSpec((MOE_TILE,) + xs.shape[1:], lambda w, xb, e, wj, *_: (xb[w], 0, 0)),
                      pl.BlockSpec((1, d, tn), lambda w, xb, e, wj, *_: (e[w], 0, wj[w])),
                      pl.BlockSpec((1, d, tn), lambda w, xb, e, wj, *_: (e[w], 0, wj[w]))],
            out_specs=pl.BlockSpec((MOE_TILE, tn), lambda w, xb, e, wj, ob, oj, *_: (ob[w], oj[w])),
            scratch_shapes=[pltpu.VMEM((d, tn), BF16), pltpu.VMEM((d, tn), BF16),
                            pltpu.VMEM((MOE_TILE, xs.shape[1] * xs.shape[2]), xs.dtype)]),
        out_shape=jax.ShapeDtypeStruct((n_slots, f), BF16),
        compiler_params=_cparams("arbitrary"),
        name="moe_up",
    )(*tabs, xs, w_gate, w_up)


def _moe_down_kernel(eid, fst, valid, h_ref, wd_ref, y_ref, wd_s):
    w = pl.program_id(0)

    @pl.when(valid[w] == 1)
    def _go():
        @pl.when(fst[w] == 1)
        def _cast():
            wd_s[...] = wd_ref[0].astype(BF16)

        _store_token_major(y_ref, _pack_bf16_pair(_dot(h_ref[...], wd_s[...])))

    @pl.when(valid[w] == 0)
    def _pad():
        y_ref[...] = jnp.zeros(y_ref.shape, y_ref.dtype)


def _moe_down(hs, w_down, tabs):
    n_slots, f = hs.shape
    d = w_down.shape[2]
    nblk = n_slots // MOE_TILE
    return pl.pallas_call(
        _moe_down_kernel,
        grid_spec=pltpu.PrefetchScalarGridSpec(
            num_scalar_prefetch=3, grid=(nblk,),
            in_specs=[pl.BlockSpec((MOE_TILE, f), lambda w, eid, *_: (w, 0)),
                      pl.BlockSpec((1, f, d), lambda w, eid, *_: (eid[w], 0, 0))],
            out_specs=pl.BlockSpec((MOE_TILE, SUBLANES, d // (2 * SUBLANES)), lambda w, eid, *_: (w, 0, 0)),
            scratch_shapes=[pltpu.VMEM((f, d), BF16)]),
        out_shape=jax.ShapeDtypeStruct((n_slots, SUBLANES, d // (2 * SUBLANES)), U32),
        compiler_params=_cparams("arbitrary"),
        name="moe_down",
    )(*tabs, hs, w_down)


def _final_kernel(dst_ref, nxt_ref, x_ref, wt_ref, mod_ref, fw_ref, ys_hbm, o_ref, buf, sem, *, rows, gate_idx):
    def consume(slot):
        wt = wt_ref[...]
        moe = None
        for kk in range(TOP_K):
            hi, lo = _unpack_bf16_pair(_load_token_major(buf, slot, kk * rows, rows))
            term = wt[:, kk:kk + 1] * jnp.concatenate([hi, lo], axis=1)
            moe = term if moe is None else moe + term
        x = x_ref[...] + mod_ref[0, gate_idx:gate_idx + 1, :] * moe
        o_ref[...] = x * lax.rsqrt(jnp.mean(x * x, axis=-1, keepdims=True) + EPS) * fw_ref[...]

    _ring_step(lambda j: dst_ref[0, 0, j], lambda j: nxt_ref[0, 0, j], ys_hbm, buf, sem, TOP_K * rows, consume)


def _final(x1, wt, dest, ys, mod, final_w, row0, n_rows, n_ctx_rows, t_lat, gate_idx):
    d = x1.shape[1]
    tm = _divisor(math.gcd(n_ctx_rows, t_lat), 128)
    t0 = row0 // tm
    nt = n_rows // tm
    mrow = functools.partial(_mod_row, tm=tm, n_ctx_rows=n_ctx_rows, t_lat=t_lat)
    dest3 = dest.reshape(-1, tm, TOP_K).transpose(0, 2, 1).reshape(-1, 1, TOP_K * tm)
    return pl.pallas_call(
        functools.partial(_final_kernel, rows=tm, gate_idx=gate_idx),
        grid=(nt,),
        in_specs=[pl.BlockSpec((1, 1, TOP_K * tm), lambda i: (t0 + i, 0, 0), memory_space=pltpu.SMEM),
                  pl.BlockSpec((1, 1, TOP_K * tm), lambda i: (t0 + jnp.minimum(i + 1, nt - 1), 0, 0),
                               memory_space=pltpu.SMEM),
                  pl.BlockSpec((tm, d), lambda i: (t0 + i, 0)),
                  pl.BlockSpec((tm, LANES), lambda i: (t0 + i, 0)),
                  pl.BlockSpec((1, mod.shape[1], d), lambda i: (mrow(t0 + i), 0, 0)),
                  pl.BlockSpec((1, d), lambda i: (0, 0)),
                  pl.BlockSpec(memory_space=pl.ANY)],
        out_specs=pl.BlockSpec((tm, d), lambda i: (i, 0)),
        out_shape=jax.ShapeDtypeStruct((n_rows, d), F32),
        scratch_shapes=[pltpu.VMEM((2, TOP_K * tm) + ys.shape[1:], ys.dtype), pltpu.SemaphoreType.DMA((2,))],
        compiler_params=_cparams("arbitrary"),
        name="moe_combine_final",
    )(dest3, dest3, x1, wt, mod, final_w.reshape(1, d), ys)


def _count_le(sorted_ends, idx):
    return jnp.sum((sorted_ends[None, :] <= idx[:, None]).astype(I32), axis=1)


def _moe_plan(eid, n_exp, n_jt):
    t = eid.shape[0]
    n_assign = t * TOP_K
    eflat = eid.reshape(n_assign)
    onehot = (eflat[:, None] == jnp.arange(n_exp, dtype=I32)[None, :]).astype(I32)
    csum = jnp.cumsum(onehot, axis=0)
    counts = csum[-1]
    nb = (counts + MOE_TILE - 1) // MOE_TILE
    blk_end = jnp.cumsum(nb)
    blk_start = blk_end - nb
    n_blocks = blk_end[-1]
    pad_start = blk_start * MOE_TILE
    dest = jnp.sum(onehot * (csum - 1 + pad_start[None, :]), axis=1)

    assert n_assign % MOE_TILE == 0
    nblk_max = n_assign // MOE_TILE + n_exp
    b_idx = jnp.arange(nblk_max, dtype=I32)
    blk_e = jnp.minimum(_count_le(blk_end, b_idx), n_exp - 1)
    blk_off = (b_idx - blk_start[blk_e]) * MOE_TILE
    blk_cnt = jnp.where(b_idx < n_blocks, jnp.clip(counts[blk_e] - blk_off, 0, MOE_TILE), 0).astype(I32)

    d_valid = (b_idx < n_blocks).astype(I32)
    last_e = blk_e[jnp.maximum(n_blocks - 1, 0)]
    d_eid = jnp.where(d_valid == 1, blk_e, last_e).astype(I32)
    d_first = jnp.logical_and(d_valid == 1, b_idx == blk_start[d_eid]).astype(I32)

    n_work = n_jt * nblk_max
    w_idx = jnp.arange(n_work, dtype=I32)
    per_e = n_jt * nb
    w_end = jnp.cumsum(per_e)
    w_valid = w_idx < w_end[-1]
    we = jnp.minimum(_count_le(w_end, w_idx), n_exp - 1)
    r = w_idx - (w_end[we] - per_e[we])
    nbe = jnp.maximum(nb[we], 1)
    u_jt = r // nbe
    u_t = r - u_jt * nbe
    u_blk = blk_start[we] + u_t
    last_w = jnp.maximum(w_end[-1] - 1, 0)
    spare = w_idx - w_end[-1]
    u_xblk = jnp.where(w_valid, u_blk, u_blk[last_w]).astype(I32)
    u_eid = jnp.where(w_valid, we, we[last_w]).astype(I32)
    u_wjt = jnp.where(w_valid, u_jt, u_jt[last_w]).astype(I32)
    u_oblk = jnp.where(w_valid, u_blk, n_blocks + spare // n_jt).astype(I32)
    u_ojt = jnp.where(w_valid, u_jt, spare % n_jt).astype(I32)
    u_first = jnp.logical_and(w_valid, u_t == 0).astype(I32)
    up_tabs = (u_xblk, u_eid, u_wjt, u_oblk, u_ojt, u_first, w_valid.astype(I32))
    down_tabs = (d_eid, d_first, d_valid)
    return dest.astype(I32), blk_cnt, up_tabs, down_tabs, n_work


def kernel(x_prompt, x_sample, state_mlstm_C, state_mlstm_n, state_mlstm_m, state_ret_S, c, c_ctx,
           norm1_w, norm2_w, w_ada, b_ada, w_in, b_mgates, mlstm_norm_w, ret_norm_w, ret_log_decay, w_out,
           w_router_group, b_router_group, w_router_expert, b_router_expert, w_exp_gate, w_exp_up,
           w_exp_down, final_norm_w):
    n_ctx, t_ctx, d_model = x_prompt.shape
    n_lat, t_lat, _ = x_sample.shape
    depth = norm1_w.shape[0]
    assert depth == 1, "single-layer trunk"
    m_heads, m_dk, m_dv = state_mlstm_C.shape[3:]
    r_heads, r_dk, r_dv = state_ret_S.shape[3:]
    n_groups = w_router_group.shape[2]
    n_exp = w_router_expert.shape[2]
    per_group = n_exp // n_groups
    d_exp = w_exp_gate.shape[3]
    n_gates = N_GATE_ROWS * m_heads
    n_ctx_rows = n_ctx * t_ctx
    step_rows = CHUNK * CHUNKS_PER_STEP
    assert t_ctx % step_rows == 0 and t_lat % step_rows == 0 and n_gates <= LANES
    assert n_groups + n_exp <= LANES

    cvec = jnp.concatenate([c_ctx[None, :], c, jnp.zeros((SUBLANES - 1 - n_lat, d_model), F32)], axis=0)
    mod = _ada_mod(cvec, w_ada[0], b_ada[0]).reshape(SUBLANES, 6, d_model)

    xc = x_prompt.reshape(n_ctx_rows, d_model)
    xl = x_sample.reshape(n_lat * t_lat, d_model)
    h = _norm_mod(xc, xl, norm1_w[0], mod, t_lat, 0, 1, BF16)

    g0 = 2 * m_heads * m_dk + 2 * m_heads * m_dv
    w_main = jnp.concatenate([w_in[0][:, :g0].astype(BF16), w_in[0][:, g0 + n_gates:].astype(BF16)], axis=1)
    w_gate = jnp.pad(w_in[0][:, g0:g0 + n_gates], ((0, 0), (0, LANES - n_gates))).astype(BF16)
    z, zg = _inproj(h, w_main, w_gate)

    gbias = jnp.pad(b_mgates[0].reshape(1, n_gates), ((0, 0), (0, LANES - n_gates)))
    st_m = jnp.broadcast_to(state_mlstm_m[:, 0][..., None], (n_lat, 2, m_heads, LANES))
    mdims = (n_ctx, t_ctx, n_lat, t_lat, m_heads, m_dk, m_dv)
    fwd_m = _mlstm_scan(z, zg, gbias, state_mlstm_C[:, 0], state_mlstm_n[:, 0], st_m, mdims, False)
    mix_m, new_c, new_n, new_m = _mlstm_scan(z, zg, gbias, state_mlstm_C[:, 0], state_mlstm_n[:, 0], st_m, mdims,
                                             True, fwd=fwd_m, norm_w=mlstm_norm_w[0])

    cos_tab, sin_tab = _rope_tables(t_lat, r_dk)
    ld = jnp.pad(ret_log_decay[0], ((0, SUBLANES - 2), (0, LANES - r_heads)))
    rdims = (n_ctx, t_ctx, n_lat, t_lat, r_heads, r_dk, r_dv)
    fwd_r = _ret_scan(z, cos_tab, sin_tab, ld, state_ret_S[:, 0], rdims, g0, False)
    mix_r, new_s = _ret_scan(z, cos_tab, sin_tab, ld, state_ret_S[:, 0], rdims, g0, True,
                             fwd=fwd_r, norm_w=ret_norm_w[0])

    x1 = _outproj(mix_m, mix_r, w_out[0].astype(BF16), xc, xl, mod, t_lat, 2)

    w_r = jnp.pad(jnp.concatenate([w_router_group[0], w_router_expert[0]], axis=1),
                  ((0, 0), (0, LANES - n_groups - n_exp)))
    b_r = jnp.pad(jnp.concatenate([b_router_group[0], b_router_expert[0]])[None, :],
                  ((0, 0), (0, LANES - n_groups - n_exp)))
    h2, eid, wt = _router(x1, norm2_w[0], mod, w_r, b_r, n_ctx_rows, t_lat, 3, 4, n_groups, per_group)

    tn_up = _divisor(d_exp, 512, LANES)
    dest, blk_cnt, up_tabs, down_tabs, n_work = _moe_plan(eid[:, :TOP_K], n_exp, d_exp // tn_up)
    xs = _scatter_rows(h2, dest, blk_cnt, MOE_TILE)
    hs = _moe_up(xs, w_exp_gate[0], w_exp_up[0], up_tabs, n_work, tn_up)
    ys = _moe_down(hs, w_exp_down[0], down_tabs)

    y_ctx = _final(x1, wt, dest, ys, mod, final_norm_w, 0, n_ctx_rows, n_ctx_rows, t_lat, 5)
    y_lat = _final(x1, wt, dest, ys, mod, final_norm_w, n_ctx_rows, n_lat * t_lat, n_ctx_rows, t_lat, 5)

    return (y_ctx.reshape(n_ctx, t_ctx, d_model), y_lat.reshape(n_lat, t_lat, d_model),
            new_c[:, None], new_n[:, None], new_m[:, None, :, :, 0], new_s[:, None])
```

```python
import functools
import math

import numpy as np
import jax
import jax.numpy as jnp
from jax import lax
from jax.experimental import pallas as pl
from jax.experimental.pallas import tpu as pltpu

F32 = jnp.float32
BF16 = jnp.bfloat16
I32 = jnp.int32
U32 = jnp.uint32

CHUNK = 128
GRID_W = 64
ROPE_BASE = 10000.0
EPS = 1e-6
TOP_K = 2
N_GATE_ROWS = 4

LANES = 128
SUBLANES = 8
VMEM_LIMIT_BYTES = 56 * 1024 * 1024
NEG_BIG = -1e30
MOE_TILE = 256


def _divisor(n, pref, mult=SUBLANES):
    if n <= pref:
        return n
    d = (pref // mult) * mult
    while d > mult and n % d:
        d -= mult
    assert n % d == 0, (n, pref, mult)
    return d


def _cparams(*sem):
    return pltpu.CompilerParams(dimension_semantics=sem, vmem_limit_bytes=VMEM_LIMIT_BYTES)


def _dot(a, b):
    return jnp.dot(a, b, preferred_element_type=F32)


def _dot_nt(a, b):
    return lax.dot_general(a, b, (((1,), (1,)), ((), ())), preferred_element_type=F32)


def _dot_tn(a, b):
    return lax.dot_general(a, b, (((0,), (0,)), ((), ())), preferred_element_type=F32)


def _split_bf16(x):
    hi = x.astype(BF16)
    lo = (x - hi.astype(F32)).astype(BF16)
    return hi, lo


def _sigmoid(x):
    return 1.0 / (1.0 + jnp.exp(-x))


def _log_sigmoid(x):
    return jnp.minimum(x, 0.0) - jnp.log(1.0 + jnp.exp(-jnp.abs(x)))


def _mod_row(i, tm, n_ctx_rows, t_lat):
    r0 = i * tm
    return jnp.where(r0 < n_ctx_rows, 0, 1 + (r0 - n_ctx_rows) // t_lat)


def _store_token_major(ref, y):
    w = ref.shape[-1]
    for s in range(SUBLANES):
        ref[:, s, :] = y[:, s * w:(s + 1) * w]


def _load_token_major(ref, slot, r0, rows):
    return jnp.concatenate([ref[slot, r0:r0 + rows, s, :] for s in range(SUBLANES)], axis=1)


def _pack_bf16_pair(y):
    n = y.shape[1] // 2
    hi = lax.bitcast_convert_type(y[:, :n].astype(BF16).astype(F32), U32)
    lo = lax.bitcast_convert_type(y[:, n:].astype(BF16).astype(F32), U32)
    return hi | (lo >> 16)


def _unpack_bf16_pair(p):
    hi = lax.bitcast_convert_type(p & jnp.uint32(0xFFFF0000), F32)
    lo = lax.bitcast_convert_type(p << 16, F32)
    return hi, lo


def _ada_kernel(c_ref, w_ref, b_ref, o_ref):
    c = c_ref[...]
    s = c * _sigmoid(c)
    s_hi, s_lo = _split_bf16(s)
    w_hi, w_lo = _split_bf16(w_ref[...])
    o_ref[...] = _dot(s_hi, w_hi) + _dot(s_lo, w_hi) + _dot(s_hi, w_lo) + b_ref[...]


def _ada_mod(cvec, w_ada, b_ada):
    r, d = cvec.shape
    n = w_ada.shape[1]
    tn = _divisor(n, 256, LANES)
    return pl.pallas_call(
        _ada_kernel,
        grid=(n // tn,),
        in_specs=[pl.BlockSpec((r, d), lambda j: (0, 0)),
                  pl.BlockSpec((d, tn), lambda j: (0, j)),
                  pl.BlockSpec((1, tn), lambda j: (0, j))],
        out_specs=pl.BlockSpec((r, tn), lambda j: (0, j)),
        out_shape=jax.ShapeDtypeStruct((r, n), F32),
        compiler_params=_cparams("parallel"),
        name="ada_mod",
    )(cvec, w_ada, b_ada.reshape(1, n))


def _norm_mod_kernel(xc_ref, xl_ref, w_ref, mod_ref, o_ref, *, shift_idx, scale_idx, n_ctx_tiles):
    def body(x_ref):
        x = x_ref[...]
        y = x * lax.rsqrt(jnp.mean(x * x, axis=-1, keepdims=True) + EPS) * w_ref[...]
        y = y * (1.0 + mod_ref[0, scale_idx:scale_idx + 1, :]) + mod_ref[0, shift_idx:shift_idx + 1, :]
        o_ref[...] = y.astype(o_ref.dtype)

    i = pl.program_id(0)
    pl.when(i < n_ctx_tiles)(lambda: body(xc_ref))
    pl.when(i >= n_ctx_tiles)(lambda: body(xl_ref))


def _norm_mod(xc, xl, w, mod, t_lat, shift_idx, scale_idx, out_dtype):
    n_ctx_rows, d = xc.shape
    m = n_ctx_rows + xl.shape[0]
    tm = _divisor(math.gcd(n_ctx_rows, t_lat), 256)
    nct = n_ctx_rows // tm
    mrow = functools.partial(_mod_row, tm=tm, n_ctx_rows=n_ctx_rows, t_lat=t_lat)
    return pl.pallas_call(
        functools.partial(_norm_mod_kernel, shift_idx=shift_idx, scale_idx=scale_idx, n_ctx_tiles=nct),
        grid=(m // tm,),
        in_specs=[pl.BlockSpec((tm, d), lambda i: (jnp.minimum(i, nct - 1), 0)),
                  pl.BlockSpec((tm, d), lambda i: (jnp.maximum(i - nct, 0), 0)),
                  pl.BlockSpec((1, d), lambda i: (0, 0)),
                  pl.BlockSpec((1, mod.shape[1], d), lambda i: (mrow(i), 0, 0))],
        out_specs=pl.BlockSpec((tm, d), lambda i: (i, 0)),
        out_shape=jax.ShapeDtypeStruct((m, d), out_dtype),
        compiler_params=_cparams("arbitrary"),
        name="norm_mod",
    )(xc, xl, w.reshape(1, d), mod)


def _inproj_kernel(a_ref, b_ref, wg_ref, z_ref, zg_ref):
    a = a_ref[...]
    z_ref[...] = _dot(a, b_ref[...]).astype(z_ref.dtype)

    @pl.when(pl.program_id(1) == 0)
    def _gates():
        zg_ref[...] = _dot(a, wg_ref[...])


def _inproj(h, w_main, w_gate):
    m, k = h.shape
    n = w_main.shape[1]
    tm = _divisor(m, 1024)
    tn = _divisor(n, 1024, LANES)
    return pl.pallas_call(
        _inproj_kernel,
        grid=(m // tm, n // tn),
        in_specs=[pl.BlockSpec((tm, k), lambda i, j: (i, 0)),
                  pl.BlockSpec((k, tn), lambda i, j: (0, j)),
                  pl.BlockSpec((k, LANES), lambda i, j: (0, 0))],
        out_specs=[pl.BlockSpec((tm, tn), lambda i, j: (i, j)),
                   pl.BlockSpec((tm, LANES), lambda i, j: (i, 0))],
        out_shape=[jax.ShapeDtypeStruct((m, n), BF16),
                   jax.ShapeDtypeStruct((m, LANES), F32)],
        compiler_params=_cparams("parallel", "arbitrary"),
        name="in_proj",
    )(h, w_main, w_gate)


CHUNKS_PER_STEP = 2


def _scan_schedule(n_ctx, t_ctx, n_lat, t_lat, reverse):
    rowblk, first, last, ctxb, latb, islat, posblk = [], [], [], [], [], [], []
    step_rows = CHUNK * CHUNKS_PER_STEP
    ns_ctx, ns_lat = t_ctx // step_rows, t_lat // step_rows
    for b in range(n_ctx):
        order = range(ns_ctx - 1, -1, -1) if reverse else range(ns_ctx)
        for pos, c in enumerate(order):
            rowblk.append(b * ns_ctx + c)
            first.append(int(pos == 0))
            last.append(int(pos == ns_ctx - 1))
            ctxb.append(b)
            latb.append(0)
            islat.append(0)
            posblk.append(ns_lat)
    base = n_ctx * ns_ctx
    for b in range(n_lat):
        order = range(ns_lat - 1, -1, -1) if reverse else range(ns_lat)
        for pos, c in enumerate(order):
            rowblk.append(base + b * ns_lat + c)
            first.append(int(pos == 0))
            last.append(int(pos == ns_lat - 1))
            ctxb.append(n_ctx - 1)
            latb.append(b)
            islat.append(1)
            posblk.append(c)
    tabs = (rowblk, first, last, ctxb, latb, islat, posblk)
    return tuple(jnp.asarray(np.asarray(t, np.int32)) for t in tabs)


def _chunk_rows(ci, reverse):
    cc = (CHUNKS_PER_STEP - 1 - ci) if reverse else ci
    return pl.ds(pl.multiple_of(cc * CHUNK, CHUNK), CHUNK)


def _mlstm_kernel(rowblk, first, last, ctxb, latb, islat, posblk,
                  q_ref, k_ref, v_ref, g_ref, gb_ref, c0_ref, n0_ref, m0_ref, *rest,
                  heads, dk, dv, reverse, combine):
    if combine:
        (hf_ref, o_ref, nw_ref, cf_ref, nf_ref, mf_ref,
         out_ref, cout_ref, nout_ref, mout_ref, c_s, n_s, m_s) = rest
    else:
        out_ref, cout_ref, nout_ref, mout_ref, c_s, n_s, m_s = rest
    w = pl.program_id(0)
    lat = islat[w] == 1

    @pl.when(first[w] == 1)
    def _init():
        c_s[...] = jnp.where(lat, c0_ref[0, 0], 0.0)
        n_s[...] = jnp.where(lat, n0_ref[0, 0], 0.0)
        m_s[...] = jnp.where(lat, m0_ref[0, 0], 0.0)

    L = CHUNK
    ti = lax.broadcasted_iota(I32, (L, L), 0)
    si = lax.broadcasted_iota(I32, (L, L), 1)
    mask = (si >= ti) if reverse else (si <= ti)
    mask_t = (ti >= si) if reverse else (ti <= si)
    scale = dk ** -0.5
    gate_row = 2 if reverse else 0
    hs_ = range(heads)

    def chunk(ci, carry):
        rows = _chunk_rows(ci, reverse)
        g = g_ref[rows, :] + gb_ref[...]
        gt = g.T
        lg = _log_sigmoid(g)
        lgt = _log_sigmoid(gt)

        st = []
        for h in hs_:
            ci_, cf_ = gate_row * heads + h, (gate_row + 1) * heads + h
            i_col, i_row = g[:, ci_:ci_ + 1], gt[ci_:ci_ + 1, :]
            lf_col, lf_row = lg[:, cf_:cf_ + 1], lgt[cf_:cf_ + 1, :]
            b_col = jnp.sum(jnp.where(mask, lf_row, 0.0), axis=1, keepdims=True)
            b_row = jnp.sum(jnp.where(mask_t, lf_col, 0.0), axis=0, keepdims=True)
            bl = jnp.sum(lf_row, axis=1, keepdims=True)
            m_old = m_s[h:h + 1, 0:1]
            dm = jnp.where(mask, b_col - b_row + i_row, NEG_BIG)
            inter = b_col + m_old
            mt = jnp.maximum(jnp.max(dm, axis=1, keepdims=True), inter)
            p = jnp.exp(dm - mt)
            wi = jnp.exp(inter - mt)
            g_col = bl - b_col + i_col
            g_row = bl - b_row + i_row
            m_new = jnp.maximum(bl + m_old, jnp.max(g_row, axis=1, keepdims=True))
            a_prev = jnp.exp(bl + m_old - m_new)
            a_col = jnp.exp(g_col - m_new) * scale
            st.append((p, wi, mt, m_new, a_prev, a_col))

        qbs = [q_ref[rows, h * dk:(h + 1) * dk] for h in hs_]
        kbs = [k_ref[rows, h * dk:(h + 1) * dk] for h in hs_]
        vbs = [v_ref[rows, h * dv:(h + 1) * dv] for h in hs_]
        ss = [_dot_nt(qbs[h], kbs[h]) * (scale * st[h][0]) for h in hs_]
        qcs = [_dot(qbs[h], c_s[h].astype(BF16)) for h in hs_]
        n_olds = [n_s[h:h + 1, :] for h in hs_]
        for h in hs_:
            p, wi, mt, m_new, a_prev, a_col = st[h]
            s_ = ss[h]
            num = _dot(s_.astype(BF16), vbs[h]) + wi * qcs[h]
            qn = jnp.sum(qbs[h].astype(F32) * n_olds[h], axis=1, keepdims=True)
            den = jnp.sum(s_, axis=1, keepdims=True) + wi * qn
            hval = num / jnp.maximum(jnp.abs(den), jnp.exp(-mt))
            if combine:
                hs = hval + hf_ref[rows, h * dv:(h + 1) * dv].astype(F32)
                y = hs * lax.rsqrt(jnp.mean(hs * hs, axis=-1, keepdims=True) + EPS)
                gate = o_ref[rows, h * dv:(h + 1) * dv].astype(F32)
                y = y * nw_ref[:, h * dv:(h + 1) * dv] * _sigmoid(gate)
                out_ref[rows, h * dv:(h + 1) * dv] = y.astype(out_ref.dtype)
            else:
                out_ref[rows, h * dv:(h + 1) * dv] = hval.astype(out_ref.dtype)
        for h in hs_:
            p, wi, mt, m_new, a_prev, a_col = st[h]
            ak = a_col * kbs[h].astype(F32)
            c_s[h] = a_prev * c_s[h] + _dot_tn(ak.astype(BF16), vbs[h])
            n_s[h:h + 1, :] = a_prev * n_olds[h] + jnp.sum(ak, axis=0, keepdims=True)
            m_s[h:h + 1, :] = jnp.broadcast_to(m_new, (1, LANES))
        return carry

    lax.fori_loop(0, CHUNKS_PER_STEP, chunk, 0)

    @pl.when(jnp.logical_and(last[w] == 1, jnp.logical_not(lat)))
    def _emit():
        if combine:
            cout_ref[0, 0] = cf_ref[0]
            nout_ref[0, 0] = nf_ref[0]
            mout_ref[0, 0] = mf_ref[0]
            cout_ref[0, 1] = c_s[...]
            nout_ref[0, 1] = n_s[...]
            mout_ref[0, 1] = m_s[...]
        else:
            cout_ref[0] = c_s[...]
            nout_ref[0] = n_s[...]
            mout_ref[0] = m_s[...]


def _mlstm_scan(z, zg, gbias, st_c, st_n, st_m, dims, reverse, fwd=None, norm_w=None):
    (n_ctx, t_ctx, n_lat, t_lat, heads, dk, dv) = dims
    combine = fwd is not None
    m = z.shape[0]
    d = 1 if reverse else 0
    step_rows = CHUNK * CHUNKS_PER_STEP
    tabs = _scan_schedule(n_ctx, t_ctx, n_lat, t_lat, reverse)
    qw, vw = heads * dk, heads * dv
    assert (2 * qw) % vw == 0
    v_blk = (2 * qw) // vw
    in_specs = [
        pl.BlockSpec((step_rows, qw), lambda w, rb, *_: (rb[w], 0)),
        pl.BlockSpec((step_rows, qw), lambda w, rb, *_: (rb[w], 1)),
        pl.BlockSpec((step_rows, vw), lambda w, rb, *_: (rb[w], v_blk)),
        pl.BlockSpec((step_rows, LANES), lambda w, rb, *_: (rb[w], 0)),
        pl.BlockSpec((1, LANES), lambda w, *_: (0, 0)),
        pl.BlockSpec((1, 1, heads, dk, dv), lambda w, rb, f, l, cb, lb, *_: (lb[w], d, 0, 0, 0)),
        pl.BlockSpec((1, 1, heads, dk), lambda w, rb, f, l, cb, lb, *_: (lb[w], d, 0, 0)),
        pl.BlockSpec((1, 1, heads, LANES), lambda w, rb, f, l, cb, lb, *_: (lb[w], d, 0, 0)),
    ]
    args = [z, z, z, zg, gbias, st_c, st_n, st_m]
    if combine:
        hf, cf, nf, mf = fwd
        in_specs += [
            pl.BlockSpec((step_rows, vw), lambda w, rb, *_: (rb[w], 0)),
            pl.BlockSpec((step_rows, vw), lambda w, rb, *_: (rb[w], v_blk + 1)),
            pl.BlockSpec((1, vw), lambda w, *_: (0, 0)),
            pl.BlockSpec((1, heads, dk, dv), lambda w, rb, f, l, cb, *_: (cb[w], 0, 0, 0)),
            pl.BlockSpec((1, heads, dk), lambda w, rb, f, l, cb, *_: (cb[w], 0, 0)),
            pl.BlockSpec((1, heads, LANES), lambda w, rb, f, l, cb, *_: (cb[w], 0, 0)),
        ]
        args += [hf, z, norm_w.reshape(1, vw), cf, nf, mf]
        nd = (2,)
        st_idx = lambda w, rb, f, l, cb, *_: (cb[w], 0, 0, 0)
        c_idx = lambda w, rb, f, l, cb, *_: (cb[w], 0, 0, 0, 0)
    else:
        nd = ()
        st_idx = lambda w, rb, f, l, cb, *_: (cb[w], 0, 0)
        c_idx = lambda w, rb, f, l, cb, *_: (cb[w], 0, 0, 0)
    out_specs = [
        pl.BlockSpec((step_rows, vw), lambda w, rb, *_: (rb[w], 0)),
        pl.BlockSpec((1,) + nd + (heads, dk, dv), c_idx),
        pl.BlockSpec((1,) + nd + (heads, dk), st_idx),
        pl.BlockSpec((1,) + nd + (heads, LANES), st_idx),
    ]
    out_shape = [
        jax.ShapeDtypeStruct((m, vw), BF16),
        jax.ShapeDtypeStruct((n_ctx,) + nd + (heads, dk, dv), F32),
        jax.ShapeDtypeStruct((n_ctx,) + nd + (heads, dk), F32),
        jax.ShapeDtypeStruct((n_ctx,) + nd + (heads, LANES), F32),
    ]
    return pl.pallas_call(
        functools.partial(_mlstm_kernel, heads=heads, dk=dk, dv=dv, reverse=reverse, combine=combine),
        grid_spec=pltpu.PrefetchScalarGridSpec(
            num_scalar_prefetch=len(tabs), grid=(m // step_rows,),
            in_specs=in_specs, out_specs=out_specs,
            scratch_shapes=[pltpu.VMEM((heads, dk, dv), F32),
                            pltpu.VMEM((heads, dk), F32),
                            pltpu.VMEM((heads, LANES), F32)]),
        out_shape=out_shape,
        compiler_params=_cparams("arbitrary"),
        name="mlstm_bwd" if reverse else "mlstm_fwd",
    )(*tabs, *args)


def _rope_partner(x, d):
    q, hlf = d // 4, d // 2
    return jnp.concatenate([x[:, q:hlf], x[:, :q], x[:, hlf + q:], x[:, hlf:hlf + q]], axis=1)


def _ret_kernel(rowblk, first, last, ctxb, latb, islat, posblk,
                q_ref, k_ref, v_ref, cos_ref, sin_ref, ld_ref, s0_ref, *rest,
                heads, dk, dv, reverse, combine):
    if combine:
        of_ref, g_ref, nw_ref, sf_ref, out_ref, sout_ref, s_s = rest
    else:
        out_ref, sout_ref, s_s = rest
    w = pl.program_id(0)
    lat = islat[w] == 1

    @pl.when(first[w] == 1)
    def _init():
        s_s[...] = jnp.where(lat, s0_ref[0, 0], 0.0)

    L = CHUNK
    ti = lax.broadcasted_iota(I32, (L, L), 0).astype(F32)
    si = lax.broadcasted_iota(I32, (L, L), 1).astype(F32)
    diff = (si - ti) if reverse else (ti - si)
    idx = lax.broadcasted_iota(I32, (L, 1), 0).astype(F32)
    q_pow = (L - idx) if reverse else (idx + 1.0)
    k_pow = idx if reverse else (L - 1.0 - idx)
    scale = dk ** -0.5
    drow = 1 if reverse else 0
    log_decay = -jnp.exp(ld_ref[drow:drow + 1, :])
    hs_ = range(heads)

    def chunk(ci, carry):
        rows = _chunk_rows(ci, reverse)
        cos_t, sin_t = cos_ref[rows, :], sin_ref[rows, :]
        qbs, kbs, kds, vbs = [], [], [], []
        for h in hs_:
            ld = log_decay[:, h:h + 1]
            q = q_ref[rows, h * dk:(h + 1) * dk].astype(F32)
            k = k_ref[rows, h * dk:(h + 1) * dk].astype(F32)
            q = q * cos_t + _rope_partner(q, dk) * sin_t
            k = k * cos_t + _rope_partner(k, dk) * sin_t
            qbs.append(q.astype(BF16))
            kbs.append(k.astype(BF16))
            kds.append((k * (scale * jnp.exp(k_pow * ld))).astype(BF16))
            vbs.append(v_ref[rows, h * dv:(h + 1) * dv])
        scores = [_dot_nt(qbs[h], kbs[h]) for h in hs_]
        inter = [_dot(qbs[h], s_s[h].astype(BF16)) for h in hs_]
        for h in hs_:
            ld = log_decay[:, h:h + 1]
            intra = jnp.where(diff >= 0.0, jnp.exp(jnp.maximum(diff, 0.0) * ld), 0.0)
            a = scores[h] * (scale * intra)
            o = _dot(a.astype(BF16), vbs[h]) + inter[h] * jnp.exp(q_pow * ld)
            if combine:
                hs = o + of_ref[rows, h * dv:(h + 1) * dv].astype(F32)
                y = hs * lax.rsqrt(jnp.mean(hs * hs, axis=-1, keepdims=True) + EPS)
                gate = g_ref[rows, h * dv:(h + 1) * dv].astype(F32)
                y = y * nw_ref[:, h * dv:(h + 1) * dv] * (gate * _sigmoid(gate))
                out_ref[rows, h * dv:(h + 1) * dv] = y.astype(out_ref.dtype)
            else:
                out_ref[rows, h * dv:(h + 1) * dv] = o.astype(out_ref.dtype)
        for h in hs_:
            ld = log_decay[:, h:h + 1]
            s_s[h] = jnp.exp(float(L) * ld) * s_s[h] + _dot_tn(kds[h], vbs[h])
        return carry

    lax.fori_loop(0, CHUNKS_PER_STEP, chunk, 0)

    @pl.when(jnp.logical_and(last[w] == 1, jnp.logical_not(lat)))
    def _emit():
        if combine:
            sout_ref[0, 0] = sf_ref[0]
            sout_ref[0, 1] = s_s[...]
        else:
            sout_ref[0] = s_s[...]


def _ret_scan(z, cos_tab, sin_tab, ld, st_s, dims, col0, reverse, fwd=None, norm_w=None):
    (n_ctx, t_ctx, n_lat, t_lat, heads, dk, dv) = dims
    combine = fwd is not None
    m = z.shape[0]
    d = 1 if reverse else 0
    step_rows = CHUNK * CHUNKS_PER_STEP
    tabs = _scan_schedule(n_ctx, t_ctx, n_lat, t_lat, reverse)
    qw, vw = heads * dk, heads * dv
    assert qw == vw and col0 % qw == 0
    b0 = col0 // qw
    in_specs = [
        pl.BlockSpec((step_rows, qw), lambda w, rb, *_: (rb[w], b0)),
        pl.BlockSpec((step_rows, qw), lambda w, rb, *_: (rb[w], b0 + 1)),
        pl.BlockSpec((step_rows, vw), lambda w, rb, *_: (rb[w], b0 + 2)),
        pl.BlockSpec((step_rows, dk), lambda w, rb, f, l, cb, lb, il, pb: (pb[w], 0)),
        pl.BlockSpec((step_rows, dk), lambda w, rb, f, l, cb, lb, il, pb: (pb[w], 0)),
        pl.BlockSpec((SUBLANES, LANES), lambda w, *_: (0, 0)),
        pl.BlockSpec((1, 1, heads, dk, dv), lambda w, rb, f, l, cb, lb, *_: (lb[w], d, 0, 0, 0)),
    ]
    args = [z, z, z, cos_tab, sin_tab, ld, st_s]
    if combine:
        of, sf = fwd
        in_specs += [
            pl.BlockSpec((step_rows, vw), lambda w, rb, *_: (rb[w], 0)),
            pl.BlockSpec((step_rows, vw), lambda w, rb, *_: (rb[w], b0 + 3)),
            pl.BlockSpec((1, vw), lambda w, *_: (0, 0)),
            pl.BlockSpec((1, heads, dk, dv), lambda w, rb, f, l, cb, *_: (cb[w], 0, 0, 0)),
        ]
        args += [of, z, norm_w.reshape(1, vw), sf]
        s_spec = pl.BlockSpec((1, 2, heads, dk, dv), lambda w, rb, f, l, cb, *_: (cb[w], 0, 0, 0, 0))
        s_shape = jax.ShapeDtypeStruct((n_ctx, 2, heads, dk, dv), F32)
    else:
        s_spec = pl.BlockSpec((1, heads, dk, dv), lambda w, rb, f, l, cb, *_: (cb[w], 0, 0, 0))
        s_shape = jax.ShapeDtypeStruct((n_ctx, heads, dk, dv), F32)
    return pl.pallas_call(
        functools.partial(_ret_kernel, heads=heads, dk=dk, dv=dv, reverse=reverse, combine=combine),
        grid_spec=pltpu.PrefetchScalarGridSpec(
            num_scalar_prefetch=len(tabs), grid=(m // step_rows,),
            in_specs=in_specs,
            out_specs=[pl.BlockSpec((step_rows, vw), lambda w, rb, *_: (rb[w], 0)), s_spec],
            scratch_shapes=[pltpu.VMEM((heads, dk, dv), F32)]),
        out_shape=[jax.ShapeDtypeStruct((m, vw), BF16), s_shape],
        compiler_params=_cparams("arbitrary"),
        name="ret_bwd" if reverse else "ret_fwd",
    )(*tabs, *args)


def _rope_tables(t_lat, d):
    quarter = d // 4
    rows = t_lat // GRID_W
    row = jnp.repeat(jnp.arange(rows, dtype=F32), GRID_W)
    col = jnp.tile(jnp.arange(GRID_W, dtype=F32), rows)
    inv = ROPE_BASE ** (-jnp.arange(quarter, dtype=F32) / quarter)
    ar = row[:, None] * inv[None, :]
    ac = col[:, None] * inv[None, :]
    cos_t = jnp.concatenate([jnp.cos(ar), jnp.cos(ar), jnp.cos(ac), jnp.cos(ac)], axis=1)
    sin_t = jnp.concatenate([-jnp.sin(ar), jnp.sin(ar), -jnp.sin(ac), jnp.sin(ac)], axis=1)
    pad_rows = CHUNK * CHUNKS_PER_STEP
    cos_t = jnp.concatenate([cos_t, jnp.ones((pad_rows, d), F32)], axis=0)
    sin_t = jnp.concatenate([sin_t, jnp.zeros((pad_rows, d), F32)], axis=0)
    return cos_t, sin_t


def _outproj_kernel(a1_ref, a2_ref, b1_ref, b2_ref, xc_ref, xl_ref, mod_ref, o_ref, *, gate_idx, n_ctx_tiles):
    acc = _dot(a1_ref[...], b1_ref[...]) + _dot(a2_ref[...], b2_ref[...])
    upd = mod_ref[0, gate_idx:gate_idx + 1, :] * acc
    i = pl.program_id(0)

    @pl.when(i < n_ctx_tiles)
    def _ctx():
        o_ref[...] = xc_ref[...] + upd

    @pl.when(i >= n_ctx_tiles)
    def _lat():
        o_ref[...] = xl_ref[...] + upd


def _outproj(a1, a2, w_out, xc, xl, mod, t_lat, gate_idx):
    m, k1 = a1.shape
    k2 = a2.shape[1]
    n = w_out.shape[1]
    n_ctx_rows = xc.shape[0]
    assert k1 == k2
    tm = _divisor(math.gcd(n_ctx_rows, t_lat), 1024)
    tn = _divisor(n, 512, LANES)
    nct = n_ctx_rows // tm
    mrow = functools.partial(_mod_row, tm=tm, n_ctx_rows=n_ctx_rows, t_lat=t_lat)
    return pl.pallas_call(
        functools.partial(_outproj_kernel, gate_idx=gate_idx, n_ctx_tiles=nct),
        grid=(m // tm, n // tn),
        in_specs=[pl.BlockSpec((tm, k1), lambda i, j: (i, 0)),
                  pl.BlockSpec((tm, k2), lambda i, j: (i, 0)),
                  pl.BlockSpec((k1, tn), lambda i, j: (0, j)),
                  pl.BlockSpec((k2, tn), lambda i, j: (1, j)),
                  pl.BlockSpec((tm, tn), lambda i, j: (jnp.minimum(i, nct - 1), jnp.where(i < nct, j, 0))),
                  pl.BlockSpec((tm, tn), lambda i, j: (jnp.maximum(i - nct, 0), jnp.where(i >= nct, j, 0))),
                  pl.BlockSpec((1, mod.shape[1], tn), lambda i, j: (mrow(i), 0, j))],
        out_specs=pl.BlockSpec((tm, tn), lambda i, j: (i, j)),
        out_shape=jax.ShapeDtypeStruct((m, n), F32),
        compiler_params=_cparams("arbitrary", "arbitrary"),
        name="out_proj",
    )(a1, a2, w_out, w_out, xc, xl, mod)


def _router_kernel(x_ref, w_ref, mod_ref, wr_ref, br_ref, h_ref, eid_ref, wt_ref,
                   *, shift_idx, scale_idx, n_groups, per_group):
    x = x_ref[...]
    y = x * lax.rsqrt(jnp.mean(x * x, axis=-1, keepdims=True) + EPS) * w_ref[...]
    y = y * (1.0 + mod_ref[0, scale_idx:scale_idx + 1, :]) + mod_ref[0, shift_idx:shift_idx + 1, :]
    _store_token_major(h_ref, _pack_bf16_pair(y))
    y_hi, y_lo = _split_bf16(y)
    w_hi, w_lo = _split_bf16(wr_ref[...])
    logits = _dot(y_hi, w_hi) + _dot(y_lo, w_hi) + _dot(y_hi, w_lo) + br_ref[...]

    n_exp = n_groups * per_group
    lane = lax.broadcasted_iota(I32, logits.shape, 1)
    gmask = lane < n_groups
    gl = jnp.where(gmask, logits, NEG_BIG)
    gmax = jnp.max(gl, axis=1, keepdims=True)
    gsum = jnp.sum(jnp.where(gmask, jnp.exp(gl - gmax), 0.0), axis=1, keepdims=True)
    g_w = 1.0 / gsum
    g_idx = jnp.min(jnp.where(gl == gmax, lane, LANES), axis=1, keepdims=True)

    in_group = jnp.logical_and(lane >= n_groups + g_idx * per_group,
                               lane < n_groups + (g_idx + 1) * per_group)
    in_group = jnp.logical_and(in_group, lane < n_groups + n_exp)
    el = jnp.where(in_group, logits, NEG_BIG)
    m1 = jnp.max(el, axis=1, keepdims=True)
    i1 = jnp.min(jnp.where(el == m1, lane, LANES), axis=1, keepdims=True)
    el2 = jnp.where(lane == i1, NEG_BIG, el)
    m2 = jnp.max(el2, axis=1, keepdims=True)
    i2 = jnp.min(jnp.where(el2 == m2, lane, LANES), axis=1, keepdims=True)
    e2 = jnp.exp(m2 - m1)
    p1 = 1.0 / (1.0 + e2)
    p2 = e2 * p1
    eid_ref[...] = jnp.where(lane == 0, i1 - n_groups, jnp.where(lane == 1, i2 - n_groups, 0))
    wt_ref[...] = jnp.where(lane == 0, g_w * p1, jnp.where(lane == 1, g_w * p2, 0.0))


def _router(x, w, mod, w_r, b_r, n_ctx_rows, t_lat, shift_idx, scale_idx, n_groups, per_group):
    m, d = x.shape
    tm = _divisor(math.gcd(n_ctx_rows, t_lat), 256)
    mrow = functools.partial(_mod_row, tm=tm, n_ctx_rows=n_ctx_rows, t_lat=t_lat)
    return pl.pallas_call(
        functools.partial(_router_kernel, shift_idx=shift_idx, scale_idx=scale_idx,
                          n_groups=n_groups, per_group=per_group),
        grid=(m // tm,),
        in_specs=[pl.BlockSpec((tm, d), lambda i: (i, 0)),
                  pl.BlockSpec((1, d), lambda i: (0, 0)),
                  pl.BlockSpec((1, mod.shape[1], d), lambda i: (mrow(i), 0, 0)),
                  pl.BlockSpec((d, LANES), lambda i: (0, 0)),
                  pl.BlockSpec((1, LANES), lambda i: (0, 0))],
        out_specs=[pl.BlockSpec((tm, SUBLANES, d // (2 * SUBLANES)), lambda i: (i, 0, 0)),
                   pl.BlockSpec((tm, LANES), lambda i: (i, 0)),
                   pl.BlockSpec((tm, LANES), lambda i: (i, 0))],
        out_shape=[jax.ShapeDtypeStruct((m, SUBLANES, d // (2 * SUBLANES)), U32),
                   jax.ShapeDtypeStruct((m, LANES), I32),
                   jax.ShapeDtypeStruct((m, LANES), F32)],
        compiler_params=_cparams("parallel"),
        name="router",
    )(x, w.reshape(1, d), mod, w_r, b_r)


GATHER_UNROLL = 8


def _row_copy(src_hbm, dst_buf, sem, src_row, slot, dst_row):
    return pltpu.make_async_copy(src_hbm.at[src_row], dst_buf.at[slot, dst_row], sem.at[slot])


def _start_row_gather(tok_fn, src_hbm, buf, sem, slot, n_idx):
    def body(jj, carry):
        for u in range(GATHER_UNROLL):
            j = jj * GATHER_UNROLL + u
            _row_copy(src_hbm, buf, sem, tok_fn(j), slot, j).start(priority=u % 2)
        return carry

    lax.fori_loop(0, n_idx // GATHER_UNROLL, body, 0)


def _wait_row_gather(src_hbm, buf, sem, slot, n_idx):
    def body(j, carry):
        _row_copy(src_hbm, buf, sem, 0, slot, j).wait()
        return carry

    lax.fori_loop(0, n_idx, body, 0, unroll=GATHER_UNROLL)


def _ring_step(cur_fn, nxt_fn, src_hbm, buf, sem, n_idx, consume):
    i = pl.program_id(0)

    @pl.when(i == 0)
    def _prime():
        _start_row_gather(cur_fn, src_hbm, buf, sem, 0, n_idx)

    for slot in range(2):
        @pl.when(i % 2 == slot)
        def _work(slot=slot):
            @pl.when(i + 1 < pl.num_programs(0))
            def _prefetch():
                _start_row_gather(nxt_fn, src_hbm, buf, sem, 1 - slot, n_idx)

            _wait_row_gather(src_hbm, buf, sem, slot, n_idx)
            consume(slot)


def _scatter_rows_kernel(cnt, dst_ref, h_ref, xs_hbm, zero, sem, *, rows, blk_rows):
    n_blk = cnt.shape[0]

    def pad_copy(b, r):
        return pltpu.make_async_copy(zero.at[0], xs_hbm.at[b * blk_rows + r], sem.at[1])

    def for_each_pad(fn):
        def blk(b, carry):
            def row(r, c2):
                fn(pad_copy(b, r))
                return c2
            return lax.fori_loop(cnt[b], blk_rows, row, carry)
        lax.fori_loop(0, n_blk, blk, 0)

    @pl.when(pl.program_id(0) == 0)
    def _pad():
        zero[...] = jnp.zeros(zero.shape, zero.dtype)
        for_each_pad(lambda cp: cp.start())
        for_each_pad(lambda cp: cp.wait())

    def row_copy(j, r):
        return pltpu.make_async_copy(h_ref.at[r], xs_hbm.at[dst_ref[0, 0, j]], sem.at[0])

    def issue(rr, carry):
        for u in range(GATHER_UNROLL // TOP_K):
            r = rr * (GATHER_UNROLL // TOP_K) + u
            for kk in range(TOP_K):
                row_copy(r * TOP_K + kk, r).start(priority=kk % 2)
        return carry

    lax.fori_loop(0, rows * TOP_K // GATHER_UNROLL, issue, 0)

    def drain(r, carry):
        for kk in range(TOP_K):
            row_copy(r * TOP_K + kk, r).wait()
        return carry

    lax.fori_loop(0, rows, drain, 0, unroll=GATHER_UNROLL // TOP_K)


def _scatter_rows(src, dest, blk_cnt, blk_rows):
    t = src.shape[0]
    rows = _divisor(t, 256)
    n_slots = blk_cnt.shape[0] * blk_rows
    return pl.pallas_call(
        functools.partial(_scatter_rows_kernel, rows=rows, blk_rows=blk_rows),
        grid_spec=pltpu.PrefetchScalarGridSpec(
            num_scalar_prefetch=1, grid=(t // rows,),
            in_specs=[pl.BlockSpec((1, 1, TOP_K * rows), lambda i, *_: (i, 0, 0), memory_space=pltpu.SMEM),
                      pl.BlockSpec((rows,) + src.shape[1:], lambda i, *_: (i, 0, 0))],
            out_specs=pl.BlockSpec(memory_space=pl.ANY),
            scratch_shapes=[pltpu.VMEM((1,) + src.shape[1:], src.dtype), pltpu.SemaphoreType.DMA((2,))]),
        out_shape=jax.ShapeDtypeStruct((n_slots,) + src.shape[1:], src.dtype),
        compiler_params=_cparams("arbitrary"),
        name="moe_dispatch",
    )(blk_cnt, dest.reshape(t // rows, 1, TOP_K * rows), src)


def _moe_up_kernel(xblk, eid, wjt, oblk, ojt, fst, valid, ring, has_nxt, nxt_e, nxt_j,
                   x_ref, wg_hbm, wu_hbm, h_ref, wbuf, wg_s, wu_s, flat, sem, *, tn):
    w = pl.program_id(0)

    def wcopy(e, j, slot, which):
        src = wu_hbm if which else wg_hbm
        cols = pl.ds(pl.multiple_of(j * tn, tn), tn)
        return pltpu.make_async_copy(src.at[e, :, cols], wbuf.at[slot, which], sem.at[slot, which])

    @pl.when(w == 0)
    def _prime():
        for which in range(2):
            wcopy(eid[0], wjt[0], ring[0], which).start()

    @pl.when(valid[w] == 1)
    def _go():
        @pl.when(fst[w] == 1)
        def _swap():
            slot = ring[w]

            @pl.when(has_nxt[w] == 1)
            def _request():
                for which in range(2):
                    wcopy(nxt_e[w], nxt_j[w], 1 - slot, which).start()

            for which in range(2):
                wcopy(eid[w], wjt[w], slot, which).wait()
            wg_s[...] = wbuf[slot, 0].astype(BF16)
            wu_s[...] = wbuf[slot, 1].astype(BF16)

        wd = x_ref.shape[-1]
        for sl in range(SUBLANES):
            flat[:, sl * wd:(sl + 1) * wd] = x_ref[:, sl, :]
        hi, lo = _unpack_bf16_pair(flat[...])
        xh, xl = hi.astype(BF16), lo.astype(BF16)
        n = xh.shape[1]
        a = _dot(xh, wg_s[:n, :]) + _dot(xl, wg_s[n:, :])
        b = _dot(xh, wu_s[:n, :]) + _dot(xl, wu_s[n:, :])
        h_ref[...] = (a * _sigmoid(a) * b).astype(h_ref.dtype)

    @pl.when(valid[w] == 0)
    def _pad():
        h_ref[...] = jnp.zeros(h_ref.shape, h_ref.dtype)


def _moe_up(xs, w_gate, w_up, tabs, n_work, tn):
    n_slots = xs.shape[0]
    d = w_gate.shape[1]
    f = w_gate.shape[2]
    return pl.pallas_call(
        functools.partial(_moe_up_kernel, tn=tn),
        grid_spec=pltpu.PrefetchScalarGridSpec(
            num_scalar_prefetch=len(tabs), grid=(n_work,),
            in_specs=[pl.BlockSpec((MOE_TILE,) + xs.shape[1:], lambda w, xb, *_: (xb[w], 0, 0)),
                      pl.BlockSpec(memory_space=pl.ANY),
                      pl.BlockSpec(memory_space=pl.ANY)],
            out_specs=pl.BlockSpec((MOE_TILE, tn), lambda w, xb, e, wj, ob, oj, *_: (ob[w], oj[w])),
            scratch_shapes=[pltpu.VMEM((2, 2, d, tn), w_gate.dtype),
                            pltpu.VMEM((d, tn), BF16), pltpu.VMEM((d, tn), BF16),
                            pltpu.VMEM((MOE_TILE, xs.shape[1] * xs.shape[2]), xs.dtype),
                            pltpu.SemaphoreType.DMA((2, 2))]),
        out_shape=jax.ShapeDtypeStruct((n_slots, f), BF16),
        compiler_params=_cparams("arbitrary"),
        name="moe_up",
    )(*tabs, xs, w_gate, w_up)


def _moe_down_kernel(eid, fst, valid, ring, has_nxt, nxt_e, h_ref, wd_hbm, y_ref, wbuf, wd_s, sem):
    w = pl.program_id(0)

    def wcopy(e, slot):
        return pltpu.make_async_copy(wd_hbm.at[e], wbuf.at[slot], sem.at[slot])

    @pl.when(w == 0)
    def _prime():
        wcopy(eid[0], ring[0]).start()

    @pl.when(valid[w] == 1)
    def _go():
        @pl.when(fst[w] == 1)
        def _swap():
            slot = ring[w]

            @pl.when(has_nxt[w] == 1)
            def _request():
                wcopy(nxt_e[w], 1 - slot).start()

            wcopy(eid[w], slot).wait()
            wd_s[...] = wbuf[slot].astype(BF16)

        _store_token_major(y_ref, _pack_bf16_pair(_dot(h_ref[...], wd_s[...])))

    @pl.when(valid[w] == 0)
    def _pad():
        y_ref[...] = jnp.zeros(y_ref.shape, y_ref.dtype)


def _moe_down(hs, w_down, tabs):
    n_slots, f = hs.shape
    d = w_down.shape[2]
    nblk = n_slots // MOE_TILE
    return pl.pallas_call(
        _moe_down_kernel,
        grid_spec=pltpu.PrefetchScalarGridSpec(
            num_scalar_prefetch=len(tabs), grid=(nblk,),
            in_specs=[pl.BlockSpec((MOE_TILE, f), lambda w, *_: (w, 0)),
                      pl.BlockSpec(memory_space=pl.ANY)],
            out_specs=pl.BlockSpec((MOE_TILE, SUBLANES, d // (2 * SUBLANES)), lambda w, *_: (w, 0, 0)),
            scratch_shapes=[pltpu.VMEM((2, f, d), w_down.dtype), pltpu.VMEM((f, d), BF16),
                            pltpu.SemaphoreType.DMA((2,))]),
        out_shape=jax.ShapeDtypeStruct((n_slots, SUBLANES, d // (2 * SUBLANES)), U32),
        compiler_params=_cparams("arbitrary"),
        name="moe_down",
    )(*tabs, hs, w_down)


def _final_kernel(dst_ref, nxt_ref, x_ref, wt_ref, mod_ref, fw_ref, ys_hbm, o_ref, buf, sem, *, rows, gate_idx):
    def consume(slot):
        wt = wt_ref[...]
        moe = None
        for kk in range(TOP_K):
            hi, lo = _unpack_bf16_pair(_load_token_major(buf, slot, kk * rows, rows))
            term = wt[:, kk:kk + 1] * jnp.concatenate([hi, lo], axis=1)
            moe = term if moe is None else moe + term
        x = x_ref[...] + mod_ref[0, gate_idx:gate_idx + 1, :] * moe
        o_ref[...] = x * lax.rsqrt(jnp.mean(x * x, axis=-1, keepdims=True) + EPS) * fw_ref[...]

    _ring_step(lambda j: dst_ref[0, 0, j], lambda j: nxt_ref[0, 0, j], ys_hbm, buf, sem, TOP_K * rows, consume)


def _final(x1, wt, dest, ys, mod, final_w, row0, n_rows, n_ctx_rows, t_lat, gate_idx):
    d = x1.shape[1]
    tm = _divisor(math.gcd(n_ctx_rows, t_lat), 128)
    t0 = row0 // tm
    nt = n_rows // tm
    mrow = functools.partial(_mod_row, tm=tm, n_ctx_rows=n_ctx_rows, t_lat=t_lat)
    dest3 = dest.reshape(-1, tm, TOP_K).transpose(0, 2, 1).reshape(-1, 1, TOP_K * tm)
    return pl.pallas_call(
        functools.partial(_final_kernel, rows=tm, gate_idx=gate_idx),
        grid=(nt,),
        in_specs=[pl.BlockSpec((1, 1, TOP_K * tm), lambda i: (t0 + i, 0, 0), memory_space=pltpu.SMEM),
                  pl.BlockSpec((1, 1, TOP_K * tm), lambda i: (t0 + jnp.minimum(i + 1, nt - 1), 0, 0),
                               memory_space=pltpu.SMEM),
                  pl.BlockSpec((tm, d), lambda i: (t0 + i, 0)),
                  pl.BlockSpec((tm, LANES), lambda i: (t0 + i, 0)),
                  pl.BlockSpec((1, mod.shape[1], d), lambda i: (mrow(t0 + i), 0, 0)),
                  pl.BlockSpec((1, d), lambda i: (0, 0)),
                  pl.BlockSpec(memory_space=pl.ANY)],
        out_specs=pl.BlockSpec((tm, d), lambda i: (i, 0)),
        out_shape=jax.ShapeDtypeStruct((n_rows, d), F32),
        scratch_shapes=[pltpu.VMEM((2, TOP_K * tm) + ys.shape[1:], ys.dtype), pltpu.SemaphoreType.DMA((2,))],
        compiler_params=_cparams("arbitrary"),
        name="moe_combine_final",
    )(dest3, dest3, x1, wt, mod, final_w.reshape(1, d), ys)


def _count_le(sorted_ends, idx):
    return jnp.sum((sorted_ends[None, :] <= idx[:, None]).astype(I32), axis=1)


def _moe_plan(eid, n_exp, n_jt):
    t = eid.shape[0]
    n_assign = t * TOP_K
    eflat = eid.reshape(n_assign)
    onehot = (eflat[:, None] == jnp.arange(n_exp, dtype=I32)[None, :]).astype(I32)
    csum = jnp.cumsum(onehot, axis=0)
    counts = csum[-1]
    nb = (counts + MOE_TILE - 1) // MOE_TILE
    blk_end = jnp.cumsum(nb)
    blk_start = blk_end - nb
    n_blocks = blk_end[-1]
    pad_start = blk_start * MOE_TILE
    dest = jnp.sum(onehot * (csum - 1 + pad_start[None, :]), axis=1)

    assert n_assign % MOE_TILE == 0
    nblk_max = n_assign // MOE_TILE + n_exp
    b_idx = jnp.arange(nblk_max, dtype=I32)
    blk_e = jnp.minimum(_count_le(blk_end, b_idx), n_exp - 1)
    blk_off = (b_idx - blk_start[blk_e]) * MOE_TILE
    blk_cnt = jnp.where(b_idx < n_blocks, jnp.clip(counts[blk_e] - blk_off, 0, MOE_TILE), 0).astype(I32)

    d_valid = (b_idx < n_blocks).astype(I32)
    last_e = blk_e[jnp.maximum(n_blocks - 1, 0)]
    d_eid = jnp.where(d_valid == 1, blk_e, last_e).astype(I32)
    d_first = jnp.logical_and(d_valid == 1, b_idx == blk_start[d_eid]).astype(I32)
    has_blk = (nb > 0).astype(I32)
    e_ord = jnp.cumsum(has_blk) - 1
    d_ring = (e_ord[d_eid] % 2).astype(I32)
    nxt_b = jnp.minimum(blk_end[d_eid], nblk_max - 1)
    d_has_nxt = (blk_end[d_eid] < n_blocks).astype(I32)
    d_nxt_e = blk_e[nxt_b].astype(I32)

    n_work = n_jt * nblk_max
    w_idx = jnp.arange(n_work, dtype=I32)
    per_e = n_jt * nb
    w_end = jnp.cumsum(per_e)
    w_valid = w_idx < w_end[-1]
    we = jnp.minimum(_count_le(w_end, w_idx), n_exp - 1)
    r = w_idx - (w_end[we] - per_e[we])
    nbe = jnp.maximum(nb[we], 1)
    u_jt = r // nbe
    u_t = r - u_jt * nbe
    u_blk = blk_start[we] + u_t
    last_w = jnp.maximum(w_end[-1] - 1, 0)
    spare = w_idx - w_end[-1]
    u_xblk = jnp.where(w_valid, u_blk, u_blk[last_w]).astype(I32)
    u_eid = jnp.where(w_valid, we, we[last_w]).astype(I32)
    u_wjt = jnp.where(w_valid, u_jt, u_jt[last_w]).astype(I32)
    u_oblk = jnp.where(w_valid, u_blk, n_blocks + spare // n_jt).astype(I32)
    u_ojt = jnp.where(w_valid, u_jt, spare % n_jt).astype(I32)
    u_first = jnp.logical_and(w_valid, u_t == 0).astype(I32)
    g_ord = jnp.cumsum(u_first) - 1
    u_ring = (g_ord % 2).astype(I32)
    nxt_w = jnp.minimum(w_idx - u_t + nbe, n_work - 1)
    u_has_nxt = jnp.logical_and(w_valid, w_idx - u_t + nbe < w_end[-1]).astype(I32)
    u_nxt_e = we[nxt_w].astype(I32)
    u_nxt_j = u_jt[nxt_w].astype(I32)
    up_tabs = (u_xblk, u_eid, u_wjt, u_oblk, u_ojt, u_first, w_valid.astype(I32), u_ring, u_has_nxt, u_nxt_e, u_nxt_j)
    down_tabs = (d_eid, d_first, d_valid, d_ring, d_has_nxt, d_nxt_e)
    return dest.astype(I32), blk_cnt, up_tabs, down_tabs, n_work


def kernel(x_prompt, x_sample, state_mlstm_C, state_mlstm_n, state_mlstm_m, state_ret_S, c, c_ctx,
           norm1_w, norm2_w, w_ada, b_ada, w_in, b_mgates, mlstm_norm_w, ret_norm_w, ret_log_decay, w_out,
           w_router_group, b_router_group, w_router_expert, b_router_expert, w_exp_gate, w_exp_up,
           w_exp_down, final_norm_w):
    n_ctx, t_ctx, d_model = x_prompt.shape
    n_lat, t_lat, _ = x_sample.shape
    depth = norm1_w.shape[0]
    assert depth == 1, "single-layer trunk"
    m_heads, m_dk, m_dv = state_mlstm_C.shape[3:]
    r_heads, r_dk, r_dv = state_ret_S.shape[3:]
    n_groups = w_router_group.shape[2]
    n_exp = w_router_expert.shape[2]
    per_group = n_exp // n_groups
    d_exp = w_exp_gate.shape[3]
    n_gates = N_GATE_ROWS * m_heads
    n_ctx_rows = n_ctx * t_ctx
    step_rows = CHUNK * CHUNKS_PER_STEP
    assert t_ctx % step_rows == 0 and t_lat % step_rows == 0 and n_gates <= LANES
    assert n_groups + n_exp <= LANES

    cvec = jnp.concatenate([c_ctx[None, :], c, jnp.zeros((SUBLANES - 1 - n_lat, d_model), F32)], axis=0)
    mod = _ada_mod(cvec, w_ada[0], b_ada[0]).reshape(SUBLANES, 6, d_model)

    xc = x_prompt.reshape(n_ctx_rows, d_model)
    xl = x_sample.reshape(n_lat * t_lat, d_model)
    h = _norm_mod(xc, xl, norm1_w[0], mod, t_lat, 0, 1, BF16)

    g0 = 2 * m_heads * m_dk + 2 * m_heads * m_dv
    w_main = jnp.concatenate([w_in[0][:, :g0].astype(BF16), w_in[0][:, g0 + n_gates:].astype(BF16)], axis=1)
    w_gate = jnp.pad(w_in[0][:, g0:g0 + n_gates], ((0, 0), (0, LANES - n_gates))).astype(BF16)
    z, zg = _inproj(h, w_main, w_gate)

    gbias = jnp.pad(b_mgates[0].reshape(1, n_gates), ((0, 0), (0, LANES - n_gates)))
    st_m = jnp.broadcast_to(state_mlstm_m[:, 0][..., None], (n_lat, 2, m_heads, LANES))
    mdims = (n_ctx, t_ctx, n_lat, t_lat, m_heads, m_dk, m_dv)
    fwd_m = _mlstm_scan(z, zg, gbias, state_mlstm_C[:, 0], state_mlstm_n[:, 0], st_m, mdims, False)
    mix_m, new_c, new_n, new_m = _mlstm_scan(z, zg, gbias, state_mlstm_C[:, 0], state_mlstm_n[:, 0], st_m, mdims,
                                             True, fwd=fwd_m, norm_w=mlstm_norm_w[0])

    cos_tab, sin_tab = _rope_tables(t_lat, r_dk)
    ld = jnp.pad(ret_log_decay[0], ((0, SUBLANES - 2), (0, LANES - r_heads)))
    rdims = (n_ctx, t_ctx, n_lat, t_lat, r_heads, r_dk, r_dv)
    fwd_r = _ret_scan(z, cos_tab, sin_tab, ld, state_ret_S[:, 0], rdims, g0, False)
    mix_r, new_s = _ret_scan(z, cos_tab, sin_tab, ld, state_ret_S[:, 0], rdims, g0, True,
                             fwd=fwd_r, norm_w=ret_norm_w[0])

    x1 = _outproj(mix_m, mix_r, w_out[0].astype(BF16), xc, xl, mod, t_lat, 2)

    w_r = jnp.pad(jnp.concatenate([w_router_group[0], w_router_expert[0]], axis=1),
                  ((0, 0), (0, LANES - n_groups - n_exp)))
    b_r = jnp.pad(jnp.concatenate([b_router_group[0], b_router_expert[0]])[None, :],
                  ((0, 0), (0, LANES - n_groups - n_exp)))
    h2, eid, wt = _router(x1, norm2_w[0], mod, w_r, b_r, n_ctx_rows, t_lat, 3, 4, n_groups, per_group)

    tn_up = _divisor(d_exp, 512, LANES)
    dest, blk_cnt, up_tabs, down_tabs, n_work = _moe_plan(eid[:, :TOP_K], n_exp, d_exp // tn_up)
    xs = _scatter_rows(h2, dest, blk_cnt, MOE_TILE)
    hs = _moe_up(xs, w_exp_gate[0], w_exp_up[0], up_tabs, n_work, tn_up)
    ys = _moe_down(hs, w_exp_down[0], down_tabs)

    y_ctx = _final(x1, wt, dest, ys, mod, final_norm_w, 0, n_ctx_rows, n_ctx_rows, t_lat, 5)
    y_lat = _final(x1, wt, dest, ys, mod, final_norm_w, n_ctx_rows, n_lat * t_lat, n_ctx_rows, t_lat, 5)

    return (y_ctx.reshape(n_ctx, t_ctx, d_model), y_lat.reshape(n_lat, t_lat, d_model),
            new_c[:, None], new_n[:, None], new_m[:, None, :, :, 0], new_s[:, None])
```

```python
import functools
import math

import numpy as np
import jax
import jax.numpy as jnp
from jax import lax
from jax.experimental import pallas as pl
from jax.experimental.pallas import tpu as pltpu

F32 = jnp.float32
BF16 = jnp.bfloat16
I32 = jnp.int32
U32 = jnp.uint32

CHUNK = 128
GRID_W = 64
ROPE_BASE = 10000.0
EPS = 1e-6
TOP_K = 2
N_GATE_ROWS = 4

LANES = 128
SUBLANES = 8
VMEM_LIMIT_BYTES = 56 * 1024 * 1024
NEG_BIG = -1e30
MOE_TILE = 256


def _divisor(n, pref, mult=SUBLANES):
    if n <= pref:
        return n
    d = (pref // mult) * mult
    while d > mult and n % d:
        d -= mult
    assert n % d == 0, (n, pref, mult)
    return d


def _cparams(*sem):
    return pltpu.CompilerParams(dimension_semantics=sem, vmem_limit_bytes=VMEM_LIMIT_BYTES)


def _dot(a, b):
    return jnp.dot(a, b, preferred_element_type=F32)


def _dot_nt(a, b):
    return lax.dot_general(a, b, (((1,), (1,)), ((), ())), preferred_element_type=F32)


def _dot_tn(a, b):
    return lax.dot_general(a, b, (((0,), (0,)), ((), ())), preferred_element_type=F32)


def _split_bf16(x):
    hi = x.astype(BF16)
    lo = (x - hi.astype(F32)).astype(BF16)
    return hi, lo


def _sigmoid(x):
    return 1.0 / (1.0 + jnp.exp(-x))


def _log_sigmoid(x):
    return jnp.minimum(x, 0.0) - jnp.log(1.0 + jnp.exp(-jnp.abs(x)))


def _mod_row(i, tm, n_ctx_rows, t_lat):
    r0 = i * tm
    return jnp.where(r0 < n_ctx_rows, 0, 1 + (r0 - n_ctx_rows) // t_lat)


def _store_token_major(ref, y):
    w = ref.shape[-1]
    for s in range(SUBLANES):
        ref[:, s, :] = y[:, s * w:(s + 1) * w]


def _load_token_major(ref, slot, r0, rows):
    return jnp.concatenate([ref[slot, r0:r0 + rows, s, :] for s in range(SUBLANES)], axis=1)


def _pack_bf16_pair(y):
    n = y.shape[1] // 2
    hi = lax.bitcast_convert_type(y[:, :n].astype(BF16).astype(F32), U32)
    lo = lax.bitcast_convert_type(y[:, n:].astype(BF16).astype(F32), U32)
    return hi | (lo >> 16)


def _unpack_bf16_pair(p):
    hi = lax.bitcast_convert_type(p & jnp.uint32(0xFFFF0000), F32)
    lo = lax.bitcast_convert_type(p << 16, F32)
    return hi, lo


def _ada_kernel(c_ref, w_ref, b_ref, o_ref):
    c = c_ref[...]
    s = c * _sigmoid(c)
    s_hi, s_lo = _split_bf16(s)
    w_hi, w_lo = _split_bf16(w_ref[...])
    o_ref[...] = _dot(s_hi, w_hi) + _dot(s_lo, w_hi) + _dot(s_hi, w_lo) + b_ref[...]


def _ada_mod(cvec, w_ada, b_ada):
    r, d = cvec.shape
    n = w_ada.shape[1]
    tn = _divisor(n, 512, LANES)
    return pl.pallas_call(
        _ada_kernel,
        grid=(n // tn,),
        in_specs=[pl.BlockSpec((r, d), lambda j: (0, 0)),
                  pl.BlockSpec((d, tn), lambda j: (0, j)),
                  pl.BlockSpec((1, tn), lambda j: (0, j))],
        out_specs=pl.BlockSpec((r, tn), lambda j: (0, j)),
        out_shape=jax.ShapeDtypeStruct((r, n), F32),
        compiler_params=_cparams("parallel"),
        name="ada_mod",
    )(cvec, w_ada, b_ada.reshape(1, n))


def _norm_mod_kernel(xc_ref, xl_ref, w_ref, mod_ref, o_ref, *, shift_idx, scale_idx, n_ctx_tiles):
    def body(x_ref):
        x = x_ref[...]
        y = x * lax.rsqrt(jnp.mean(x * x, axis=-1, keepdims=True) + EPS) * w_ref[...]
        y = y * (1.0 + mod_ref[0, scale_idx:scale_idx + 1, :]) + mod_ref[0, shift_idx:shift_idx + 1, :]
        o_ref[...] = y.astype(o_ref.dtype)

    i = pl.program_id(0)
    pl.when(i < n_ctx_tiles)(lambda: body(xc_ref))
    pl.when(i >= n_ctx_tiles)(lambda: body(xl_ref))


def _norm_mod(xc, xl, w, mod, t_lat, shift_idx, scale_idx, out_dtype):
    n_ctx_rows, d = xc.shape
    m = n_ctx_rows + xl.shape[0]
    tm = _divisor(math.gcd(n_ctx_rows, t_lat), 256)
    nct = n_ctx_rows // tm
    mrow = functools.partial(_mod_row, tm=tm, n_ctx_rows=n_ctx_rows, t_lat=t_lat)
    return pl.pallas_call(
        functools.partial(_norm_mod_kernel, shift_idx=shift_idx, scale_idx=scale_idx, n_ctx_tiles=nct),
        grid=(m // tm,),
        in_specs=[pl.BlockSpec((tm, d), lambda i: (jnp.minimum(i, nct - 1), 0)),
                  pl.BlockSpec((tm, d), lambda i: (jnp.maximum(i - nct, 0), 0)),
                  pl.BlockSpec((1, d), lambda i: (0, 0)),
                  pl.BlockSpec((1, mod.shape[1], d), lambda i: (mrow(i), 0, 0))],
        out_specs=pl.BlockSpec((tm, d), lambda i: (i, 0)),
        out_shape=jax.ShapeDtypeStruct((m, d), out_dtype),
        compiler_params=_cparams("arbitrary"),
        name="norm_mod",
    )(xc, xl, w.reshape(1, d), mod)


def _win_prep_kernel(w_ref, o_ref, g_ref, *, g0, n_gates):
    x = w_ref[...]
    o_ref[:, :g0] = x[:, :g0].astype(o_ref.dtype)
    o_ref[:, g0:] = x[:, g0 + n_gates:].astype(o_ref.dtype)
    pad = jnp.zeros((x.shape[0], LANES - n_gates), x.dtype)
    g_ref[...] = jnp.concatenate([x[:, g0:g0 + n_gates], pad], axis=1).astype(g_ref.dtype)


def _win_prep(w_in, g0, n_gates):
    k, n = w_in.shape
    tr = _divisor(k, 256)
    return pl.pallas_call(
        functools.partial(_win_prep_kernel, g0=g0, n_gates=n_gates),
        grid=(k // tr,),
        in_specs=[pl.BlockSpec((tr, n), lambda i: (i, 0))],
        out_specs=[pl.BlockSpec((tr, n - n_gates), lambda i: (i, 0)),
                   pl.BlockSpec((tr, LANES), lambda i: (i, 0))],
        out_shape=[jax.ShapeDtypeStruct((k, n - n_gates), BF16),
                   jax.ShapeDtypeStruct((k, LANES), BF16)],
        compiler_params=_cparams("parallel"),
        name="w_in_prep",
    )(w_in)


def _inproj_kernel(a_ref, b_ref, wg_ref, z_ref, zg_ref):
    a = a_ref[...]
    z_ref[...] = _dot(a, b_ref[...]).astype(z_ref.dtype)

    @pl.when(pl.program_id(1) == 0)
    def _gates():
        zg_ref[...] = _dot(a, wg_ref[...])


def _inproj(h, w_main, w_gate):
    m, k = h.shape
    n = w_main.shape[1]
    tm = _divisor(m, 1024)
    tn = _divisor(n, 1024, LANES)
    return pl.pallas_call(
        _inproj_kernel,
        grid=(m // tm, n // tn),
        in_specs=[pl.BlockSpec((tm, k), lambda i, j: (i, 0)),
                  pl.BlockSpec((k, tn), lambda i, j: (0, j)),
                  pl.BlockSpec((k, LANES), lambda i, j: (0, 0))],
        out_specs=[pl.BlockSpec((tm, tn), lambda i, j: (i, j)),
                   pl.BlockSpec((tm, LANES), lambda i, j: (i, 0))],
        out_shape=[jax.ShapeDtypeStruct((m, n), BF16),
                   jax.ShapeDtypeStruct((m, LANES), F32)],
        compiler_params=_cparams("parallel", "arbitrary"),
        name="in_proj",
    )(h, w_main, w_gate)


CHUNKS_PER_STEP = 2


def _scan_schedule(n_ctx, t_ctx, n_lat, t_lat, reverse):
    rowblk, first, last, ctxb, latb, islat, posblk = [], [], [], [], [], [], []
    step_rows = CHUNK * CHUNKS_PER_STEP
    ns_ctx, ns_lat = t_ctx // step_rows, t_lat // step_rows
    for b in range(n_ctx):
        order = range(ns_ctx - 1, -1, -1) if reverse else range(ns_ctx)
        for pos, c in enumerate(order):
            rowblk.append(b * ns_ctx + c)
            first.append(int(pos == 0))
            last.append(int(pos == ns_ctx - 1))
            ctxb.append(b)
            latb.append(0)
            islat.append(0)
            posblk.append(ns_lat)
    base = n_ctx * ns_ctx
    for b in range(n_lat):
        order = range(ns_lat - 1, -1, -1) if reverse else range(ns_lat)
        for pos, c in enumerate(order):
            rowblk.append(base + b * ns_lat + c)
            first.append(int(pos == 0))
            last.append(int(pos == ns_lat - 1))
            ctxb.append(n_ctx - 1)
            latb.append(b)
            islat.append(1)
            posblk.append(c)
    tabs = (rowblk, first, last, ctxb, latb, islat, posblk)
    return tuple(jnp.asarray(np.asarray(t, np.int32)) for t in tabs)


def _chunk_rows(ci, reverse):
    cc = (CHUNKS_PER_STEP - 1 - ci) if reverse else ci
    return pl.ds(pl.multiple_of(cc * CHUNK, CHUNK), CHUNK)


def _mlstm_kernel(rowblk, first, last, ctxb, latb, islat, posblk,
                  q_ref, k_ref, v_ref, g_ref, gb_ref, c0_ref, n0_ref, m0_ref, *rest,
                  heads, dk, dv, reverse, combine):
    if combine:
        (hf_ref, o_ref, nw_ref, cf_ref, nf_ref, mf_ref,
         out_ref, cout_ref, nout_ref, mout_ref, c_s, n_s, m_s) = rest
    else:
        out_ref, cout_ref, nout_ref, mout_ref, c_s, n_s, m_s = rest
    w = pl.program_id(0)
    lat = islat[w] == 1

    @pl.when(first[w] == 1)
    def _init():
        c_s[...] = jnp.where(lat, c0_ref[0, 0], 0.0)
        n_s[...] = jnp.where(lat, n0_ref[0, 0], 0.0)
        m_s[...] = jnp.where(lat, m0_ref[0, 0], 0.0)

    L = CHUNK
    ti = lax.broadcasted_iota(I32, (L, L), 0)
    si = lax.broadcasted_iota(I32, (L, L), 1)
    mask = (si >= ti) if reverse else (si <= ti)
    mask_t = (ti >= si) if reverse else (ti <= si)
    scale = dk ** -0.5
    gate_row = 2 if reverse else 0
    hs_ = range(heads)

    def chunk(ci, carry):
        rows = _chunk_rows(ci, reverse)
        g = g_ref[rows, :] + gb_ref[...]
        gt = g.T
        lg = _log_sigmoid(g)
        lgt = _log_sigmoid(gt)

        st = []
        for h in hs_:
            ci_, cf_ = gate_row * heads + h, (gate_row + 1) * heads + h
            i_col, i_row = g[:, ci_:ci_ + 1], gt[ci_:ci_ + 1, :]
            lf_col, lf_row = lg[:, cf_:cf_ + 1], lgt[cf_:cf_ + 1, :]
            b_col = jnp.sum(jnp.where(mask, lf_row, 0.0), axis=1, keepdims=True)
            b_row = jnp.sum(jnp.where(mask_t, lf_col, 0.0), axis=0, keepdims=True)
            bl = jnp.sum(lf_row, axis=1, keepdims=True)
            m_old = m_s[h:h + 1, 0:1]
            dm = jnp.where(mask, b_col - b_row + i_row, NEG_BIG)
            inter = b_col + m_old
            mt = jnp.maximum(jnp.max(dm, axis=1, keepdims=True), inter)
            p = jnp.exp(dm - mt)
            wi = jnp.exp(inter - mt)
            g_col = bl - b_col + i_col
            g_row = bl - b_row + i_row
            m_new = jnp.maximum(bl + m_old, jnp.max(g_row, axis=1, keepdims=True))
            a_prev = jnp.exp(bl + m_old - m_new)
            a_col = jnp.exp(g_col - m_new) * scale
            st.append((p, wi, mt, m_new, a_prev, a_col))

        qbs = [q_ref[rows, h * dk:(h + 1) * dk] for h in hs_]
        kbs = [k_ref[rows, h * dk:(h + 1) * dk] for h in hs_]
        vbs = [v_ref[rows, h * dv:(h + 1) * dv] for h in hs_]
        ss = [_dot_nt(qbs[h], kbs[h]) * (scale * st[h][0]) for h in hs_]
        qcs = [_dot(qbs[h], c_s[h].astype(BF16)) for h in hs_]
        n_olds = [n_s[h:h + 1, :] for h in hs_]
        for h in hs_:
            p, wi, mt, m_new, a_prev, a_col = st[h]
            s_ = ss[h]
            num = _dot(s_.astype(BF16), vbs[h]) + wi * qcs[h]
            qn = jnp.sum(qbs[h].astype(F32) * n_olds[h], axis=1, keepdims=True)
            den = jnp.sum(s_, axis=1, keepdims=True) + wi * qn
            hval = num / jnp.maximum(jnp.abs(den), jnp.exp(-mt))
            if combine:
                hs = hval + hf_ref[rows, h * dv:(h + 1) * dv].astype(F32)
                y = hs * lax.rsqrt(jnp.mean(hs * hs, axis=-1, keepdims=True) + EPS)
                gate = o_ref[rows, h * dv:(h + 1) * dv].astype(F32)
                y = y * nw_ref[:, h * dv:(h + 1) * dv] * _sigmoid(gate)
                out_ref[rows, h * dv:(h + 1) * dv] = y.astype(out_ref.dtype)
            else:
                out_ref[rows, h * dv:(h + 1) * dv] = hval.astype(out_ref.dtype)
        for h in hs_:
            p, wi, mt, m_new, a_prev, a_col = st[h]
            ak = a_col * kbs[h].astype(F32)
            c_s[h] = a_prev * c_s[h] + _dot_tn(ak.astype(BF16), vbs[h])
            n_s[h:h + 1, :] = a_prev * n_olds[h] + jnp.sum(ak, axis=0, keepdims=True)
            m_s[h:h + 1, :] = jnp.broadcast_to(m_new, (1, LANES))
        return carry

    lax.fori_loop(0, CHUNKS_PER_STEP, chunk, 0)

    @pl.when(jnp.logical_and(last[w] == 1, jnp.logical_not(lat)))
    def _emit():
        if combine:
            cout_ref[0, 0] = cf_ref[0]
            nout_ref[0, 0] = nf_ref[0]
            mout_ref[0, 0] = mf_ref[0]
            cout_ref[0, 1] = c_s[...]
            nout_ref[0, 1] = n_s[...]
            mout_ref[0, 1] = m_s[...]
        else:
            cout_ref[0] = c_s[...]
            nout_ref[0] = n_s[...]
            mout_ref[0] = m_s[...]


def _mlstm_scan(z, zg, gbias, st_c, st_n, st_m, dims, reverse, fwd=None, norm_w=None):
    (n_ctx, t_ctx, n_lat, t_lat, heads, dk, dv) = dims
    combine = fwd is not None
    m = z.shape[0]
    d = 1 if reverse else 0
    step_rows = CHUNK * CHUNKS_PER_STEP
    tabs = _scan_schedule(n_ctx, t_ctx, n_lat, t_lat, reverse)
    qw, vw = heads * dk, heads * dv
    assert (2 * qw) % vw == 0
    v_blk = (2 * qw) // vw
    in_specs = [
        pl.BlockSpec((step_rows, qw), lambda w, rb, *_: (rb[w], 0)),
        pl.BlockSpec((step_rows, qw), lambda w, rb, *_: (rb[w], 1)),
        pl.BlockSpec((step_rows, vw), lambda w, rb, *_: (rb[w], v_blk)),
        pl.BlockSpec((step_rows, LANES), lambda w, rb, *_: (rb[w], 0)),
        pl.BlockSpec((1, LANES), lambda w, *_: (0, 0)),
        pl.BlockSpec((1, 1, heads, dk, dv), lambda w, rb, f, l, cb, lb, *_: (lb[w], d, 0, 0, 0)),
        pl.BlockSpec((1, 1, heads, dk), lambda w, rb, f, l, cb, lb, *_: (lb[w], d, 0, 0)),
        pl.BlockSpec((1, 1, heads, LANES), lambda w, rb, f, l, cb, lb, *_: (lb[w], d, 0, 0)),
    ]
    args = [z, z, z, zg, gbias, st_c, st_n, st_m]
    if combine:
        hf, cf, nf, mf = fwd
        in_specs += [
            pl.BlockSpec((step_rows, vw), lambda w, rb, *_: (rb[w], 0)),
            pl.BlockSpec((step_rows, vw), lambda w, rb, *_: (rb[w], v_blk + 1)),
            pl.BlockSpec((1, vw), lambda w, *_: (0, 0)),
            pl.BlockSpec((1, heads, dk, dv), lambda w, rb, f, l, cb, *_: (cb[w], 0, 0, 0)),
            pl.BlockSpec((1, heads, dk), lambda w, rb, f, l, cb, *_: (cb[w], 0, 0)),
            pl.BlockSpec((1, heads, LANES), lambda w, rb, f, l, cb, *_: (cb[w], 0, 0)),
        ]
        args += [hf, z, norm_w.reshape(1, vw), cf, nf, mf]
        nd = (2,)
        st_idx = lambda w, rb, f, l, cb, *_: (cb[w], 0, 0, 0)
        c_idx = lambda w, rb, f, l, cb, *_: (cb[w], 0, 0, 0, 0)
    else:
        nd = ()
        st_idx = lambda w, rb, f, l, cb, *_: (cb[w], 0, 0)
        c_idx = lambda w, rb, f, l, cb, *_: (cb[w], 0, 0, 0)
    out_specs = [
        pl.BlockSpec((step_rows, vw), lambda w, rb, *_: (rb[w], 0)),
        pl.BlockSpec((1,) + nd + (heads, dk, dv), c_idx),
        pl.BlockSpec((1,) + nd + (heads, dk), st_idx),
        pl.BlockSpec((1,) + nd + (heads, LANES), st_idx),
    ]
    out_shape = [
        jax.ShapeDtypeStruct((m, vw), BF16),
        jax.ShapeDtypeStruct((n_ctx,) + nd + (heads, dk, dv), F32),
        jax.ShapeDtypeStruct((n_ctx,) + nd + (heads, dk), F32),
        jax.ShapeDtypeStruct((n_ctx,) + nd + (heads, LANES), F32),
    ]
    return pl.pallas_call(
        functools.partial(_mlstm_kernel, heads=heads, dk=dk, dv=dv, reverse=reverse, combine=combine),
        grid_spec=pltpu.PrefetchScalarGridSpec(
            num_scalar_prefetch=len(tabs), grid=(m // step_rows,),
            in_specs=in_specs, out_specs=out_specs,
            scratch_shapes=[pltpu.VMEM((heads, dk, dv), F32),
                            pltpu.VMEM((heads, dk), F32),
                            pltpu.VMEM((heads, LANES), F32)]),
        out_shape=out_shape,
        compiler_params=_cparams("arbitrary"),
        name="mlstm_bwd" if reverse else "mlstm_fwd",
    )(*tabs, *args)


def _rope_partner(x, d):
    q, hlf = d // 4, d // 2
    return jnp.concatenate([x[:, q:hlf], x[:, :q], x[:, hlf + q:], x[:, hlf:hlf + q]], axis=1)


def _ret_kernel(rowblk, first, last, ctxb, latb, islat, posblk,
                q_ref, k_ref, v_ref, cos_ref, sin_ref, ld_ref, s0_ref, *rest,
                heads, dk, dv, reverse, combine):
    if combine:
        of_ref, g_ref, nw_ref, sf_ref, out_ref, sout_ref, s_s = rest
    else:
        out_ref, sout_ref, s_s = rest
    w = pl.program_id(0)
    lat = islat[w] == 1

    @pl.when(first[w] == 1)
    def _init():
        s_s[...] = jnp.where(lat, s0_ref[0, 0], 0.0)

    L = CHUNK
    ti = lax.broadcasted_iota(I32, (L, L), 0).astype(F32)
    si = lax.broadcasted_iota(I32, (L, L), 1).astype(F32)
    diff = (si - ti) if reverse else (ti - si)
    idx = lax.broadcasted_iota(I32, (L, 1), 0).astype(F32)
    q_pow = (L - idx) if reverse else (idx + 1.0)
    k_pow = idx if reverse else (L - 1.0 - idx)
    scale = dk ** -0.5
    drow = 1 if reverse else 0
    log_decay = -jnp.exp(ld_ref[drow:drow + 1, :])
    hs_ = range(heads)

    def chunk(ci, carry):
        rows = _chunk_rows(ci, reverse)
        cos_t, sin_t = cos_ref[rows, :], sin_ref[rows, :]
        qbs, kbs, kds, vbs = [], [], [], []
        for h in hs_:
            ld = log_decay[:, h:h + 1]
            q = q_ref[rows, h * dk:(h + 1) * dk].astype(F32)
            k = k_ref[rows, h * dk:(h + 1) * dk].astype(F32)
            q = q * cos_t + _rope_partner(q, dk) * sin_t
            k = k * cos_t + _rope_partner(k, dk) * sin_t
            qbs.append(q.astype(BF16))
            kbs.append(k.astype(BF16))
            kds.append((k * (scale * jnp.exp(k_pow * ld))).astype(BF16))
            vbs.append(v_ref[rows, h * dv:(h + 1) * dv])
        scores = [_dot_nt(qbs[h], kbs[h]) for h in hs_]
        inter = [_dot(qbs[h], s_s[h].astype(BF16)) for h in hs_]
        for h in hs_:
            ld = log_decay[:, h:h + 1]
            intra = jnp.where(diff >= 0.0, jnp.exp(jnp.maximum(diff, 0.0) * ld), 0.0)
            a = scores[h] * (scale * intra)
            o = _dot(a.astype(BF16), vbs[h]) + inter[h] * jnp.exp(q_pow * ld)
            if combine:
                hs = o + of_ref[rows, h * dv:(h + 1) * dv].astype(F32)
                y = hs * lax.rsqrt(jnp.mean(hs * hs, axis=-1, keepdims=True) + EPS)
                gate = g_ref[rows, h * dv:(h + 1) * dv].astype(F32)
                y = y * nw_ref[:, h * dv:(h + 1) * dv] * (gate * _sigmoid(gate))
                out_ref[rows, h * dv:(h + 1) * dv] = y.astype(out_ref.dtype)
            else:
                out_ref[rows, h * dv:(h + 1) * dv] = o.astype(out_ref.dtype)
        for h in hs_:
            ld = log_decay[:, h:h + 1]
            s_s[h] = jnp.exp(float(L) * ld) * s_s[h] + _dot_tn(kds[h], vbs[h])
        return carry

    lax.fori_loop(0, CHUNKS_PER_STEP, chunk, 0)

    @pl.when(jnp.logical_and(last[w] == 1, jnp.logical_not(lat)))
    def _emit():
        if combine:
            sout_ref[0, 0] = sf_ref[0]
            sout_ref[0, 1] = s_s[...]
        else:
            sout_ref[0] = s_s[...]


def _ret_scan(z, cos_tab, sin_tab, ld, st_s, dims, col0, reverse, fwd=None, norm_w=None):
    (n_ctx, t_ctx, n_lat, t_lat, heads, dk, dv) = dims
    combine = fwd is not None
    m = z.shape[0]
    d = 1 if reverse else 0
    step_rows = CHUNK * CHUNKS_PER_STEP
    tabs = _scan_schedule(n_ctx, t_ctx, n_lat, t_lat, reverse)
    qw, vw = heads * dk, heads * dv
    assert qw == vw and col0 % qw == 0
    b0 = col0 // qw
    in_specs = [
        pl.BlockSpec((step_rows, qw), lambda w, rb, *_: (rb[w], b0)),
        pl.BlockSpec((step_rows, qw), lambda w, rb, *_: (rb[w], b0 + 1)),
        pl.BlockSpec((step_rows, vw), lambda w, rb, *_: (rb[w], b0 + 2)),
        pl.BlockSpec((step_rows, dk), lambda w, rb, f, l, cb, lb, il, pb: (pb[w], 0)),
        pl.BlockSpec((step_rows, dk), lambda w, rb, f, l, cb, lb, il, pb: (pb[w], 0)),
        pl.BlockSpec((SUBLANES, LANES), lambda w, *_: (0, 0)),
        pl.BlockSpec((1, 1, heads, dk, dv), lambda w, rb, f, l, cb, lb, *_: (lb[w], d, 0, 0, 0)),
    ]
    args = [z, z, z, cos_tab, sin_tab, ld, st_s]
    if combine:
        of, sf = fwd
        in_specs += [
            pl.BlockSpec((step_rows, vw), lambda w, rb, *_: (rb[w], 0)),
            pl.BlockSpec((step_rows, vw), lambda w, rb, *_: (rb[w], b0 + 3)),
            pl.BlockSpec((1, vw), lambda w, *_: (0, 0)),
            pl.BlockSpec((1, heads, dk, dv), lambda w, rb, f, l, cb, *_: (cb[w], 0, 0, 0)),
        ]
        args += [of, z, norm_w.reshape(1, vw), sf]
        s_spec = pl.BlockSpec((1, 2, heads, dk, dv), lambda w, rb, f, l, cb, *_: (cb[w], 0, 0, 0, 0))
        s_shape = jax.ShapeDtypeStruct((n_ctx, 2, heads, dk, dv), F32)
    else:
        s_spec = pl.BlockSpec((1, heads, dk, dv), lambda w, rb, f, l, cb, *_: (cb[w], 0, 0, 0))
        s_shape = jax.ShapeDtypeStruct((n_ctx, heads, dk, dv), F32)
    return pl.pallas_call(
        functools.partial(_ret_kernel, heads=heads, dk=dk, dv=dv, reverse=reverse, combine=combine),
        grid_spec=pltpu.PrefetchScalarGridSpec(
            num_scalar_prefetch=len(tabs), grid=(m // step_rows,),
            in_specs=in_specs,
            out_specs=[pl.BlockSpec((step_rows, vw), lambda w, rb, *_: (rb[w], 0)), s_spec],
            scratch_shapes=[pltpu.VMEM((heads, dk, dv), F32)]),
        out_shape=[jax.ShapeDtypeStruct((m, vw), BF16), s_shape],
        compiler_params=_cparams("arbitrary"),
        name="ret_bwd" if reverse else "ret_fwd",
    )(*tabs, *args)


def _rope_tables(t_lat, d):
    quarter = d // 4
    rows = t_lat // GRID_W
    row = jnp.repeat(jnp.arange(rows, dtype=F32), GRID_W)
    col = jnp.tile(jnp.arange(GRID_W, dtype=F32), rows)
    inv = ROPE_BASE ** (-jnp.arange(quarter, dtype=F32) / quarter)
    ar = row[:, None] * inv[None, :]
    ac = col[:, None] * inv[None, :]
    cos_t = jnp.concatenate([jnp.cos(ar), jnp.cos(ar), jnp.cos(ac), jnp.cos(ac)], axis=1)
    sin_t = jnp.concatenate([-jnp.sin(ar), jnp.sin(ar), -jnp.sin(ac), jnp.sin(ac)], axis=1)
    pad_rows = CHUNK * CHUNKS_PER_STEP
    cos_t = jnp.concatenate([cos_t, jnp.ones((pad_rows, d), F32)], axis=0)
    sin_t = jnp.concatenate([sin_t, jnp.zeros((pad_rows, d), F32)], axis=0)
    return cos_t, sin_t


def _outproj_kernel(a1_ref, a2_ref, b1_ref, b2_ref, xc_ref, xl_ref, mod_ref, o_ref, *, gate_idx, n_ctx_tiles):
    acc = _dot(a1_ref[...], b1_ref[...]) + _dot(a2_ref[...], b2_ref[...])
    upd = mod_ref[0, gate_idx:gate_idx + 1, :] * acc
    i = pl.program_id(0)

    @pl.when(i < n_ctx_tiles)
    def _ctx():
        o_ref[...] = xc_ref[...] + upd

    @pl.when(i >= n_ctx_tiles)
    def _lat():
        o_ref[...] = xl_ref[...] + upd


def _outproj(a1, a2, w_out, xc, xl, mod, t_lat, gate_idx):
    m, k1 = a1.shape
    k2 = a2.shape[1]
    n = w_out.shape[1]
    n_ctx_rows = xc.shape[0]
    assert k1 == k2
    tm = _divisor(math.gcd(n_ctx_rows, t_lat), 1024)
    tn = _divisor(n, 512, LANES)
    nct = n_ctx_rows // tm
    mrow = functools.partial(_mod_row, tm=tm, n_ctx_rows=n_ctx_rows, t_lat=t_lat)
    return pl.pallas_call(
        functools.partial(_outproj_kernel, gate_idx=gate_idx, n_ctx_tiles=nct),
        grid=(m // tm, n // tn),
        in_specs=[pl.BlockSpec((tm, k1), lambda i, j: (i, 0)),
                  pl.BlockSpec((tm, k2), lambda i, j: (i, 0)),
                  pl.BlockSpec((k1, tn), lambda i, j: (0, j)),
                  pl.BlockSpec((k2, tn), lambda i, j: (1, j)),
                  pl.BlockSpec((tm, tn), lambda i, j: (jnp.minimum(i, nct - 1), jnp.where(i < nct, j, 0))),
                  pl.BlockSpec((tm, tn), lambda i, j: (jnp.maximum(i - nct, 0), jnp.where(i >= nct, j, 0))),
                  pl.BlockSpec((1, mod.shape[1], tn), lambda i, j: (mrow(i), 0, j))],
        out_specs=pl.BlockSpec((tm, tn), lambda i, j: (i, j)),
        out_shape=jax.ShapeDtypeStruct((m, n), F32),
        compiler_params=_cparams("arbitrary", "arbitrary"),
        name="out_proj",
    )(a1, a2, w_out, w_out, xc, xl, mod)


def _router_kernel(x_ref, w_ref, mod_ref, wr_ref, br_ref, h_ref, eid_ref, wt_ref,
                   *, shift_idx, scale_idx, n_groups, per_group):
    x = x_ref[...]
    y = x * lax.rsqrt(jnp.mean(x * x, axis=-1, keepdims=True) + EPS) * w_ref[...]
    y = y * (1.0 + mod_ref[0, scale_idx:scale_idx + 1, :]) + mod_ref[0, shift_idx:shift_idx + 1, :]
    _store_token_major(h_ref, _pack_bf16_pair(y))
    logits = _dot(y.astype(BF16), wr_ref[...]) + br_ref[...]

    n_exp = n_groups * per_group
    lane = lax.broadcasted_iota(I32, logits.shape, 1)
    gmask = lane < n_groups
    gl = jnp.where(gmask, logits, NEG_BIG)
    gmax = jnp.max(gl, axis=1, keepdims=True)
    gsum = jnp.sum(jnp.where(gmask, jnp.exp(gl - gmax), 0.0), axis=1, keepdims=True)
    g_w = 1.0 / gsum
    g_idx = jnp.min(jnp.where(gl == gmax, lane, LANES), axis=1, keepdims=True)

    in_group = jnp.logical_and(lane >= n_groups + g_idx * per_group,
                               lane < n_groups + (g_idx + 1) * per_group)
    in_group = jnp.logical_and(in_group, lane < n_groups + n_exp)
    el = jnp.where(in_group, logits, NEG_BIG)
    m1 = jnp.max(el, axis=1, keepdims=True)
    i1 = jnp.min(jnp.where(el == m1, lane, LANES), axis=1, keepdims=True)
    el2 = jnp.where(lane == i1, NEG_BIG, el)
    m2 = jnp.max(el2, axis=1, keepdims=True)
    i2 = jnp.min(jnp.where(el2 == m2, lane, LANES), axis=1, keepdims=True)
    e2 = jnp.exp(m2 - m1)
    p1 = 1.0 / (1.0 + e2)
    p2 = e2 * p1
    eid_ref[...] = jnp.where(lane == 0, i1 - n_groups, jnp.where(lane == 1, i2 - n_groups, 0))
    wt_ref[...] = jnp.where(lane == 0, g_w * p1, jnp.where(lane == 1, g_w * p2, 0.0))


def _router(x, w, mod, w_r, b_r, n_ctx_rows, t_lat, shift_idx, scale_idx, n_groups, per_group):
    m, d = x.shape
    tm = _divisor(math.gcd(n_ctx_rows, t_lat), 256)
    mrow = functools.partial(_mod_row, tm=tm, n_ctx_rows=n_ctx_rows, t_lat=t_lat)
    return pl.pallas_call(
        functools.partial(_router_kernel, shift_idx=shift_idx, scale_idx=scale_idx,
                          n_groups=n_groups, per_group=per_group),
        grid=(m // tm,),
        in_specs=[pl.BlockSpec((tm, d), lambda i: (i, 0)),
                  pl.BlockSpec((1, d), lambda i: (0, 0)),
                  pl.BlockSpec((1, mod.shape[1], d), lambda i: (mrow(i), 0, 0)),
                  pl.BlockSpec((d, LANES), lambda i: (0, 0)),
                  pl.BlockSpec((1, LANES), lambda i: (0, 0))],
        out_specs=[pl.BlockSpec((tm, SUBLANES, d // (2 * SUBLANES)), lambda i: (i, 0, 0)),
                   pl.BlockSpec((tm, LANES), lambda i: (i, 0)),
                   pl.BlockSpec((tm, LANES), lambda i: (i, 0))],
        out_shape=[jax.ShapeDtypeStruct((m, SUBLANES, d // (2 * SUBLANES)), U32),
                   jax.ShapeDtypeStruct((m, LANES), I32),
                   jax.ShapeDtypeStruct((m, LANES), F32)],
        compiler_params=_cparams("parallel"),
        name="router",
    )(x, w.reshape(1, d), mod, w_r, b_r)


GATHER_UNROLL = 8


def _row_copy(src_hbm, dst_buf, sem, src_row, slot, dst_row):
    return pltpu.make_async_copy(src_hbm.at[src_row], dst_buf.at[slot, dst_row], sem.at[slot])


def _start_row_gather(tok_fn, src_hbm, buf, sem, slot, n_idx):
    def body(jj, carry):
        for u in range(GATHER_UNROLL):
            j = jj * GATHER_UNROLL + u
            _row_copy(src_hbm, buf, sem, tok_fn(j), slot, j).start(priority=u % 2)
        return carry

    lax.fori_loop(0, n_idx // GATHER_UNROLL, body, 0)


def _wait_row_gather(src_hbm, buf, sem, slot, n_idx):
    def body(j, carry):
        _row_copy(src_hbm, buf, sem, 0, slot, j).wait()
        return carry

    lax.fori_loop(0, n_idx, body, 0, unroll=GATHER_UNROLL)


def _ring_step(cur_fn, nxt_fn, src_hbm, buf, sem, n_idx, consume):
    i = pl.program_id(0)

    @pl.when(i == 0)
    def _prime():
        _start_row_gather(cur_fn, src_hbm, buf, sem, 0, n_idx)

    for slot in range(2):
        @pl.when(i % 2 == slot)
        def _work(slot=slot):
            @pl.when(i + 1 < pl.num_programs(0))
            def _prefetch():
                _start_row_gather(nxt_fn, src_hbm, buf, sem, 1 - slot, n_idx)

            _wait_row_gather(src_hbm, buf, sem, slot, n_idx)
            consume(slot)


def _scatter_rows_kernel(cnt, dst_ref, h_ref, xs_hbm, zero, sem, *, rows, blk_rows):
    n_blk = cnt.shape[0]

    def pad_copy(b, r):
        return pltpu.make_async_copy(zero.at[0], xs_hbm.at[b * blk_rows + r], sem.at[1])

    def for_each_pad(fn):
        def blk(b, carry):
            def row(r, c2):
                fn(pad_copy(b, r))
                return c2
            return lax.fori_loop(cnt[b], blk_rows, row, carry)
        lax.fori_loop(0, n_blk, blk, 0)

    @pl.when(pl.program_id(0) == 0)
    def _pad():
        zero[...] = jnp.zeros(zero.shape, zero.dtype)
        for_each_pad(lambda cp: cp.start())
        for_each_pad(lambda cp: cp.wait())

    def row_copy(j, r):
        return pltpu.make_async_copy(h_ref.at[r], xs_hbm.at[dst_ref[0, 0, j]], sem.at[0])

    def issue(rr, carry):
        for u in range(GATHER_UNROLL // TOP_K):
            r = rr * (GATHER_UNROLL // TOP_K) + u
            for kk in range(TOP_K):
                row_copy(r * TOP_K + kk, r).start(priority=kk % 2)
        return carry

    lax.fori_loop(0, rows * TOP_K // GATHER_UNROLL, issue, 0)

    def drain(r, carry):
        for kk in range(TOP_K):
            row_copy(r * TOP_K + kk, r).wait()
        return carry

    lax.fori_loop(0, rows, drain, 0, unroll=GATHER_UNROLL // TOP_K)


def _scatter_rows(src, dest, blk_cnt, blk_rows):
    t = src.shape[0]
    rows = _divisor(t, 256)
    n_slots = blk_cnt.shape[0] * blk_rows
    return pl.pallas_call(
        functools.partial(_scatter_rows_kernel, rows=rows, blk_rows=blk_rows),
        grid_spec=pltpu.PrefetchScalarGridSpec(
            num_scalar_prefetch=1, grid=(t // rows,),
            in_specs=[pl.BlockSpec((1, 1, TOP_K * rows), lambda i, *_: (i, 0, 0), memory_space=pltpu.SMEM),
                      pl.BlockSpec((rows,) + src.shape[1:], lambda i, *_: (i, 0, 0))],
            out_specs=pl.BlockSpec(memory_space=pl.ANY),
            scratch_shapes=[pltpu.VMEM((1,) + src.shape[1:], src.dtype), pltpu.SemaphoreType.DMA((2,))]),
        out_shape=jax.ShapeDtypeStruct((n_slots,) + src.shape[1:], src.dtype),
        compiler_params=_cparams("arbitrary"),
        name="moe_dispatch",
    )(blk_cnt, dest.reshape(t // rows, 1, TOP_K * rows), src)


def _moe_up_kernel(xblk, eid, wjt, oblk, ojt, fst, valid, ring, has_nxt, nxt_e, nxt_j,
                   x_ref, wg_hbm, wu_hbm, h_ref, wbuf, wg_s, wu_s, flat, sem, *, tn):
    w = pl.program_id(0)

    def wcopy(e, j, slot, which):
        src = wu_hbm if which else wg_hbm
        cols = pl.ds(pl.multiple_of(j * tn, tn), tn)
        return pltpu.make_async_copy(src.at[e, :, cols], wbuf.at[slot, which], sem.at[slot, which])

    @pl.when(w == 0)
    def _prime():
        for which in range(2):
            wcopy(eid[0], wjt[0], ring[0], which).start()

    @pl.when(valid[w] == 1)
    def _go():
        @pl.when(fst[w] == 1)
        def _swap():
            slot = ring[w]

            @pl.when(has_nxt[w] == 1)
            def _request():
                for which in range(2):
                    wcopy(nxt_e[w], nxt_j[w], 1 - slot, which).start()

            for which in range(2):
                wcopy(eid[w], wjt[w], slot, which).wait()
            wg_s[...] = wbuf[slot, 0].astype(BF16)
            wu_s[...] = wbuf[slot, 1].astype(BF16)

        wd = x_ref.shape[-1]
        for sl in range(SUBLANES):
            flat[:, sl * wd:(sl + 1) * wd] = x_ref[:, sl, :]
        hi, lo = _unpack_bf16_pair(flat[...])
        xh, xl = hi.astype(BF16), lo.astype(BF16)
        n = xh.shape[1]
        a = _dot(xh, wg_s[:n, :]) + _dot(xl, wg_s[n:, :])
        b = _dot(xh, wu_s[:n, :]) + _dot(xl, wu_s[n:, :])
        h_ref[...] = (a * _sigmoid(a) * b).astype(h_ref.dtype)

    @pl.when(valid[w] == 0)
    def _pad():
        h_ref[...] = jnp.zeros(h_ref.shape, h_ref.dtype)


def _moe_up(xs, w_gate, w_up, tabs, n_work, tn):
    n_slots = xs.shape[0]
    d = w_gate.shape[1]
    f = w_gate.shape[2]
    return pl.pallas_call(
        functools.partial(_moe_up_kernel, tn=tn),
        grid_spec=pltpu.PrefetchScalarGridSpec(
            num_scalar_prefetch=len(tabs), grid=(n_work,),
            in_specs=[pl.BlockSpec((MOE_TILE,) + xs.shape[1:], lambda w, xb, *_: (xb[w], 0, 0)),
                      pl.BlockSpec(memory_space=pl.ANY),
                      pl.BlockSpec(memory_space=pl.ANY)],
            out_specs=pl.BlockSpec((MOE_TILE, tn), lambda w, xb, e, wj, ob, oj, *_: (ob[w], oj[w])),
            scratch_shapes=[pltpu.VMEM((2, 2, d, tn), w_gate.dtype),
                            pltpu.VMEM((d, tn), BF16), pltpu.VMEM((d, tn), BF16),
                            pltpu.VMEM((MOE_TILE, xs.shape[1] * xs.shape[2]), xs.dtype),
                            pltpu.SemaphoreType.DMA((2, 2))]),
        out_shape=jax.ShapeDtypeStruct((n_slots, f), BF16),
        compiler_params=_cparams("arbitrary"),
        name="moe_up",
    )(*tabs, xs, w_gate, w_up)


def _moe_down_kernel(eid, fst, valid, ring, has_nxt, nxt_e, h_ref, wd_hbm, y_ref, wbuf, wd_s, sem):
    w = pl.program_id(0)

    def wcopy(e, slot):
        return pltpu.make_async_copy(wd_hbm.at[e], wbuf.at[slot], sem.at[slot])

    @pl.when(w == 0)
    def _prime():
        wcopy(eid[0], ring[0]).start()

    @pl.when(valid[w] == 1)
    def _go():
        @pl.when(fst[w] == 1)
        def _swap():
            slot = ring[w]

            @pl.when(has_nxt[w] == 1)
            def _request():
                wcopy(nxt_e[w], 1 - slot).start()

            wcopy(eid[w], slot).wait()
            wd_s[...] = wbuf[slot].astype(BF16)

        _store_token_major(y_ref, _pack_bf16_pair(_dot(h_ref[...], wd_s[...])))

    @pl.when(valid[w] == 0)
    def _pad():
        y_ref[...] = jnp.zeros(y_ref.shape, y_ref.dtype)


def _moe_down(hs, w_down, tabs):
    n_slots, f = hs.shape
    d = w_down.shape[2]
    nblk = n_slots // MOE_TILE
    return pl.pallas_call(
        _moe_down_kernel,
        grid_spec=pltpu.PrefetchScalarGridSpec(
            num_scalar_prefetch=len(tabs), grid=(nblk,),
            in_specs=[pl.BlockSpec((MOE_TILE, f), lambda w, *_: (w, 0)),
                      pl.BlockSpec(memory_space=pl.ANY)],
            out_specs=pl.BlockSpec((MOE_TILE, SUBLANES, d // (2 * SUBLANES)), lambda w, *_: (w, 0, 0)),
            scratch_shapes=[pltpu.VMEM((2, f, d), w_down.dtype), pltpu.VMEM((f, d), BF16),
                            pltpu.SemaphoreType.DMA((2,))]),
        out_shape=jax.ShapeDtypeStruct((n_slots, SUBLANES, d // (2 * SUBLANES)), U32),
        compiler_params=_cparams("arbitrary"),
        name="moe_down",
    )(*tabs, hs, w_down)


def _final_kernel(dst_ref, nxt_ref, x_ref, wt_ref, mod_ref, fw_ref, ys_hbm, o_ref, buf, sem, *, rows, gate_idx):
    def consume(slot):
        wt = wt_ref[...]
        moe = None
        for kk in range(TOP_K):
            hi, lo = _unpack_bf16_pair(_load_token_major(buf, slot, kk * rows, rows))
            term = wt[:, kk:kk + 1] * jnp.concatenate([hi, lo], axis=1)
            moe = term if moe is None else moe + term
        x = x_ref[...] + mod_ref[0, gate_idx:gate_idx + 1, :] * moe
        o_ref[...] = x * lax.rsqrt(jnp.mean(x * x, axis=-1, keepdims=True) + EPS) * fw_ref[...]

    _ring_step(lambda j: dst_ref[0, 0, j], lambda j: nxt_ref[0, 0, j], ys_hbm, buf, sem, TOP_K * rows, consume)


def _final(x1, wt, dest, ys, mod, final_w, row0, n_rows, n_ctx_rows, t_lat, gate_idx):
    d = x1.shape[1]
    tm = _divisor(math.gcd(n_ctx_rows, t_lat), 128)
    t0 = row0 // tm
    nt = n_rows // tm
    mrow = functools.partial(_mod_row, tm=tm, n_ctx_rows=n_ctx_rows, t_lat=t_lat)
    dest3 = dest.reshape(-1, tm, TOP_K).transpose(0, 2, 1).reshape(-1, 1, TOP_K * tm)
    return pl.pallas_call(
        functools.partial(_final_kernel, rows=tm, gate_idx=gate_idx),
        grid=(nt,),
        in_specs=[pl.BlockSpec((1, 1, TOP_K * tm), lambda i: (t0 + i, 0, 0), memory_space=pltpu.SMEM),
                  pl.BlockSpec((1, 1, TOP_K * tm), lambda i: (t0 + jnp.minimum(i + 1, nt - 1), 0, 0),
                               memory_space=pltpu.SMEM),
                  pl.BlockSpec((tm, d), lambda i: (t0 + i, 0)),
                  pl.BlockSpec((tm, LANES), lambda i: (t0 + i, 0)),
                  pl.BlockSpec((1, mod.shape[1], d), lambda i: (mrow(t0 + i), 0, 0)),
                  pl.BlockSpec((1, d), lambda i: (0, 0)),
                  pl.BlockSpec(memory_space=pl.ANY)],
        out_specs=pl.BlockSpec((tm, d), lambda i: (i, 0)),
        out_shape=jax.ShapeDtypeStruct((n_rows, d), F32),
        scratch_shapes=[pltpu.VMEM((2, TOP_K * tm) + ys.shape[1:], ys.dtype), pltpu.SemaphoreType.DMA((2,))],
        compiler_params=_cparams("arbitrary"),
        name="moe_combine_final",
    )(dest3, dest3, x1, wt, mod, final_w.reshape(1, d), ys)


def _count_le(sorted_ends, idx):
    return jnp.sum((sorted_ends[None, :] <= idx[:, None]).astype(I32), axis=1)


def _moe_plan(eid, n_exp, n_jt):
    t = eid.shape[0]
    n_assign = t * TOP_K
    eflat = eid.reshape(n_assign)
    onehot = (eflat[:, None] == jnp.arange(n_exp, dtype=I32)[None, :]).astype(I32)
    csum = jnp.cumsum(onehot, axis=0)
    counts = csum[-1]
    nb = (counts + MOE_TILE - 1) // MOE_TILE
    blk_end = jnp.cumsum(nb)
    blk_start = blk_end - nb
    n_blocks = blk_end[-1]
    pad_start = blk_start * MOE_TILE
    dest = jnp.sum(onehot * (csum - 1 + pad_start[None, :]), axis=1)

    assert n_assign % MOE_TILE == 0
    nblk_max = n_assign // MOE_TILE + n_exp
    b_idx = jnp.arange(nblk_max, dtype=I32)
    blk_e = jnp.minimum(_count_le(blk_end, b_idx), n_exp - 1)
    blk_off = (b_idx - blk_start[blk_e]) * MOE_TILE
    blk_cnt = jnp.where(b_idx < n_blocks, jnp.clip(counts[blk_e] - blk_off, 0, MOE_TILE), 0).astype(I32)

    d_valid = (b_idx < n_blocks).astype(I32)
    last_e = blk_e[jnp.maximum(n_blocks - 1, 0)]
    d_eid = jnp.where(d_valid == 1, blk_e, last_e).astype(I32)
    d_first = jnp.logical_and(d_valid == 1, b_idx == blk_start[d_eid]).astype(I32)
    has_blk = (nb > 0).astype(I32)
    e_ord = jnp.cumsum(has_blk) - 1
    d_ring = (e_ord[d_eid] % 2).astype(I32)
    nxt_b = jnp.minimum(blk_end[d_eid], nblk_max - 1)
    d_has_nxt = (blk_end[d_eid] < n_blocks).astype(I32)
    d_nxt_e = blk_e[nxt_b].astype(I32)

    n_work = n_jt * nblk_max
    w_idx = jnp.arange(n_work, dtype=I32)
    per_e = n_jt * nb
    w_end = jnp.cumsum(per_e)
    w_valid = w_idx < w_end[-1]
    we = jnp.minimum(_count_le(w_end, w_idx), n_exp - 1)
    r = w_idx - (w_end[we] - per_e[we])
    nbe = jnp.maximum(nb[we], 1)
    u_jt = r // nbe
    u_t = r - u_jt * nbe
    u_blk = blk_start[we] + u_t
    last_w = jnp.maximum(w_end[-1] - 1, 0)
    spare = w_idx - w_end[-1]
    u_xblk = jnp.where(w_valid, u_blk, u_blk[last_w]).astype(I32)
    u_eid = jnp.where(w_valid, we, we[last_w]).astype(I32)
    u_wjt = jnp.where(w_valid, u_jt, u_jt[last_w]).astype(I32)
    u_oblk = jnp.where(w_valid, u_blk, n_blocks + spare // n_jt).astype(I32)
    u_ojt = jnp.where(w_valid, u_jt, spare % n_jt).astype(I32)
    u_first = jnp.logical_and(w_valid, u_t == 0).astype(I32)
    g_ord = jnp.cumsum(u_first) - 1
    u_ring = (g_ord % 2).astype(I32)
    nxt_w = jnp.minimum(w_idx - u_t + nbe, n_work - 1)
    u_has_nxt = jnp.logical_and(w_valid, w_idx - u_t + nbe < w_end[-1]).astype(I32)
    u_nxt_e = we[nxt_w].astype(I32)
    u_nxt_j = u_jt[nxt_w].astype(I32)
    up_tabs = (u_xblk, u_eid, u_wjt, u_oblk, u_ojt, u_first, w_valid.astype(I32), u_ring, u_has_nxt, u_nxt_e, u_nxt_j)
    down_tabs = (d_eid, d_first, d_valid, d_ring, d_has_nxt, d_nxt_e)
    return dest.astype(I32), blk_cnt, up_tabs, down_tabs, n_work


def kernel(x_prompt, x_sample, state_mlstm_C, state_mlstm_n, state_mlstm_m, state_ret_S, c, c_ctx,
           norm1_w, norm2_w, w_ada, b_ada, w_in, b_mgates, mlstm_norm_w, ret_norm_w, ret_log_decay, w_out,
           w_router_group, b_router_group, w_router_expert, b_router_expert, w_exp_gate, w_exp_up,
           w_exp_down, final_norm_w):
    n_ctx, t_ctx, d_model = x_prompt.shape
    n_lat, t_lat, _ = x_sample.shape
    depth = norm1_w.shape[0]
    assert depth == 1, "single-layer trunk"
    m_heads, m_dk, m_dv = state_mlstm_C.shape[3:]
    r_heads, r_dk, r_dv = state_ret_S.shape[3:]
    n_groups = w_router_group.shape[2]
    n_exp = w_router_expert.shape[2]
    per_group = n_exp // n_groups
    d_exp = w_exp_gate.shape[3]
    n_gates = N_GATE_ROWS * m_heads
    n_ctx_rows = n_ctx * t_ctx
    step_rows = CHUNK * CHUNKS_PER_STEP
    assert t_ctx % step_rows == 0 and t_lat % step_rows == 0 and n_gates <= LANES
    assert n_groups + n_exp <= LANES

    cvec = jnp.concatenate([c_ctx[None, :], c, jnp.zeros((SUBLANES - 1 - n_lat, d_model), F32)], axis=0)
    mod = _ada_mod(cvec, w_ada[0], b_ada[0]).reshape(SUBLANES, 6, d_model)

    xc = x_prompt.reshape(n_ctx_rows, d_model)
    xl = x_sample.reshape(n_lat * t_lat, d_model)
    h = _norm_mod(xc, xl, norm1_w[0], mod, t_lat, 0, 1, BF16)

    g0 = 2 * m_heads * m_dk + 2 * m_heads * m_dv
    w_main, w_gate = _win_prep(w_in[0], g0, n_gates)
    z, zg = _inproj(h, w_main, w_gate)

    gbias = jnp.pad(b_mgates[0].reshape(1, n_gates), ((0, 0), (0, LANES - n_gates)))
    st_m = jnp.broadcast_to(state_mlstm_m[:, 0][..., None], (n_lat, 2, m_heads, LANES))
    mdims = (n_ctx, t_ctx, n_lat, t_lat, m_heads, m_dk, m_dv)
    fwd_m = _mlstm_scan(z, zg, gbias, state_mlstm_C[:, 0], state_mlstm_n[:, 0], st_m, mdims, False)
    mix_m, new_c, new_n, new_m = _mlstm_scan(z, zg, gbias, state_mlstm_C[:, 0], state_mlstm_n[:, 0], st_m, mdims,
                                             True, fwd=fwd_m, norm_w=mlstm_norm_w[0])

    cos_tab, sin_tab = _rope_tables(t_lat, r_dk)
    ld = jnp.pad(ret_log_decay[0], ((0, SUBLANES - 2), (0, LANES - r_heads)))
    rdims = (n_ctx, t_ctx, n_lat, t_lat, r_heads, r_dk, r_dv)
    fwd_r = _ret_scan(z, cos_tab, sin_tab, ld, state_ret_S[:, 0], rdims, g0, False)
    mix_r, new_s = _ret_scan(z, cos_tab, sin_tab, ld, state_ret_S[:, 0], rdims, g0, True,
                             fwd=fwd_r, norm_w=ret_norm_w[0])

    x1 = _outproj(mix_m, mix_r, w_out[0].astype(BF16), xc, xl, mod, t_lat, 2)

    w_r = jnp.pad(jnp.concatenate([w_router_group[0], w_router_expert[0]], axis=1),
                  ((0, 0), (0, LANES - n_groups - n_exp)))
    b_r = jnp.pad(jnp.concatenate([b_router_group[0], b_router_expert[0]])[None, :],
                  ((0, 0), (0, LANES - n_groups - n_exp)))
    h2, eid, wt = _router(x1, norm2_w[0], mod, w_r.astype(BF16), b_r, n_ctx_rows, t_lat, 3, 4, n_groups, per_group)

    tn_up = _divisor(d_exp, 512, LANES)
    dest, blk_cnt, up_tabs, down_tabs, n_work = _moe_plan(eid[:, :TOP_K], n_exp, d_exp // tn_up)
    xs = _scatter_rows(h2, dest, blk_cnt, MOE_TILE)
    hs = _moe_up(xs, w_exp_gate[0], w_exp_up[0], up_tabs, n_work, tn_up)
    ys = _moe_down(hs, w_exp_down[0], down_tabs)

    y_ctx = _final(x1, wt, dest, ys, mod, final_norm_w, 0, n_ctx_rows, n_ctx_rows, t_lat, 5)
    y_lat = _final(x1, wt, dest, ys, mod, final_norm_w, n_ctx_rows, n_lat * t_lat, n_ctx_rows, t_lat, 5)

    return (y_ctx.reshape(n_ctx, t_ctx, d_model), y_lat.reshape(n_lat, t_lat, d_model),
            new_c[:, None], new_n[:, None], new_m[:, None, :, :, 0], new_s[:, None])
```

```python
import functools
import math

import numpy as np
import jax
import jax.numpy as jnp
from jax import lax
from jax.experimental import pallas as pl
from jax.experimental.pallas import tpu as pltpu

F32 = jnp.float32
BF16 = jnp.bfloat16
I32 = jnp.int32
U32 = jnp.uint32

CHUNK = 128
GRID_W = 64
ROPE_BASE = 10000.0
EPS = 1e-6
TOP_K = 2
N_GATE_ROWS = 4

LANES = 128
SUBLANES = 8
VMEM_LIMIT_BYTES = 56 * 1024 * 1024
NEG_BIG = -1e30
MOE_TILE = 256


def _divisor(n, pref, mult=SUBLANES):
    if n <= pref:
        return n
    d = (pref // mult) * mult
    while d > mult and n % d:
        d -= mult
    assert n % d == 0, (n, pref, mult)
    return d


def _cparams(*sem):
    return pltpu.CompilerParams(dimension_semantics=sem, vmem_limit_bytes=VMEM_LIMIT_BYTES)


def _dot(a, b):
    return jnp.dot(a, b, preferred_element_type=F32)


def _dot_nt(a, b):
    return lax.dot_general(a, b, (((1,), (1,)), ((), ())), preferred_element_type=F32)


def _dot_tn(a, b):
    return lax.dot_general(a, b, (((0,), (0,)), ((), ())), preferred_element_type=F32)


def _split_bf16(x):
    hi = x.astype(BF16)
    lo = (x - hi.astype(F32)).astype(BF16)
    return hi, lo


def _sigmoid(x):
    return 1.0 / (1.0 + jnp.exp(-x))


def _log_sigmoid(x):
    return jnp.minimum(x, 0.0) - jnp.log(1.0 + jnp.exp(-jnp.abs(x)))


def _mod_row(i, tm, n_ctx_rows, t_lat):
    r0 = i * tm
    return jnp.where(r0 < n_ctx_rows, 0, 1 + (r0 - n_ctx_rows) // t_lat)


def _store_token_major(ref, y):
    w = ref.shape[-1]
    for s in range(SUBLANES):
        ref[:, s, :] = y[:, s * w:(s + 1) * w]


def _load_token_major(ref, slot, r0, rows):
    return jnp.concatenate([ref[slot, r0:r0 + rows, s, :] for s in range(SUBLANES)], axis=1)


def _pack_bf16_pair(y):
    n = y.shape[1] // 2
    hi = lax.bitcast_convert_type(y[:, :n].astype(BF16).astype(F32), U32)
    lo = lax.bitcast_convert_type(y[:, n:].astype(BF16).astype(F32), U32)
    return hi | (lo >> 16)


def _unpack_bf16_pair(p):
    hi = lax.bitcast_convert_type(p & jnp.uint32(0xFFFF0000), F32)
    lo = lax.bitcast_convert_type(p << 16, F32)
    return hi, lo


def _ada_kernel(c_ref, w_ref, b_ref, o_ref):
    c = c_ref[...]
    s = c * _sigmoid(c)
    s_hi, s_lo = _split_bf16(s)
    w_hi, w_lo = _split_bf16(w_ref[...])
    o_ref[...] = _dot(s_hi, w_hi) + _dot(s_lo, w_hi) + _dot(s_hi, w_lo) + b_ref[...]


def _ada_mod(cvec, w_ada, b_ada):
    r, d = cvec.shape
    n = w_ada.shape[1]
    tn = _divisor(n, 512, LANES)
    return pl.pallas_call(
        _ada_kernel,
        grid=(n // tn,),
        in_specs=[pl.BlockSpec((r, d), lambda j: (0, 0)),
                  pl.BlockSpec((d, tn), lambda j: (0, j)),
                  pl.BlockSpec((1, tn), lambda j: (0, j))],
        out_specs=pl.BlockSpec((r, tn), lambda j: (0, j)),
        out_shape=jax.ShapeDtypeStruct((r, n), F32),
        compiler_params=_cparams("parallel"),
        name="ada_mod",
    )(cvec, w_ada, b_ada.reshape(1, n))


def _norm_mod_kernel(xc_ref, xl_ref, w_ref, mod_ref, o_ref, *, shift_idx, scale_idx, n_ctx_tiles):
    def body(x_ref):
        x = x_ref[...]
        y = x * lax.rsqrt(jnp.mean(x * x, axis=-1, keepdims=True) + EPS) * w_ref[...]
        y = y * (1.0 + mod_ref[0, scale_idx:scale_idx + 1, :]) + mod_ref[0, shift_idx:shift_idx + 1, :]
        o_ref[...] = y.astype(o_ref.dtype)

    i = pl.program_id(0)
    pl.when(i < n_ctx_tiles)(lambda: body(xc_ref))
    pl.when(i >= n_ctx_tiles)(lambda: body(xl_ref))


def _norm_mod(xc, xl, w, mod, t_lat, shift_idx, scale_idx, out_dtype):
    n_ctx_rows, d = xc.shape
    m = n_ctx_rows + xl.shape[0]
    tm = _divisor(math.gcd(n_ctx_rows, t_lat), 256)
    nct = n_ctx_rows // tm
    mrow = functools.partial(_mod_row, tm=tm, n_ctx_rows=n_ctx_rows, t_lat=t_lat)
    return pl.pallas_call(
        functools.partial(_norm_mod_kernel, shift_idx=shift_idx, scale_idx=scale_idx, n_ctx_tiles=nct),
        grid=(m // tm,),
        in_specs=[pl.BlockSpec((tm, d), lambda i: (jnp.minimum(i, nct - 1), 0)),
                  pl.BlockSpec((tm, d), lambda i: (jnp.maximum(i - nct, 0), 0)),
                  pl.BlockSpec((1, d), lambda i: (0, 0)),
                  pl.BlockSpec((1, mod.shape[1], d), lambda i: (mrow(i), 0, 0))],
        out_specs=pl.BlockSpec((tm, d), lambda i: (i, 0)),
        out_shape=jax.ShapeDtypeStruct((m, d), out_dtype),
        compiler_params=_cparams("arbitrary"),
        name="norm_mod",
    )(xc, xl, w.reshape(1, d), mod)


WPREP_ROWS = 512


def _win_prep_kernel(wt_hbm, o_ref, g_ref, buf, gbuf, sem, *, g0, n_gates):
    j = pl.program_id(0)
    nj = pl.num_programs(0)

    def rows_of(t):
        r0 = t * WPREP_ROWS
        return pl.multiple_of(jnp.where(r0 < g0, r0, r0 + n_gates), SUBLANES)

    def tile_copy(t, slot):
        return pltpu.make_async_copy(wt_hbm.at[pl.ds(rows_of(t), WPREP_ROWS), :], buf.at[slot], sem.at[slot])

    gate_copy = pltpu.make_async_copy(wt_hbm.at[pl.ds(g0, n_gates), :], gbuf, sem.at[2])

    @pl.when(j == 0)
    def _prime():
        gate_copy.start()
        tile_copy(0, 0).start()

    for slot in range(2):
        @pl.when(j % 2 == slot)
        def _work(slot=slot):
            @pl.when(j + 1 < nj)
            def _prefetch():
                tile_copy(j + 1, 1 - slot).start()

            tile_copy(j, slot).wait()
            o_ref[...] = buf[slot].T.astype(o_ref.dtype)

    @pl.when(j == 0)
    def _gates():
        gate_copy.wait()
        g = gbuf[...].astype(F32)
        g = jnp.concatenate([g, jnp.zeros((LANES - n_gates, g.shape[1]), F32)], axis=0)
        g_ref[...] = g.T.astype(g_ref.dtype)


def _win_prep(wt, g0, n_gates):
    n, k = wt.shape
    n_main = n - n_gates
    assert g0 % WPREP_ROWS == 0 and n_main % WPREP_ROWS == 0 and n_gates % SUBLANES == 0
    return pl.pallas_call(
        functools.partial(_win_prep_kernel, g0=g0, n_gates=n_gates),
        grid=(n_main // WPREP_ROWS,),
        in_specs=[pl.BlockSpec(memory_space=pl.ANY)],
        out_specs=[pl.BlockSpec((k, WPREP_ROWS), lambda j: (0, j)),
                   pl.BlockSpec((k, LANES), lambda j: (0, 0))],
        out_shape=[jax.ShapeDtypeStruct((k, n_main), BF16),
                   jax.ShapeDtypeStruct((k, LANES), BF16)],
        scratch_shapes=[pltpu.VMEM((2, WPREP_ROWS, k), wt.dtype), pltpu.VMEM((n_gates, k), wt.dtype),
                        pltpu.SemaphoreType.DMA((3,))],
        compiler_params=_cparams("arbitrary"),
        name="w_in_prep",
    )(wt)


def _inproj_kernel(a_ref, b_ref, wg_ref, z_ref, zg_ref):
    a = a_ref[...]
    z_ref[...] = _dot(a, b_ref[...]).astype(z_ref.dtype)

    @pl.when(pl.program_id(1) == 0)
    def _gates():
        zg_ref[...] = _dot(a, wg_ref[...])


def _inproj(h, w_main, w_gate):
    m, k = h.shape
    n = w_main.shape[1]
    tm = _divisor(m, 1024)
    tn = _divisor(n, 1024, LANES)
    return pl.pallas_call(
        _inproj_kernel,
        grid=(m // tm, n // tn),
        in_specs=[pl.BlockSpec((tm, k), lambda i, j: (i, 0)),
                  pl.BlockSpec((k, tn), lambda i, j: (0, j)),
                  pl.BlockSpec((k, LANES), lambda i, j: (0, 0))],
        out_specs=[pl.BlockSpec((tm, tn), lambda i, j: (i, j)),
                   pl.BlockSpec((tm, LANES), lambda i, j: (i, 0))],
        out_shape=[jax.ShapeDtypeStruct((m, n), BF16),
                   jax.ShapeDtypeStruct((m, LANES), F32)],
        compiler_params=_cparams("parallel", "arbitrary"),
        name="in_proj",
    )(h, w_main, w_gate)


CHUNKS_PER_STEP = 2


def _scan_schedule(n_ctx, t_ctx, n_lat, t_lat, reverse):
    rowblk, first, last, ctxb, latb, islat, posblk = [], [], [], [], [], [], []
    step_rows = CHUNK * CHUNKS_PER_STEP
    ns_ctx, ns_lat = t_ctx // step_rows, t_lat // step_rows
    for b in range(n_ctx):
        order = range(ns_ctx - 1, -1, -1) if reverse else range(ns_ctx)
        for pos, c in enumerate(order):
            rowblk.append(b * ns_ctx + c)
            first.append(int(pos == 0))
            last.append(int(pos == ns_ctx - 1))
            ctxb.append(b)
            latb.append(0)
            islat.append(0)
            posblk.append(ns_lat)
    base = n_ctx * ns_ctx
    for b in range(n_lat):
        order = range(ns_lat - 1, -1, -1) if reverse else range(ns_lat)
        for pos, c in enumerate(order):
            rowblk.append(base + b * ns_lat + c)
            first.append(int(pos == 0))
            last.append(int(pos == ns_lat - 1))
            ctxb.append(n_ctx - 1)
            latb.append(b)
            islat.append(1)
            posblk.append(c)
    tabs = (rowblk, first, last, ctxb, latb, islat, posblk)
    return tuple(jnp.asarray(np.asarray(t, np.int32)) for t in tabs)


def _chunk_rows(ci, reverse):
    cc = (CHUNKS_PER_STEP - 1 - ci) if reverse else ci
    return pl.ds(pl.multiple_of(cc * CHUNK, CHUNK), CHUNK)


def _mlstm_kernel(rowblk, first, last, ctxb, latb, islat, posblk,
                  q_ref, k_ref, v_ref, g_ref, gb_ref, c0_ref, n0_ref, m0_ref, *rest,
                  heads, dk, dv, reverse, combine):
    if combine:
        (hf_ref, o_ref, nw_ref, cf_ref, nf_ref, mf_ref,
         out_ref, cout_ref, nout_ref, mout_ref, c_s, n_s, m_s) = rest
    else:
        out_ref, cout_ref, nout_ref, mout_ref, c_s, n_s, m_s = rest
    w = pl.program_id(0)
    lat = islat[w] == 1

    @pl.when(first[w] == 1)
    def _init():
        c_s[...] = jnp.where(lat, c0_ref[0, 0], 0.0)
        n_s[...] = jnp.where(lat, n0_ref[0, 0], 0.0)
        m_s[...] = jnp.where(lat, m0_ref[0, 0], 0.0)

    L = CHUNK
    ti = lax.broadcasted_iota(I32, (L, L), 0)
    si = lax.broadcasted_iota(I32, (L, L), 1)
    mask = (si >= ti) if reverse else (si <= ti)
    mask_t = (ti >= si) if reverse else (ti <= si)
    scale = dk ** -0.5
    gate_row = 2 if reverse else 0
    hs_ = range(heads)

    def chunk(ci, carry):
        rows = _chunk_rows(ci, reverse)
        g = g_ref[rows, :] + gb_ref[...]
        gt = g.T
        lg = _log_sigmoid(g)
        lgt = _log_sigmoid(gt)

        st = []
        for h in hs_:
            ci_, cf_ = gate_row * heads + h, (gate_row + 1) * heads + h
            i_col, i_row = g[:, ci_:ci_ + 1], gt[ci_:ci_ + 1, :]
            lf_col, lf_row = lg[:, cf_:cf_ + 1], lgt[cf_:cf_ + 1, :]
            b_col = jnp.sum(jnp.where(mask, lf_row, 0.0), axis=1, keepdims=True)
            b_row = jnp.sum(jnp.where(mask_t, lf_col, 0.0), axis=0, keepdims=True)
            bl = jnp.sum(lf_row, axis=1, keepdims=True)
            m_old = m_s[h:h + 1, 0:1]
            dm = jnp.where(mask, b_col - b_row + i_row, NEG_BIG)
            inter = b_col + m_old
            mt = jnp.maximum(jnp.max(dm, axis=1, keepdims=True), inter)
            p = jnp.exp(dm - mt)
            wi = jnp.exp(inter - mt)
            g_col = bl - b_col + i_col
            g_row = bl - b_row + i_row
            m_new = jnp.maximum(bl + m_old, jnp.max(g_row, axis=1, keepdims=True))
            a_prev = jnp.exp(bl + m_old - m_new)
            a_col = jnp.exp(g_col - m_new) * scale
            st.append((p, wi, mt, m_new, a_prev, a_col))

        qbs = [q_ref[rows, h * dk:(h + 1) * dk] for h in hs_]
        kbs = [k_ref[rows, h * dk:(h + 1) * dk] for h in hs_]
        vbs = [v_ref[rows, h * dv:(h + 1) * dv] for h in hs_]
        ss = [_dot_nt(qbs[h], kbs[h]) * (scale * st[h][0]) for h in hs_]
        qcs = [_dot(qbs[h], c_s[h].astype(BF16)) for h in hs_]
        n_olds = [n_s[h:h + 1, :] for h in hs_]
        for h in hs_:
            p, wi, mt, m_new, a_prev, a_col = st[h]
            s_ = ss[h]
            num = _dot(s_.astype(BF16), vbs[h]) + wi * qcs[h]
            qn = jnp.sum(qbs[h].astype(F32) * n_olds[h], axis=1, keepdims=True)
            den = jnp.sum(s_, axis=1, keepdims=True) + wi * qn
            hval = num / jnp.maximum(jnp.abs(den), jnp.exp(-mt))
            if combine:
                hs = hval + hf_ref[rows, h * dv:(h + 1) * dv].astype(F32)
                y = hs * lax.rsqrt(jnp.mean(hs * hs, axis=-1, keepdims=True) + EPS)
                gate = o_ref[rows, h * dv:(h + 1) * dv].astype(F32)
                y = y * nw_ref[:, h * dv:(h + 1) * dv] * _sigmoid(gate)
                out_ref[rows, h * dv:(h + 1) * dv] = y.astype(out_ref.dtype)
            else:
                out_ref[rows, h * dv:(h + 1) * dv] = hval.astype(out_ref.dtype)
        for h in hs_:
            p, wi, mt, m_new, a_prev, a_col = st[h]
            ak = a_col * kbs[h].astype(F32)
            c_s[h] = a_prev * c_s[h] + _dot_tn(ak.astype(BF16), vbs[h])
            n_s[h:h + 1, :] = a_prev * n_olds[h] + jnp.sum(ak, axis=0, keepdims=True)
            m_s[h:h + 1, :] = jnp.broadcast_to(m_new, (1, LANES))
        return carry

    lax.fori_loop(0, CHUNKS_PER_STEP, chunk, 0)

    @pl.when(jnp.logical_and(last[w] == 1, jnp.logical_not(lat)))
    def _emit():
        if combine:
            cout_ref[0, 0] = cf_ref[0]
            nout_ref[0, 0] = nf_ref[0]
            mout_ref[0, 0] = mf_ref[0]
            cout_ref[0, 1] = c_s[...]
            nout_ref[0, 1] = n_s[...]
            mout_ref[0, 1] = m_s[...]
        else:
            cout_ref[0] = c_s[...]
            nout_ref[0] = n_s[...]
            mout_ref[0] = m_s[...]


def _mlstm_scan(z, zg, gbias, st_c, st_n, st_m, dims, reverse, fwd=None, norm_w=None):
    (n_ctx, t_ctx, n_lat, t_lat, heads, dk, dv) = dims
    combine = fwd is not None
    m = z.shape[0]
    d = 1 if reverse else 0
    step_rows = CHUNK * CHUNKS_PER_STEP
    tabs = _scan_schedule(n_ctx, t_ctx, n_lat, t_lat, reverse)
    qw, vw = heads * dk, heads * dv
    assert (2 * qw) % vw == 0
    v_blk = (2 * qw) // vw
    in_specs = [
        pl.BlockSpec((step_rows, qw), lambda w, rb, *_: (rb[w], 0)),
        pl.BlockSpec((step_rows, qw), lambda w, rb, *_: (rb[w], 1)),
        pl.BlockSpec((step_rows, vw), lambda w, rb, *_: (rb[w], v_blk)),
        pl.BlockSpec((step_rows, LANES), lambda w, rb, *_: (rb[w], 0)),
        pl.BlockSpec((1, LANES), lambda w, *_: (0, 0)),
        pl.BlockSpec((1, 1, heads, dk, dv), lambda w, rb, f, l, cb, lb, *_: (lb[w], d, 0, 0, 0)),
        pl.BlockSpec((1, 1, heads, dk), lambda w, rb, f, l, cb, lb, *_: (lb[w], d, 0, 0)),
        pl.BlockSpec((1, 1, heads, LANES), lambda w, rb, f, l, cb, lb, *_: (lb[w], d, 0, 0)),
    ]
    args = [z, z, z, zg, gbias, st_c, st_n, st_m]
    if combine:
        hf, cf, nf, mf = fwd
        in_specs += [
            pl.BlockSpec((step_rows, vw), lambda w, rb, *_: (rb[w], 0)),
            pl.BlockSpec((step_rows, vw), lambda w, rb, *_: (rb[w], v_blk + 1)),
            pl.BlockSpec((1, vw), lambda w, *_: (0, 0)),
            pl.BlockSpec((1, heads, dk, dv), lambda w, rb, f, l, cb, *_: (cb[w], 0, 0, 0)),
            pl.BlockSpec((1, heads, dk), lambda w, rb, f, l, cb, *_: (cb[w], 0, 0)),
            pl.BlockSpec((1, heads, LANES), lambda w, rb, f, l, cb, *_: (cb[w], 0, 0)),
        ]
        args += [hf, z, norm_w.reshape(1, vw), cf, nf, mf]
        nd = (2,)
        st_idx = lambda w, rb, f, l, cb, *_: (cb[w], 0, 0, 0)
        c_idx = lambda w, rb, f, l, cb, *_: (cb[w], 0, 0, 0, 0)
    else:
        nd = ()
        st_idx = lambda w, rb, f, l, cb, *_: (cb[w], 0, 0)
        c_idx = lambda w, rb, f, l, cb, *_: (cb[w], 0, 0, 0)
    out_specs = [
        pl.BlockSpec((step_rows, vw), lambda w, rb, *_: (rb[w], 0)),
        pl.BlockSpec((1,) + nd + (heads, dk, dv), c_idx),
        pl.BlockSpec((1,) + nd + (heads, dk), st_idx),
        pl.BlockSpec((1,) + nd + (heads, LANES), st_idx),
    ]
    out_shape = [
        jax.ShapeDtypeStruct((m, vw), BF16),
        jax.ShapeDtypeStruct((n_ctx,) + nd + (heads, dk, dv), F32),
        jax.ShapeDtypeStruct((n_ctx,) + nd + (heads, dk), F32),
        jax.ShapeDtypeStruct((n_ctx,) + nd + (heads, LANES), F32),
    ]
    return pl.pallas_call(
        functools.partial(_mlstm_kernel, heads=heads, dk=dk, dv=dv, reverse=reverse, combine=combine),
        grid_spec=pltpu.PrefetchScalarGridSpec(
            num_scalar_prefetch=len(tabs), grid=(m // step_rows,),
            in_specs=in_specs, out_specs=out_specs,
            scratch_shapes=[pltpu.VMEM((heads, dk, dv), F32),
                            pltpu.VMEM((heads, dk), F32),
                            pltpu.VMEM((heads, LANES), F32)]),
        out_shape=out_shape,
        compiler_params=_cparams("arbitrary"),
        name="mlstm_bwd" if reverse else "mlstm_fwd",
    )(*tabs, *args)


def _rope_partner(x, d):
    q, hlf = d // 4, d // 2
    return jnp.concatenate([x[:, q:hlf], x[:, :q], x[:, hlf + q:], x[:, hlf:hlf + q]], axis=1)


def _ret_kernel(rowblk, first, last, ctxb, latb, islat, posblk,
                q_ref, k_ref, v_ref, cos_ref, sin_ref, ld_ref, s0_ref, *rest,
                heads, dk, dv, reverse, combine):
    if combine:
        of_ref, g_ref, nw_ref, sf_ref, out_ref, sout_ref, s_s = rest
    else:
        out_ref, sout_ref, s_s = rest
    w = pl.program_id(0)
    lat = islat[w] == 1

    @pl.when(first[w] == 1)
    def _init():
        s_s[...] = jnp.where(lat, s0_ref[0, 0], 0.0)

    L = CHUNK
    ti = lax.broadcasted_iota(I32, (L, L), 0).astype(F32)
    si = lax.broadcasted_iota(I32, (L, L), 1).astype(F32)
    diff = (si - ti) if reverse else (ti - si)
    idx = lax.broadcasted_iota(I32, (L, 1), 0).astype(F32)
    q_pow = (L - idx) if reverse else (idx + 1.0)
    k_pow = idx if reverse else (L - 1.0 - idx)
    scale = dk ** -0.5
    drow = 1 if reverse else 0
    log_decay = -jnp.exp(ld_ref[drow:drow + 1, :])
    hs_ = range(heads)

    def chunk(ci, carry):
        rows = _chunk_rows(ci, reverse)
        cos_t, sin_t = cos_ref[rows, :], sin_ref[rows, :]
        qbs, kbs, kds, vbs = [], [], [], []
        for h in hs_:
            ld = log_decay[:, h:h + 1]
            q = q_ref[rows, h * dk:(h + 1) * dk].astype(F32)
            k = k_ref[rows, h * dk:(h + 1) * dk].astype(F32)
            q = q * cos_t + _rope_partner(q, dk) * sin_t
            k = k * cos_t + _rope_partner(k, dk) * sin_t
            qbs.append(q.astype(BF16))
            kbs.append(k.astype(BF16))
            kds.append((k * (scale * jnp.exp(k_pow * ld))).astype(BF16))
            vbs.append(v_ref[rows, h * dv:(h + 1) * dv])
        scores = [_dot_nt(qbs[h], kbs[h]) for h in hs_]
        inter = [_dot(qbs[h], s_s[h].astype(BF16)) for h in hs_]
        for h in hs_:
            ld = log_decay[:, h:h + 1]
            intra = jnp.where(diff >= 0.0, jnp.exp(jnp.maximum(diff, 0.0) * ld), 0.0)
            a = scores[h] * (scale * intra)
            o = _dot(a.astype(BF16), vbs[h]) + inter[h] * jnp.exp(q_pow * ld)
            if combine:
                hs = o + of_ref[rows, h * dv:(h + 1) * dv].astype(F32)
                y = hs * lax.rsqrt(jnp.mean(hs * hs, axis=-1, keepdims=True) + EPS)
                gate = g_ref[rows, h * dv:(h + 1) * dv].astype(F32)
                y = y * nw_ref[:, h * dv:(h + 1) * dv] * (gate * _sigmoid(gate))
                out_ref[rows, h * dv:(h + 1) * dv] = y.astype(out_ref.dtype)
            else:
                out_ref[rows, h * dv:(h + 1) * dv] = o.astype(out_ref.dtype)
        for h in hs_:
            ld = log_decay[:, h:h + 1]
            s_s[h] = jnp.exp(float(L) * ld) * s_s[h] + _dot_tn(kds[h], vbs[h])
        return carry

    lax.fori_loop(0, CHUNKS_PER_STEP, chunk, 0)

    @pl.when(jnp.logical_and(last[w] == 1, jnp.logical_not(lat)))
    def _emit():
        if combine:
            sout_ref[0, 0] = sf_ref[0]
            sout_ref[0, 1] = s_s[...]
        else:
            sout_ref[0] = s_s[...]


def _ret_scan(z, cos_tab, sin_tab, ld, st_s, dims, col0, reverse, fwd=None, norm_w=None):
    (n_ctx, t_ctx, n_lat, t_lat, heads, dk, dv) = dims
    combine = fwd is not None
    m = z.shape[0]
    d = 1 if reverse else 0
    step_rows = CHUNK * CHUNKS_PER_STEP
    tabs = _scan_schedule(n_ctx, t_ctx, n_lat, t_lat, reverse)
    qw, vw = heads * dk, heads * dv
    assert qw == vw and col0 % qw == 0
    b0 = col0 // qw
    in_specs = [
        pl.BlockSpec((step_rows, qw), lambda w, rb, *_: (rb[w], b0)),
        pl.BlockSpec((step_rows, qw), lambda w, rb, *_: (rb[w], b0 + 1)),
        pl.BlockSpec((step_rows, vw), lambda w, rb, *_: (rb[w], b0 + 2)),
        pl.BlockSpec((step_rows, dk), lambda w, rb, f, l, cb, lb, il, pb: (pb[w], 0)),
        pl.BlockSpec((step_rows, dk), lambda w, rb, f, l, cb, lb, il, pb: (pb[w], 0)),
        pl.BlockSpec((SUBLANES, LANES), lambda w, *_: (0, 0)),
        pl.BlockSpec((1, 1, heads, dk, dv), lambda w, rb, f, l, cb, lb, *_: (lb[w], d, 0, 0, 0)),
    ]
    args = [z, z, z, cos_tab, sin_tab, ld, st_s]
    if combine:
        of, sf = fwd
        in_specs += [
            pl.BlockSpec((step_rows, vw), lambda w, rb, *_: (rb[w], 0)),
            pl.BlockSpec((step_rows, vw), lambda w, rb, *_: (rb[w], b0 + 3)),
            pl.BlockSpec((1, vw), lambda w, *_: (0, 0)),
            pl.BlockSpec((1, heads, dk, dv), lambda w, rb, f, l, cb, *_: (cb[w], 0, 0, 0)),
        ]
        args += [of, z, norm_w.reshape(1, vw), sf]
        s_spec = pl.BlockSpec((1, 2, heads, dk, dv), lambda w, rb, f, l, cb, *_: (cb[w], 0, 0, 0, 0))
        s_shape = jax.ShapeDtypeStruct((n_ctx, 2, heads, dk, dv), F32)
    else:
        s_spec = pl.BlockSpec((1, heads, dk, dv), lambda w, rb, f, l, cb, *_: (cb[w], 0, 0, 0))
        s_shape = jax.ShapeDtypeStruct((n_ctx, heads, dk, dv), F32)
    return pl.pallas_call(
        functools.partial(_ret_kernel, heads=heads, dk=dk, dv=dv, reverse=reverse, combine=combine),
        grid_spec=pltpu.PrefetchScalarGridSpec(
            num_scalar_prefetch=len(tabs), grid=(m // step_rows,),
            in_specs=in_specs,
            out_specs=[pl.BlockSpec((step_rows, vw), lambda w, rb, *_: (rb[w], 0)), s_spec],
            scratch_shapes=[pltpu.VMEM((heads, dk, dv), F32)]),
        out_shape=[jax.ShapeDtypeStruct((m, vw), BF16), s_shape],
        compiler_params=_cparams("arbitrary"),
        name="ret_bwd" if reverse else "ret_fwd",
    )(*tabs, *args)


def _rope_tables(t_lat, d):
    quarter = d // 4
    rows = t_lat // GRID_W
    row = jnp.repeat(jnp.arange(rows, dtype=F32), GRID_W)
    col = jnp.tile(jnp.arange(GRID_W, dtype=F32), rows)
    inv = ROPE_BASE ** (-jnp.arange(quarter, dtype=F32) / quarter)
    ar = row[:, None] * inv[None, :]
    ac = col[:, None] * inv[None, :]
    cos_t = jnp.concatenate([jnp.cos(ar), jnp.cos(ar), jnp.cos(ac), jnp.cos(ac)], axis=1)
    sin_t = jnp.concatenate([-jnp.sin(ar), jnp.sin(ar), -jnp.sin(ac), jnp.sin(ac)], axis=1)
    pad_rows = CHUNK * CHUNKS_PER_STEP
    cos_t = jnp.concatenate([cos_t, jnp.ones((pad_rows, d), F32)], axis=0)
    sin_t = jnp.concatenate([sin_t, jnp.zeros((pad_rows, d), F32)], axis=0)
    return cos_t, sin_t


def _outproj_kernel(a1_ref, a2_ref, b1_ref, b2_ref, xc_ref, xl_ref, mod_ref, o_ref, *, gate_idx, n_ctx_tiles):
    acc = _dot(a1_ref[...], b1_ref[...]) + _dot(a2_ref[...], b2_ref[...])
    upd = mod_ref[0, gate_idx:gate_idx + 1, :] * acc
    i = pl.program_id(0)

    @pl.when(i < n_ctx_tiles)
    def _ctx():
        o_ref[...] = xc_ref[...] + upd

    @pl.when(i >= n_ctx_tiles)
    def _lat():
        o_ref[...] = xl_ref[...] + upd


def _outproj(a1, a2, w_out, xc, xl, mod, t_lat, gate_idx):
    m, k1 = a1.shape
    k2 = a2.shape[1]
    n = w_out.shape[1]
    n_ctx_rows = xc.shape[0]
    assert k1 == k2
    tm = _divisor(math.gcd(n_ctx_rows, t_lat), 1024)
    tn = _divisor(n, 512, LANES)
    nct = n_ctx_rows // tm
    mrow = functools.partial(_mod_row, tm=tm, n_ctx_rows=n_ctx_rows, t_lat=t_lat)
    return pl.pallas_call(
        functools.partial(_outproj_kernel, gate_idx=gate_idx, n_ctx_tiles=nct),
        grid=(m // tm, n // tn),
        in_specs=[pl.BlockSpec((tm, k1), lambda i, j: (i, 0)),
                  pl.BlockSpec((tm, k2), lambda i, j: (i, 0)),
                  pl.BlockSpec((k1, tn), lambda i, j: (0, j)),
                  pl.BlockSpec((k2, tn), lambda i, j: (1, j)),
                  pl.BlockSpec((tm, tn), lambda i, j: (jnp.minimum(i, nct - 1), jnp.where(i < nct, j, 0))),
                  pl.BlockSpec((tm, tn), lambda i, j: (jnp.maximum(i - nct, 0), jnp.where(i >= nct, j, 0))),
                  pl.BlockSpec((1, mod.shape[1], tn), lambda i, j: (mrow(i), 0, j))],
        out_specs=pl.BlockSpec((tm, tn), lambda i, j: (i, j)),
        out_shape=jax.ShapeDtypeStruct((m, n), F32),
        compiler_params=_cparams("arbitrary", "arbitrary"),
        name="out_proj",
    )(a1, a2, w_out, w_out, xc, xl, mod)


def _router_kernel(x_ref, w_ref, mod_ref, wr_ref, br_ref, h_ref, eid_ref, wt_ref,
                   *, shift_idx, scale_idx, n_groups, per_group):
    x = x_ref[...]
    y = x * lax.rsqrt(jnp.mean(x * x, axis=-1, keepdims=True) + EPS) * w_ref[...]
    y = y * (1.0 + mod_ref[0, scale_idx:scale_idx + 1, :]) + mod_ref[0, shift_idx:shift_idx + 1, :]
    _store_token_major(h_ref, _pack_bf16_pair(y))
    logits = _dot(y.astype(BF16), wr_ref[...]) + br_ref[...]

    n_exp = n_groups * per_group
    lane = lax.broadcasted_iota(I32, logits.shape, 1)
    gmask = lane < n_groups
    gl = jnp.where(gmask, logits, NEG_BIG)
    gmax = jnp.max(gl, axis=1, keepdims=True)
    gsum = jnp.sum(jnp.where(gmask, jnp.exp(gl - gmax), 0.0), axis=1, keepdims=True)
    g_w = 1.0 / gsum
    g_idx = jnp.min(jnp.where(gl == gmax, lane, LANES), axis=1, keepdims=True)

    in_group = jnp.logical_and(lane >= n_groups + g_idx * per_group,
                               lane < n_groups + (g_idx + 1) * per_group)
    in_group = jnp.logical_and(in_group, lane < n_groups + n_exp)
    el = jnp.where(in_group, logits, NEG_BIG)
    m1 = jnp.max(el, axis=1, keepdims=True)
    i1 = jnp.min(jnp.where(el == m1, lane, LANES), axis=1, keepdims=True)
    el2 = jnp.where(lane == i1, NEG_BIG, el)
    m2 = jnp.max(el2, axis=1, keepdims=True)
    i2 = jnp.min(jnp.where(el2 == m2, lane, LANES), axis=1, keepdims=True)
    e2 = jnp.exp(m2 - m1)
    p1 = 1.0 / (1.0 + e2)
    p2 = e2 * p1
    eid_ref[...] = jnp.where(lane == 0, i1 - n_groups, jnp.where(lane == 1, i2 - n_groups, 0))
    wt_ref[...] = jnp.where(lane == 0, g_w * p1, jnp.where(lane == 1, g_w * p2, 0.0))


def _router(x, w, mod, w_r, b_r, n_ctx_rows, t_lat, shift_idx, scale_idx, n_groups, per_group):
    m, d = x.shape
    tm = _divisor(math.gcd(n_ctx_rows, t_lat), 256)
    mrow = functools.partial(_mod_row, tm=tm, n_ctx_rows=n_ctx_rows, t_lat=t_lat)
    return pl.pallas_call(
        functools.partial(_router_kernel, shift_idx=shift_idx, scale_idx=scale_idx,
                          n_groups=n_groups, per_group=per_group),
        grid=(m // tm,),
        in_specs=[pl.BlockSpec((tm, d), lambda i: (i, 0)),
                  pl.BlockSpec((1, d), lambda i: (0, 0)),
                  pl.BlockSpec((1, mod.shape[1], d), lambda i: (mrow(i), 0, 0)),
                  pl.BlockSpec((d, LANES), lambda i: (0, 0)),
                  pl.BlockSpec((1, LANES), lambda i: (0, 0))],
        out_specs=[pl.BlockSpec((tm, SUBLANES, d // (2 * SUBLANES)), lambda i: (i, 0, 0)),
                   pl.BlockSpec((tm, LANES), lambda i: (i, 0)),
                   pl.BlockSpec((tm, LANES), lambda i: (i, 0))],
        out_shape=[jax.ShapeDtypeStruct((m, SUBLANES, d // (2 * SUBLANES)), U32),
                   jax.ShapeDtypeStruct((m, LANES), I32),
                   jax.ShapeDtypeStruct((m, LANES), F32)],
        compiler_params=_cparams("parallel"),
        name="router",
    )(x, w.reshape(1, d), mod, w_r, b_r)


GATHER_UNROLL = 8


def _row_copy(src_hbm, dst_buf, sem, src_row, slot, dst_row):
    return pltpu.make_async_copy(src_hbm.at[src_row], dst_buf.at[slot, dst_row], sem.at[slot])


def _start_row_gather(tok_fn, src_hbm, buf, sem, slot, n_idx):
    def body(jj, carry):
        for u in range(GATHER_UNROLL):
            j = jj * GATHER_UNROLL + u
            _row_copy(src_hbm, buf, sem, tok_fn(j), slot, j).start(priority=u % 2)
        return carry

    lax.fori_loop(0, n_idx // GATHER_UNROLL, body, 0)


def _wait_row_gather(src_hbm, buf, sem, slot, n_idx):
    def body(j, carry):
        _row_copy(src_hbm, buf, sem, 0, slot, j).wait()
        return carry

    lax.fori_loop(0, n_idx, body, 0, unroll=GATHER_UNROLL)


def _ring_step(cur_fn, nxt_fn, src_hbm, buf, sem, n_idx, consume):
    i = pl.program_id(0)

    @pl.when(i == 0)
    def _prime():
        _start_row_gather(cur_fn, src_hbm, buf, sem, 0, n_idx)

    for slot in range(2):
        @pl.when(i % 2 == slot)
        def _work(slot=slot):
            @pl.when(i + 1 < pl.num_programs(0))
            def _prefetch():
                _start_row_gather(nxt_fn, src_hbm, buf, sem, 1 - slot, n_idx)

            _wait_row_gather(src_hbm, buf, sem, slot, n_idx)
            consume(slot)


def _scatter_rows_kernel(cnt, dst_ref, h_ref, xs_hbm, zero, sem, *, rows, blk_rows):
    n_blk = cnt.shape[0]

    def pad_copy(b, r):
        return pltpu.make_async_copy(zero.at[0], xs_hbm.at[b * blk_rows + r], sem.at[1])

    def for_each_pad(fn):
        def blk(b, carry):
            def row(r, c2):
                fn(pad_copy(b, r))
                return c2
            return lax.fori_loop(cnt[b], blk_rows, row, carry)
        lax.fori_loop(0, n_blk, blk, 0)

    @pl.when(pl.program_id(0) == 0)
    def _pad():
        zero[...] = jnp.zeros(zero.shape, zero.dtype)
        for_each_pad(lambda cp: cp.start())
        for_each_pad(lambda cp: cp.wait())

    def row_copy(j, r):
        return pltpu.make_async_copy(h_ref.at[r], xs_hbm.at[dst_ref[0, 0, j]], sem.at[0])

    def issue(rr, carry):
        for u in range(GATHER_UNROLL // TOP_K):
            r = rr * (GATHER_UNROLL // TOP_K) + u
            for kk in range(TOP_K):
                row_copy(r * TOP_K + kk, r).start(priority=kk % 2)
        return carry

    lax.fori_loop(0, rows * TOP_K // GATHER_UNROLL, issue, 0)

    def drain(r, carry):
        for kk in range(TOP_K):
            row_copy(r * TOP_K + kk, r).wait()
        return carry

    lax.fori_loop(0, rows, drain, 0, unroll=GATHER_UNROLL // TOP_K)


def _scatter_rows(src, dest, blk_cnt, blk_rows):
    t = src.shape[0]
    rows = _divisor(t, 256)
    n_slots = blk_cnt.shape[0] * blk_rows
    return pl.pallas_call(
        functools.partial(_scatter_rows_kernel, rows=rows, blk_rows=blk_rows),
        grid_spec=pltpu.PrefetchScalarGridSpec(
            num_scalar_prefetch=1, grid=(t // rows,),
            in_specs=[pl.BlockSpec((1, 1, TOP_K * rows), lambda i, *_: (i, 0, 0), memory_space=pltpu.SMEM),
                      pl.BlockSpec((rows,) + src.shape[1:], lambda i, *_: (i, 0, 0))],
            out_specs=pl.BlockSpec(memory_space=pl.ANY),
            scratch_shapes=[pltpu.VMEM((1,) + src.shape[1:], src.dtype), pltpu.SemaphoreType.DMA((2,))]),
        out_shape=jax.ShapeDtypeStruct((n_slots,) + src.shape[1:], src.dtype),
        compiler_params=_cparams("arbitrary"),
        name="moe_dispatch",
    )(blk_cnt, dest.reshape(t // rows, 1, TOP_K * rows), src)


def _moe_up_kernel(xblk, eid, wjt, oblk, ojt, fst, valid, ring, has_nxt, nxt_e, nxt_j,
                   x_ref, wg_hbm, wu_hbm, h_ref, wbuf, wg_s, wu_s, flat, sem, *, tn):
    w = pl.program_id(0)

    def wcopy(e, j, slot, which):
        src = wu_hbm if which else wg_hbm
        cols = pl.ds(pl.multiple_of(j * tn, tn), tn)
        return pltpu.make_async_copy(src.at[e, :, cols], wbuf.at[slot, which], sem.at[slot, which])

    @pl.when(w == 0)
    def _prime():
        for which in range(2):
            wcopy(eid[0], wjt[0], ring[0], which).start()

    @pl.when(valid[w] == 1)
    def _go():
        @pl.when(fst[w] == 1)
        def _swap():
            slot = ring[w]

            @pl.when(has_nxt[w] == 1)
            def _request():
                for which in range(2):
                    wcopy(nxt_e[w], nxt_j[w], 1 - slot, which).start()

            for which in range(2):
                wcopy(eid[w], wjt[w], slot, which).wait()
            wg_s[...] = wbuf[slot, 0].astype(BF16)
            wu_s[...] = wbuf[slot, 1].astype(BF16)

        wd = x_ref.shape[-1]
        for sl in range(SUBLANES):
            flat[:, sl * wd:(sl + 1) * wd] = x_ref[:, sl, :]
        hi, lo = _unpack_bf16_pair(flat[...])
        xh, xl = hi.astype(BF16), lo.astype(BF16)
        n = xh.shape[1]
        a = _dot(xh, wg_s[:n, :]) + _dot(xl, wg_s[n:, :])
        b = _dot(xh, wu_s[:n, :]) + _dot(xl, wu_s[n:, :])
        h_ref[...] = (a * _sigmoid(a) * b).astype(h_ref.dtype)

    @pl.when(valid[w] == 0)
    def _pad():
        h_ref[...] = jnp.zeros(h_ref.shape, h_ref.dtype)


def _moe_up(xs, w_gate, w_up, tabs, n_work, tn):
    n_slots = xs.shape[0]
    d = w_gate.shape[1]
    f = w_gate.shape[2]
    return pl.pallas_call(
        functools.partial(_moe_up_kernel, tn=tn),
        grid_spec=pltpu.PrefetchScalarGridSpec(
            num_scalar_prefetch=len(tabs), grid=(n_work,),
            in_specs=[pl.BlockSpec((MOE_TILE,) + xs.shape[1:], lambda w, xb, *_: (xb[w], 0, 0)),
                      pl.BlockSpec(memory_space=pl.ANY),
                      pl.BlockSpec(memory_space=pl.ANY)],
            out_specs=pl.BlockSpec((MOE_TILE, tn), lambda w, xb, e, wj, ob, oj, *_: (ob[w], oj[w])),
            scratch_shapes=[pltpu.VMEM((2, 2, d, tn), w_gate.dtype),
                            pltpu.VMEM((d, tn), BF16), pltpu.VMEM((d, tn), BF16),
                            pltpu.VMEM((MOE_TILE, xs.shape[1] * xs.shape[2]), xs.dtype),
                            pltpu.SemaphoreType.DMA((2, 2))]),
        out_shape=jax.ShapeDtypeStruct((n_slots, f), BF16),
        compiler_params=_cparams("arbitrary"),
        name="moe_up",
    )(*tabs, xs, w_gate, w_up)


def _moe_down_kernel(eid, fst, valid, ring, has_nxt, nxt_e, h_ref, wd_hbm, y_ref, wbuf, wd_s, sem):
    w = pl.program_id(0)

    def wcopy(e, slot):
        return pltpu.make_async_copy(wd_hbm.at[e], wbuf.at[slot], sem.at[slot])

    @pl.when(w == 0)
    def _prime():
        wcopy(eid[0], ring[0]).start()

    @pl.when(valid[w] == 1)
    def _go():
        @pl.when(fst[w] == 1)
        def _swap():
            slot = ring[w]

            @pl.when(has_nxt[w] == 1)
            def _request():
                wcopy(nxt_e[w], 1 - slot).start()

            wcopy(eid[w], slot).wait()
            wd_s[...] = wbuf[slot].astype(BF16)

        _store_token_major(y_ref, _pack_bf16_pair(_dot(h_ref[...], wd_s[...])))

    @pl.when(valid[w] == 0)
    def _pad():
        y_ref[...] = jnp.zeros(y_ref.shape, y_ref.dtype)


def _moe_down(hs, w_down, tabs):
    n_slots, f = hs.shape
    d = w_down.shape[2]
    nblk = n_slots // MOE_TILE
    return pl.pallas_call(
        _moe_down_kernel,
        grid_spec=pltpu.PrefetchScalarGridSpec(
            num_scalar_prefetch=len(tabs), grid=(nblk,),
            in_specs=[pl.BlockSpec((MOE_TILE, f), lambda w, *_: (w, 0)),
                      pl.BlockSpec(memory_space=pl.ANY)],
            out_specs=pl.BlockSpec((MOE_TILE, SUBLANES, d // (2 * SUBLANES)), lambda w, *_: (w, 0, 0)),
            scratch_shapes=[pltpu.VMEM((2, f, d), w_down.dtype), pltpu.VMEM((f, d), BF16),
                            pltpu.SemaphoreType.DMA((2,))]),
        out_shape=jax.ShapeDtypeStruct((n_slots, SUBLANES, d // (2 * SUBLANES)), U32),
        compiler_params=_cparams("arbitrary"),
        name="moe_down",
    )(*tabs, hs, w_down)


def _final_kernel(dst_ref, nxt_ref, x_ref, wt_ref, mod_ref, fw_ref, ys_hbm, o_ref, buf, sem, *, rows, gate_idx):
    def consume(slot):
        wt = wt_ref[...]
        moe = None
        for kk in range(TOP_K):
            hi, lo = _unpack_bf16_pair(_load_token_major(buf, slot, kk * rows, rows))
            term = wt[:, kk:kk + 1] * jnp.concatenate([hi, lo], axis=1)
            moe = term if moe is None else moe + term
        x = x_ref[...] + mod_ref[0, gate_idx:gate_idx + 1, :] * moe
        o_ref[...] = x * lax.rsqrt(jnp.mean(x * x, axis=-1, keepdims=True) + EPS) * fw_ref[...]

    _ring_step(lambda j: dst_ref[0, 0, j], lambda j: nxt_ref[0, 0, j], ys_hbm, buf, sem, TOP_K * rows, consume)


def _final(x1, wt, dest, ys, mod, final_w, row0, n_rows, n_ctx_rows, t_lat, gate_idx):
    d = x1.shape[1]
    tm = _divisor(math.gcd(n_ctx_rows, t_lat), 128)
    t0 = row0 // tm
    nt = n_rows // tm
    mrow = functools.partial(_mod_row, tm=tm, n_ctx_rows=n_ctx_rows, t_lat=t_lat)
    dest3 = dest.reshape(-1, tm, TOP_K).transpose(0, 2, 1).reshape(-1, 1, TOP_K * tm)
    return pl.pallas_call(
        functools.partial(_final_kernel, rows=tm, gate_idx=gate_idx),
        grid=(nt,),
        in_specs=[pl.BlockSpec((1, 1, TOP_K * tm), lambda i: (t0 + i, 0, 0), memory_space=pltpu.SMEM),
                  pl.BlockSpec((1, 1, TOP_K * tm), lambda i: (t0 + jnp.minimum(i + 1, nt - 1), 0, 0),
                               memory_space=pltpu.SMEM),
                  pl.BlockSpec((tm, d), lambda i: (t0 + i, 0)),
                  pl.BlockSpec((tm, LANES), lambda i: (t0 + i, 0)),
                  pl.BlockSpec((1, mod.shape[1], d), lambda i: (mrow(t0 + i), 0, 0)),
                  pl.BlockSpec((1, d), lambda i: (0, 0)),
                  pl.BlockSpec(memory_space=pl.ANY)],
        out_specs=pl.BlockSpec((tm, d), lambda i: (i, 0)),
        out_shape=jax.ShapeDtypeStruct((n_rows, d), F32),
        scratch_shapes=[pltpu.VMEM((2, TOP_K * tm) + ys.shape[1:], ys.dtype), pltpu.SemaphoreType.DMA((2,))],
        compiler_params=_cparams("arbitrary"),
        name="moe_combine_final",
    )(dest3, dest3, x1, wt, mod, final_w.reshape(1, d), ys)


def _count_le(sorted_ends, idx):
    return jnp.sum((sorted_ends[None, :] <= idx[:, None]).astype(I32), axis=1)


def _moe_plan(eid, n_exp, n_jt):
    t = eid.shape[0]
    n_assign = t * TOP_K
    eflat = eid.reshape(n_assign)
    onehot = (eflat[:, None] == jnp.arange(n_exp, dtype=I32)[None, :]).astype(I32)
    csum = jnp.cumsum(onehot, axis=0)
    counts = csum[-1]
    nb = (counts + MOE_TILE - 1) // MOE_TILE
    blk_end = jnp.cumsum(nb)
    blk_start = blk_end - nb
    n_blocks = blk_end[-1]
    pad_start = blk_start * MOE_TILE
    dest = jnp.sum(onehot * (csum - 1 + pad_start[None, :]), axis=1)

    assert n_assign % MOE_TILE == 0
    nblk_max = n_assign // MOE_TILE + n_exp
    b_idx = jnp.arange(nblk_max, dtype=I32)
    blk_e = jnp.minimum(_count_le(blk_end, b_idx), n_exp - 1)
    blk_off = (b_idx - blk_start[blk_e]) * MOE_TILE
    blk_cnt = jnp.where(b_idx < n_blocks, jnp.clip(counts[blk_e] - blk_off, 0, MOE_TILE), 0).astype(I32)

    d_valid = (b_idx < n_blocks).astype(I32)
    last_e = blk_e[jnp.maximum(n_blocks - 1, 0)]
    d_eid = jnp.where(d_valid == 1, blk_e, last_e).astype(I32)
    d_first = jnp.logical_and(d_valid == 1, b_idx == blk_start[d_eid]).astype(I32)
    has_blk = (nb > 0).astype(I32)
    e_ord = jnp.cumsum(has_blk) - 1
    d_ring = (e_ord[d_eid] % 2).astype(I32)
    nxt_b = jnp.minimum(blk_end[d_eid], nblk_max - 1)
    d_has_nxt = (blk_end[d_eid] < n_blocks).astype(I32)
    d_nxt_e = blk_e[nxt_b].astype(I32)

    n_work = n_jt * nblk_max
    w_idx = jnp.arange(n_work, dtype=I32)
    per_e = n_jt * nb
    w_end = jnp.cumsum(per_e)
    w_valid = w_idx < w_end[-1]
    we = jnp.minimum(_count_le(w_end, w_idx), n_exp - 1)
    r = w_idx - (w_end[we] - per_e[we])
    nbe = jnp.maximum(nb[we], 1)
    u_jt = r // nbe
    u_t = r - u_jt * nbe
    u_blk = blk_start[we] + u_t
    last_w = jnp.maximum(w_end[-1] - 1, 0)
    spare = w_idx - w_end[-1]
    u_xblk = jnp.where(w_valid, u_blk, u_blk[last_w]).astype(I32)
    u_eid = jnp.where(w_valid, we, we[last_w]).astype(I32)
    u_wjt = jnp.where(w_valid, u_jt, u_jt[last_w]).astype(I32)
    u_oblk = jnp.where(w_valid, u_blk, n_blocks + spare // n_jt).astype(I32)
    u_ojt = jnp.where(w_valid, u_jt, spare % n_jt).astype(I32)
    u_first = jnp.logical_and(w_valid, u_t == 0).astype(I32)
    g_ord = jnp.cumsum(u_first) - 1
    u_ring = (g_ord % 2).astype(I32)
    nxt_w = jnp.minimum(w_idx - u_t + nbe, n_work - 1)
    u_has_nxt = jnp.logical_and(w_valid, w_idx - u_t + nbe < w_end[-1]).astype(I32)
    u_nxt_e = we[nxt_w].astype(I32)
    u_nxt_j = u_jt[nxt_w].astype(I32)
    up_tabs = (u_xblk, u_eid, u_wjt, u_oblk, u_ojt, u_first, w_valid.astype(I32), u_ring, u_has_nxt, u_nxt_e, u_nxt_j)
    down_tabs = (d_eid, d_first, d_valid, d_ring, d_has_nxt, d_nxt_e)
    return dest.astype(I32), blk_cnt, up_tabs, down_tabs, n_work


def kernel(x_prompt, x_sample, state_mlstm_C, state_mlstm_n, state_mlstm_m, state_ret_S, c, c_ctx,
           norm1_w, norm2_w, w_ada, b_ada, w_in, b_mgates, mlstm_norm_w, ret_norm_w, ret_log_decay, w_out,
           w_router_group, b_router_group, w_router_expert, b_router_expert, w_exp_gate, w_exp_up,
           w_exp_down, final_norm_w):
    n_ctx, t_ctx, d_model = x_prompt.shape
    n_lat, t_lat, _ = x_sample.shape
    depth = norm1_w.shape[0]
    assert depth == 1, "single-layer trunk"
    m_heads, m_dk, m_dv = state_mlstm_C.shape[3:]
    r_heads, r_dk, r_dv = state_ret_S.shape[3:]
    n_groups = w_router_group.shape[2]
    n_exp = w_router_expert.shape[2]
    per_group = n_exp // n_groups
    d_exp = w_exp_gate.shape[3]
    n_gates = N_GATE_ROWS * m_heads
    n_ctx_rows = n_ctx * t_ctx
    step_rows = CHUNK * CHUNKS_PER_STEP
    assert t_ctx % step_rows == 0 and t_lat % step_rows == 0 and n_gates <= LANES
    assert n_groups + n_exp <= LANES

    cvec = jnp.concatenate([c_ctx[None, :], c, jnp.zeros((SUBLANES - 1 - n_lat, d_model), F32)], axis=0)
    mod = _ada_mod(cvec, w_ada[0], b_ada[0]).reshape(SUBLANES, 6, d_model)

    xc = x_prompt.reshape(n_ctx_rows, d_model)
    xl = x_sample.reshape(n_lat * t_lat, d_model)
    h = _norm_mod(xc, xl, norm1_w[0], mod, t_lat, 0, 1, BF16)

    g0 = 2 * m_heads * m_dk + 2 * m_heads * m_dv
    w_main, w_gate = _win_prep(jnp.swapaxes(w_in[0], 0, 1), g0, n_gates)
    z, zg = _inproj(h, w_main, w_gate)

    gbias = jnp.pad(b_mgates[0].reshape(1, n_gates), ((0, 0), (0, LANES - n_gates)))
    st_m = jnp.broadcast_to(state_mlstm_m[:, 0][..., None], (n_lat, 2, m_heads, LANES))
    mdims = (n_ctx, t_ctx, n_lat, t_lat, m_heads, m_dk, m_dv)
    fwd_m = _mlstm_scan(z, zg, gbias, state_mlstm_C[:, 0], state_mlstm_n[:, 0], st_m, mdims, False)
    mix_m, new_c, new_n, new_m = _mlstm_scan(z, zg, gbias, state_mlstm_C[:, 0], state_mlstm_n[:, 0], st_m, mdims,
                                             True, fwd=fwd_m, norm_w=mlstm_norm_w[0])

    cos_tab, sin_tab = _rope_tables(t_lat, r_dk)
    ld = jnp.pad(ret_log_decay[0], ((0, SUBLANES - 2), (0, LANES - r_heads)))
    rdims = (n_ctx, t_ctx, n_lat, t_lat, r_heads, r_dk, r_dv)
    fwd_r = _ret_scan(z, cos_tab, sin_tab, ld, state_ret_S[:, 0], rdims, g0, False)
    mix_r, new_s = _ret_scan(z, cos_tab, sin_tab, ld, state_ret_S[:, 0], rdims, g0, True,
                             fwd=fwd_r, norm_w=ret_norm_w[0])

    x1 = _outproj(mix_m, mix_r, w_out[0].astype(BF16), xc, xl, mod, t_lat, 2)

    w_r = jnp.pad(jnp.concatenate([w_router_group[0], w_router_expert[0]], axis=1),
                  ((0, 0), (0, LANES - n_groups - n_exp)))
    b_r = jnp.pad(jnp.concatenate([b_router_group[0], b_router_expert[0]])[None, :],
                  ((0, 0), (0, LANES - n_groups - n_exp)))
    h2, eid, wt = _router(x1, norm2_w[0], mod, w_r.astype(BF16), b_r, n_ctx_rows, t_lat, 3, 4, n_groups, per_group)

    tn_up = _divisor(d_exp, 512, LANES)
    dest, blk_cnt, up_tabs, down_tabs, n_work = _moe_plan(eid[:, :TOP_K], n_exp, d_exp // tn_up)
    xs = _scatter_rows(h2, dest, blk_cnt, MOE_TILE)
    hs = _moe_up(xs, w_exp_gate[0], w_exp_up[0], up_tabs, n_work, tn_up)
    ys = _moe_down(hs, w_exp_down[0], down_tabs)

    y_ctx = _final(x1, wt, dest, ys, mod, final_norm_w, 0, n_ctx_rows, n_ctx_rows, t_lat, 5)
    y_lat = _final(x1, wt, dest, ys, mod, final_norm_w, n_ctx_rows, n_lat * t_lat, n_ctx_rows, t_lat, 5)

    return (y_ctx.reshape(n_ctx, t_ctx, d_model), y_lat.reshape(n_lat, t_lat, d_model),
            new_c[:, None], new_n[:, None], new_m[:, None, :, :, 0], new_s[:, None])
```
